```python
import math
import jax, jax.numpy as jnp
from jax import lax
import numpy as np

D_MODEL = 2048
BATCH = 16
SEQ = 256
DEPTH = 2
DEC_BATCH = 8
DEC_SEQ = 2048
PAST_LEN = 256

GRID_W = 64
N_MIXERS = 2
N_ATTN_LAYERS = (DEPTH + 1) // 2
N_SSM_LAYERS = DEPTH // 2
N_HEADS = 8
HEAD_DIM = 128
V_DIM = 2 * HEAD_DIM
ROPE_THETA = 10000.0
Q_BLOCK = 128
SSM_GROUP = 16
N_GROUPS = D_MODEL // SSM_GROUP
STATE_DIM = 64
D_FF = 5632
NORM_EPS = 1e-6
SUBLN_EPS = 1e-5
DT_MIN = 1e-3
DT_MAX = 1e-1

kernel_name = 'hybrid_diffattn_s5_prefix_step'

F32 = jnp.float32


def rms_norm(x, g, eps=NORM_EPS):
    xf = x.astype(F32)
    y = xf * lax.rsqrt(jnp.mean(xf * xf, axis=-1, keepdims=True) + eps)
    return (y * g.astype(F32)).astype(x.dtype)


def modulation(cond, w_mod, b_mod):
    m = jax.nn.silu(cond) @ w_mod + b_mod
    m = m.reshape(cond.shape[:-1] + (1, 6, D_MODEL))
    return [m[..., j, :] for j in range(6)]


def lambda_init_fn(layer):
    return 0.8 - 0.6 * math.exp(-0.3 * layer)


def axial_rope_tables(length):
    rows = length // GRID_W
    r, col = jnp.meshgrid(jnp.arange(rows), jnp.arange(GRID_W), indexing='ij')
    r = r.reshape(-1).astype(F32)
    col = col.reshape(-1).astype(F32)
    half = HEAD_DIM // 2
    inv = ROPE_THETA ** (-jnp.arange(0, half, 2, dtype=F32) / half)
    ar = r[:, None] * inv
    ac = col[:, None] * inv
    ang = jnp.concatenate([ar, ar, ac, ac], axis=-1)
    return jnp.cos(ang), jnp.sin(ang)


def apply_axial_rope(x, cos, sin):
    q4 = HEAD_DIM // 4
    xf = x.astype(F32)
    xr = xf.reshape(xf.shape[:-1] + (2, 2, q4))
    rot = jnp.stack([-xr[..., 1, :], xr[..., 0, :]], axis=-2).reshape(xf.shape)
    c = cos[None, :, None, None, :]
    s = sin[None, :, None, None, :]
    return (xf * c + rot * s).astype(x.dtype)


def diff_qkv(xm, w_qkv):
    b, l, _ = xm.shape
    q, k, v = jnp.split(xm @ w_qkv, 3, axis=-1)
    return (q.reshape(b, l, N_HEADS, 2, HEAD_DIM),
            k.reshape(b, l, N_HEADS, 2, HEAD_DIM),
            v.reshape(b, l, N_HEADS, V_DIM))


def diff_lambda(lv, lam_init):
    lv = lv.astype(F32)
    return jnp.exp(jnp.sum(lv[0] * lv[1])) - jnp.exp(jnp.sum(lv[2] * lv[3])) + lam_init


def diff_attention(q, k, v, lam):
    b, lq = q.shape[:2]
    nb = lq // Q_BLOCK
    qb = q.reshape(b, nb, Q_BLOCK, N_HEADS, 2, HEAD_DIM).transpose(1, 0, 2, 3, 4, 5)
    scale = HEAD_DIM ** -0.5

    def one_block(qblk):
        s = jnp.einsum('bqhcd,bkhcd->bhcqk', qblk, k).astype(F32) * scale
        p = jax.nn.softmax(s, axis=-1)
        w = p[:, :, 0] - lam * p[:, :, 1]
        return jnp.einsum('bhqk,bkhe->bqhe', w.astype(v.dtype), v)

    o = lax.map(one_block, qb)
    return o.transpose(1, 0, 2, 3, 4).reshape(b, lq, N_HEADS, V_DIM)


def diff_attn_out(o, subln_g, lam_init, w_o):
    b, l = o.shape[:2]
    o = rms_norm(o, subln_g, SUBLN_EPS) * (1.0 - lam_init)
    return o.reshape(b, l, N_HEADS * V_DIM) @ w_o


def cmul(ar, ai, br, bi):
    return ar * br - ai * bi, ar * bi + ai * br


def s5_discretize(a_re, a_im, log_step, b_re, b_im):
    dt = jnp.exp(log_step.astype(F32))[:, None]
    a_re = a_re.astype(F32)
    a_im = a_im.astype(F32)
    mag = jnp.exp(a_re * dt)
    lb_re = mag * jnp.cos(a_im * dt)
    lb_im = mag * jnp.sin(a_im * dt)
    num_re = lb_re - 1.0
    num_im = lb_im
    den = a_re * a_re + a_im * a_im
    f_re = (num_re * a_re + num_im * a_im) / den
    f_im = (num_im * a_re - num_re * a_im) / den
    bb_re, bb_im = cmul(f_re[..., None], f_im[..., None], b_re.astype(F32), b_im.astype(F32))
    return lb_re, lb_im, bb_re, bb_im


def s5_scan(u, lb_re, lb_im, bb_re, bb_im, c_re, c_im, h0, reverse):
    bu_re = jnp.einsum('blgc,gpc->blgp', u, bb_re)
    bu_im = jnp.einsum('blgc,gpc->blgp', u, bb_im)
    if reverse:
        bu_re = jnp.flip(bu_re, axis=1)
        bu_im = jnp.flip(bu_im, axis=1)
    if h0 is not None:
        ir, ii = cmul(lb_re, lb_im, h0[0].astype(F32), h0[1].astype(F32))
        bu_re = bu_re.at[:, 0].add(ir)
        bu_im = bu_im.at[:, 0].add(ii)
    l = u.shape[1]
    a_re = jnp.broadcast_to(lb_re, (1, l) + lb_re.shape)
    a_im = jnp.broadcast_to(lb_im, (1, l) + lb_im.shape)

    def combine(e1, e2):
        a1r, a1i, b1r, b1i = e1
        a2r, a2i, b2r, b2i = e2
        ar, ai = cmul(a2r, a2i, a1r, a1i)
        br, bi = cmul(a2r, a2i, b1r, b1i)
        return (ar, ai, br + b2r, bi + b2i)

    _, _, h_re, h_im = lax.associative_scan(combine, (a_re, a_im, bu_re, bu_im), axis=1)
    final = (h_re[:, -1], h_im[:, -1])
    if reverse:
        h_re = jnp.flip(h_re, axis=1)
        h_im = jnp.flip(h_im, axis=1)
    y = (jnp.einsum('blgp,gcp->blgc', h_re, c_re.astype(F32))
         - jnp.einsum('blgp,gcp->blgc', h_im, c_im.astype(F32)))
    return y, final


def s5_mixer(xm, a_re, a_im, log_step, b_re, b_im, c_re, c_im, d_skip, w_glu, b_glu, h0_re, h0_im):
    b, l, _ = xm.shape
    u = xm.reshape(b, l, N_GROUPS, SSM_GROUP)
    y = None
    fin_re = []
    fin_im = []
    for d in range(2):
        lb_re, lb_im, bb_re, bb_im = s5_discretize(a_re[d], a_im[d], log_step[d], b_re[d], b_im[d])
        h0 = None if h0_re is None else (h0_re[:, d], h0_im[:, d])
        yd, (fr, fi) = s5_scan(u, lb_re, lb_im, bb_re, bb_im, c_re[d], c_im[d], h0, d == 1)
        y = yd if y is None else y + yd
        fin_re.append(fr)
        fin_im.append(fi)
    z = y.reshape(b, l, D_MODEL) + d_skip.astype(F32) * xm.astype(F32)
    g = jax.nn.gelu(z)
    out = g * jax.nn.sigmoid(g @ w_glu.astype(F32) + b_glu.astype(F32))
    return out.astype(xm.dtype), jnp.stack(fin_re, axis=1), jnp.stack(fin_im, axis=1)


def conv_ffn(xm, w_up, conv_w, conv_b, w_down):
    h = xm @ w_up
    hp = jnp.pad(h, ((0, 0), (1, 1), (0, 0)))
    h = hp[:, :-2] * conv_w[0] + hp[:, 1:-1] * conv_w[1] + hp[:, 2:] * conv_w[2] + conv_b
    gate, val = jnp.split(h, 2, axis=-1)
    return (jax.nn.silu(gate) * val) @ w_down


def setup_inputs(seed: int = 0) -> dict:
    key = jax.random.key(seed)
    ks = jax.random.split(key, 32)
    nrm = jax.random.normal
    D = D_MODEL
    NA, NS = N_ATTN_LAYERS, N_SSM_LAYERS
    a_im0 = math.pi * jnp.arange(STATE_DIM, dtype=F32)
    return {
        'x_prompt': nrm(ks[0], (BATCH, SEQ, D), F32),
        'x_sample': nrm(ks[1], (DEC_BATCH, DEC_SEQ, D), F32),
        'cache_k': nrm(ks[2], (DEC_BATCH, NA, PAST_LEN, N_HEADS, 2, HEAD_DIM), F32),
        'cache_v': nrm(ks[3], (DEC_BATCH, NA, PAST_LEN, N_HEADS, V_DIM), F32),
        'state_re': 0.3 * nrm(ks[4], (DEC_BATCH, NS, 2, N_GROUPS, STATE_DIM), F32),
        'state_im': 0.3 * nrm(ks[5], (DEC_BATCH, NS, 2, N_GROUPS, STATE_DIM), F32),
        'c': nrm(ks[6], (DEC_BATCH, D), F32),
        'c_ctx': nrm(ks[7], (D,), F32),
        'w_mod': 0.5 * D ** -0.5 * nrm(ks[8], (DEPTH, D, 6 * D), F32),
        'b_mod': 0.01 * nrm(ks[9], (DEPTH, 6 * D), F32),
        'norm_g': 1.0 + 0.02 * nrm(ks[10], (DEPTH, 2, D), F32),
        'w_qkv': D ** -0.5 * nrm(ks[11], (NA, D, 3 * D), F32),
        'lam_vecs': 0.1 * nrm(ks[12], (NA, 4, HEAD_DIM), F32),
        'subln_g': 1.0 + 0.02 * nrm(ks[13], (NA, V_DIM), F32),
        'w_o': D ** -0.5 * nrm(ks[14], (NA, D, D), F32),
        'ssm_a_re': -0.5 * (1.0 + 0.02 * nrm(ks[15], (NS, 2, N_GROUPS, STATE_DIM), F32)),
        'ssm_a_im': a_im0 + 0.01 * nrm(ks[16], (NS, 2, N_GROUPS, STATE_DIM), F32),
        'ssm_log_step': jax.random.uniform(ks[17], (NS, 2, N_GROUPS), F32, math.log(DT_MIN), math.log(DT_MAX)),
        'ssm_b_re': (2 * SSM_GROUP) ** -0.5 * nrm(ks[18], (NS, 2, N_GROUPS, STATE_DIM, SSM_GROUP), F32),
        'ssm_b_im': (2 * SSM_GROUP) ** -0.5 * nrm(ks[19], (NS, 2, N_GROUPS, STATE_DIM, SSM_GROUP), F32),
        'ssm_c_re': (2 * STATE_DIM) ** -0.5 * nrm(ks[20], (NS, 2, N_GROUPS, SSM_GROUP, STATE_DIM), F32),
        'ssm_c_im': (2 * STATE_DIM) ** -0.5 * nrm(ks[21], (NS, 2, N_GROUPS, SSM_GROUP, STATE_DIM), F32),
        'ssm_d': nrm(ks[22], (NS, D), F32),
        'w_glu': D ** -0.5 * nrm(ks[23], (NS, D, D), F32),
        'b_glu': 0.01 * nrm(ks[24], (NS, D), F32),
        'w_up': D ** -0.5 * nrm(ks[25], (DEPTH, D, 2 * D_FF), F32),
        'conv_w': 3 ** -0.5 * nrm(ks[26], (DEPTH, 3, 2 * D_FF), F32),
        'conv_b': 0.01 * nrm(ks[27], (DEPTH, 2 * D_FF), F32),
        'w_down': D_FF ** -0.5 * nrm(ks[28], (DEPTH, D_FF, D), F32),
        'final_g': 1.0 + 0.02 * nrm(ks[29], (D,), F32),
    }


def reference(x_prompt, x_sample, cache_k, cache_v, state_re, state_im, c, c_ctx,
              w_mod, b_mod, norm_g, w_qkv, lam_vecs, subln_g, w_o,
              ssm_a_re, ssm_a_im, ssm_log_step, ssm_b_re, ssm_b_im, ssm_c_re, ssm_c_im,
              ssm_d, w_glu, b_glu, w_up, conv_w, conv_b, w_down, final_g):
    xp = x_prompt
    xs = x_sample
    cos, sin = axial_rope_tables(xs.shape[1])
    new_k, new_v, new_sre, new_sim = [], [], [], []
    for i in range(DEPTH):
        mix = i % N_MIXERS
        slot = i // N_MIXERS
        sh1p, sc1p, g1p, sh2p, sc2p, g2p = modulation(c_ctx, w_mod[i], b_mod[i])
        sh1s, sc1s, g1s, sh2s, sc2s, g2s = modulation(c, w_mod[i], b_mod[i])
        hp = rms_norm(xp, norm_g[i, 0]) * (1.0 + sc1p) + sh1p
        hs = rms_norm(xs, norm_g[i, 0]) * (1.0 + sc1s) + sh1s
        if mix == 0:
            lam_init = lambda_init_fn(i)
            lam = diff_lambda(lam_vecs[slot], lam_init)
            qp, kp, vp = diff_qkv(hp, w_qkv[slot])
            mp = diff_attn_out(diff_attention(qp, kp, vp, lam), subln_g[slot], lam_init, w_o[slot])
            new_k.append(kp)
            new_v.append(vp)
            qs, ks_, vs = diff_qkv(hs, w_qkv[slot])
            qs = apply_axial_rope(qs, cos, sin)
            ks_ = apply_axial_rope(ks_, cos, sin)
            k_all = jnp.concatenate([ks_, cache_k[:, slot].astype(ks_.dtype)], axis=1)
            v_all = jnp.concatenate([vs, cache_v[:, slot].astype(vs.dtype)], axis=1)
            ms = diff_attn_out(diff_attention(qs, k_all, v_all, lam), subln_g[slot], lam_init, w_o[slot])
        else:
            mp, fre, fim = s5_mixer(hp, ssm_a_re[slot], ssm_a_im[slot], ssm_log_step[slot],
                                    ssm_b_re[slot], ssm_b_im[slot], ssm_c_re[slot], ssm_c_im[slot],
                                    ssm_d[slot], w_glu[slot], b_glu[slot], None, None)
            new_sre.append(fre)
            new_sim.append(fim)
            ms, _, _ = s5_mixer(hs, ssm_a_re[slot], ssm_a_im[slot], ssm_log_step[slot],
                                ssm_b_re[slot], ssm_b_im[slot], ssm_c_re[slot], ssm_c_im[slot],
                                ssm_d[slot], w_glu[slot], b_glu[slot],
                                state_re[:, slot], state_im[:, slot])
        xp = xp + (g1p * mp).astype(xp.dtype)
        xs = xs + (g1s * ms).astype(xs.dtype)
        hp = rms_norm(xp, norm_g[i, 1]) * (1.0 + sc2p) + sh2p
        hs = rms_norm(xs, norm_g[i, 1]) * (1.0 + sc2s) + sh2s
        xp = xp + (g2p * conv_ffn(hp, w_up[i], conv_w[i], conv_b[i], w_down[i])).astype(xp.dtype)
        xs = xs + (g2s * conv_ffn(hs, w_up[i], conv_w[i], conv_b[i], w_down[i])).astype(xs.dtype)
    y_prompt = rms_norm(xp, final_g)
    y_sample = rms_norm(xs, final_g)
    new_cache_k = jnp.stack(new_k, axis=1)
    new_cache_v = jnp.stack(new_v, axis=1)
    new_state_re = jnp.stack(new_sre, axis=1)
    new_state_im = jnp.stack(new_sim, axis=1)
    return (y_prompt, y_sample, new_cache_k, new_cache_v, new_state_re, new_state_im)
```

```python
import functools
import math

import jax
import jax.numpy as jnp
from jax import lax
from jax.experimental import pallas as pl
from jax.experimental.pallas import tpu as pltpu

F32 = jnp.float32
BF16 = jnp.bfloat16

GRID_W = 64
ROPE_THETA = 10000.0
NORM_EPS = 1e-6
SUBLN_EPS = 1e-5
N_MIXERS = 2

LANE = 128
SUBLANE = 8
BF16_ROWS = 16
MXU_DIM = 256
VMEM_LIMIT_BYTES = 52 * 1024 * 1024


def _tile(n, pref, align):
    t = min(pref, n)
    t -= t % align
    while t >= align:
        if n % t == 0:
            return t
        t -= align
    return n


def _row_tile(m, seq_len, n_mod, pref, align):
    return _tile(m, pref, align) if n_mod == 1 else _tile(seq_len, pref, align)


def _mod_index(tm, seq_len, n_mod):
    return lambda i: (i * tm // seq_len) % n_mod


def _params(*sem):
    return pltpu.CompilerParams(dimension_semantics=sem, vmem_limit_bytes=VMEM_LIMIT_BYTES)


def _norm_mod(x, g, sc, sh):
    r = lax.rsqrt(jnp.mean(x * x, axis=-1, keepdims=True) + NORM_EPS)
    return x * r * (g * (1.0 + sc)) + sh


def _mod_kernel(c_ref, w_ref, b_ref, o_ref):
    c = c_ref[...]
    s = (c * jax.nn.sigmoid(c)).astype(BF16)
    o_ref[0] = jnp.dot(s, w_ref[0].astype(BF16), preferred_element_type=F32) + b_ref[0]


def _modulation(cond, w_mod, b_mod):
    depth, d, n = w_mod.shape
    rows = cond.shape[0]
    tn = _tile(n, 1024, LANE)
    return pl.pallas_call(
        _mod_kernel,
        grid=(depth, n // tn),
        in_specs=[pl.BlockSpec((rows, d), lambda l, j: (0, 0)),
                  pl.BlockSpec((1, d, tn), lambda l, j: (l, 0, j)),
                  pl.BlockSpec((1, 1, tn), lambda l, j: (l, 0, j))],
        out_specs=pl.BlockSpec((1, rows, tn), lambda l, j: (l, 0, j)),
        out_shape=jax.ShapeDtypeStruct((depth, rows, n), F32),
        compiler_params=_params("parallel", "parallel"),
        name="modulation",
    )(cond, w_mod, b_mod.reshape(depth, 1, n))


def _qkv_kernel(*refs, rope, n_rope_tiles, tn):
    if rope:
        x_ref, g_ref, sc_ref, sh_ref, w_ref, cos_ref, sin_ref, o_ref, xm_ref = refs
    else:
        x_ref, g_ref, sc_ref, sh_ref, w_ref, o_ref, xm_ref = refs
    j = pl.program_id(1)

    @pl.when(j == 0)
    def _():
        xm_ref[...] = _norm_mod(x_ref[...], g_ref[...], sc_ref[0], sh_ref[0]).astype(BF16)

    acc = jnp.dot(xm_ref[...], w_ref[...], preferred_element_type=F32)
    if not rope:
        o_ref[...] = acc.astype(o_ref.dtype)
        return

    @pl.when(j < n_rope_tiles)
    def _():
        cos = cos_ref[...]
        sin = sin_ref[...]
        lane = lax.broadcasted_iota(jnp.int32, cos.shape, 1)
        low_quarter = (lane % (LANE // 2)) < (LANE // 4)
        for c in range(tn // LANE):
            x = acc[:, c * LANE:(c + 1) * LANE]
            rot = jnp.where(low_quarter, pltpu.roll(x, LANE - LANE // 4, 1), pltpu.roll(x, LANE // 4, 1))
            o_ref[:, c * LANE:(c + 1) * LANE] = (x * cos + rot * sin).astype(o_ref.dtype)

    @pl.when(j >= n_rope_tiles)
    def _():
        o_ref[...] = acc.astype(o_ref.dtype)


def _qkv_proj(x, g, sc, sh, w, seq_len, out_dtype, rope_tables=None, n_rope_cols=0):
    m, d = x.shape
    n = w.shape[1]
    n_mod = sc.shape[0]
    tm = _tile(seq_len, 1024, BF16_ROWS) if rope_tables is not None else _row_tile(m, seq_len, n_mod, 1024, BF16_ROWS)
    tn = _tile(n_rope_cols if rope_tables is not None else n, 512, LANE)
    mod_i = _mod_index(tm, seq_len, n_mod)
    in_specs = [pl.BlockSpec((tm, d), lambda i, j: (i, 0)),
                pl.BlockSpec((1, d), lambda i, j: (0, 0)),
                pl.BlockSpec((1, 1, d), lambda i, j: (mod_i(i), 0, 0)),
                pl.BlockSpec((1, 1, d), lambda i, j: (mod_i(i), 0, 0)),
                pl.BlockSpec((d, tn), lambda i, j: (0, j))]
    args = [x, g, sc, sh, w]
    if rope_tables is not None:
        tiles_per_seq = seq_len // tm
        in_specs += [pl.BlockSpec((tm, LANE), lambda i, j: (i % tiles_per_seq, 0))] * 2
        args += list(rope_tables)
    return pl.pallas_call(
        functools.partial(_qkv_kernel, rope=rope_tables is not None,
                          n_rope_tiles=n_rope_cols // tn, tn=tn),
        grid=(m // tm, n // tn),
        in_specs=in_specs,
        out_specs=pl.BlockSpec((tm, tn), lambda i, j: (i, j)),
        out_shape=jax.ShapeDtypeStruct((m, n), out_dtype),
        scratch_shapes=[pltpu.VMEM((tm, d), BF16)],
        compiler_params=_params("parallel", "arbitrary"),
        name="qkv_proj",
    )(*args)


def _rope_tables(length, head_dim):
    pos = jnp.arange(length)
    r = (pos // GRID_W).astype(F32)
    col = (pos % GRID_W).astype(F32)
    half = head_dim // 2
    inv = ROPE_THETA ** (-jnp.arange(0, half, 2, dtype=F32) / half)
    ar = r[:, None] * inv
    ac = col[:, None] * inv
    ang = jnp.concatenate([ar, ar, ac, ac], axis=-1)
    sign = jnp.where((jnp.arange(head_dim) % half) < head_dim // 4, -1.0, 1.0).astype(F32)
    return jnp.cos(ang), jnp.sin(ang) * sign


def _attn_kernel(*refs, has_cache, lam_init, scale, head_dim):
    if has_cache:
        lam_ref, q_ref, k_ref, v_ref, ck_ref, cv_ref, g_ref, o_ref = refs
    else:
        lam_ref, q_ref, k_ref, v_ref, g_ref, o_ref = refs
    lv = lam_ref[...]
    lam = (jnp.exp(jnp.sum(lv[0:1] * lv[1:2], axis=-1, keepdims=True))
           - jnp.exp(jnp.sum(lv[2:3] * lv[3:4], axis=-1, keepdims=True)) + lam_init)
    q = q_ref[...].astype(BF16)
    ks = [k_ref[...].astype(BF16)]
    vs = [v_ref[...].astype(BF16)]
    if has_cache:
        ks.append(ck_ref[0, 0].astype(BF16))
        vs.append(cv_ref[0, 0].astype(BF16))
    outs = []
    for c in range(2):
        qc = q[:, c * head_dim:(c + 1) * head_dim]
        ss = [lax.dot_general(qc, k[:, c * head_dim:(c + 1) * head_dim],
                              (((1,), (1,)), ((), ())), preferred_element_type=F32) for k in ks]
        m = functools.reduce(jnp.maximum, [jnp.max(s, axis=-1, keepdims=True) for s in ss])
        ps = [jnp.exp((s - m) * scale) for s in ss]
        denom = functools.reduce(jnp.add, [jnp.sum(p, axis=-1, keepdims=True) for p in ps])
        acc = functools.reduce(jnp.add, [jnp.dot(p.astype(BF16), v, preferred_element_type=F32)
                                         for p, v in zip(ps, vs)])
        outs.append(acc / denom)
    o = outs[0] - lam * outs[1]
    r = lax.rsqrt(jnp.mean(o * o, axis=-1, keepdims=True) + SUBLN_EPS)
    o_ref[...] = (o * r * g_ref[...] * (1.0 - lam_init)).astype(o_ref.dtype)


def _diff_attention(qkv, lam_vecs, subln_g, lam_init, batch, seq_len, n_heads, head_dim,
                    cache_k=None, cache_v=None, slot=0):
    hw = 2 * head_dim
    tq = _tile(seq_len, 256, BF16_ROWS)
    nq = seq_len // tq
    has_cache = cache_k is not None
    in_specs = [pl.BlockSpec(lam_vecs.shape, lambda b, h, i: (0, 0)),
                pl.BlockSpec((tq, hw), lambda b, h, i: (b * nq + i, h)),
                pl.BlockSpec((seq_len, hw), lambda b, h, i: (b, n_heads + h)),
                pl.BlockSpec((seq_len, hw), lambda b, h, i: (b, 2 * n_heads + h))]
    args = [lam_vecs, qkv, qkv, qkv]
    if has_cache:
        past = cache_k.shape[2]
        in_specs += [pl.BlockSpec((1, 1, past, hw), lambda b, h, i: (b, slot, 0, h))] * 2
        args += [cache_k, cache_v]
    in_specs.append(pl.BlockSpec((1, hw), lambda b, h, i: (0, 0)))
    args.append(subln_g)
    return pl.pallas_call(
        functools.partial(_attn_kernel, has_cache=has_cache, lam_init=lam_init,
                          scale=head_dim ** -0.5, head_dim=head_dim),
        grid=(batch, n_heads, nq),
        in_specs=in_specs,
        out_specs=pl.BlockSpec((tq, hw), lambda b, h, i: (b * nq + i, h)),
        out_shape=jax.ShapeDtypeStruct((batch * seq_len, n_heads * hw), BF16),
        compiler_params=_params("parallel", "parallel", "arbitrary"),
        name="diff_attention",
    )(*args)


def _mm_res_kernel(a_ref, w_ref, x_ref, gate_ref, o_ref):
    acc = jnp.dot(a_ref[...], w_ref[...], preferred_element_type=F32)
    o_ref[...] = x_ref[...] + gate_ref[0] * acc


def _matmul_gate_residual(a, w, x, gate, seq_len):
    m, k = a.shape
    n = w.shape[1]
    tm = _row_tile(m, seq_len, gate.shape[0], 512, BF16_ROWS)
    tn = _tile(n, 512, LANE)
    mod_i = _mod_index(tm, seq_len, gate.shape[0])
    return pl.pallas_call(
        _mm_res_kernel,
        grid=(m // tm, n // tn),
        in_specs=[pl.BlockSpec((tm, k), lambda i, j: (i, 0)),
                  pl.BlockSpec((k, tn), lambda i, j: (0, j)),
                  pl.BlockSpec((tm, tn), lambda i, j: (i, j)),
                  pl.BlockSpec((1, 1, tn), lambda i, j: (mod_i(i), 0, j))],
        out_specs=pl.BlockSpec((tm, tn), lambda i, j: (i, j)),
        out_shape=jax.ShapeDtypeStruct((m, n), F32),
        compiler_params=_params("parallel", "arbitrary"),
        name="matmul_gate_residual",
    )(a, w, x, gate)


def _up_kernel(x_ref, xp_ref, xn_ref, g_ref, sc_ref, sh_ref, wg_ref, wv_ref,
               cwg_ref, cwv_ref, cbg_ref, cbv_ref, o_ref, xm_ref, *, tm, seq_len):
    halo = BF16_ROWS
    i = pl.program_id(0)
    j = pl.program_id(1)

    @pl.when(j == 0)
    def _():
        g, sc, sh = g_ref[...], sc_ref[0], sh_ref[0]
        xm_ref[0:halo] = _norm_mod(xp_ref[...], g, sc, sh).astype(BF16)
        xm_ref[halo:halo + tm] = _norm_mod(x_ref[...], g, sc, sh).astype(BF16)
        xm_ref[halo + tm:] = _norm_mod(xn_ref[...], g, sc, sh).astype(BF16)

    xm = xm_ref[...]
    pos = lax.rem(lax.broadcasted_iota(jnp.int32, (tm, 1), 0) + i * tm, seq_len)
    first = pos == 0
    last = pos == seq_len - 1

    def conv_branch(w_ref, cw_ref, cb_ref):
        h = jnp.dot(xm, w_ref[...], preferred_element_type=F32)
        h_prev = jnp.where(first, 0.0, pltpu.roll(h, 1, 0)[halo:halo + tm])
        h_next = jnp.where(last, 0.0, pltpu.roll(h, tm + 2 * halo - 1, 0)[halo:halo + tm])
        cw = cw_ref[...]
        return h_prev * cw[0:1] + h[halo:halo + tm] * cw[1:2] + h_next * cw[2:3] + cb_ref[...]

    gate = conv_branch(wg_ref, cwg_ref, cbg_ref)
    val = conv_branch(wv_ref, cwv_ref, cbv_ref)
    o_ref[...] = (gate * jax.nn.sigmoid(gate) * val).astype(o_ref.dtype)


def _conv_ffn_up(x, g, sc, sh, w_up, conv_w, conv_b, seq_len):
    m, d = x.shape
    d_ff = w_up.shape[1] // 2
    halo = BF16_ROWS
    tm = _row_tile(m, seq_len, sc.shape[0], 1024, halo)
    tn = _tile(d_ff, 512, LANE)
    nj = d_ff // tn
    mod_i = _mod_index(tm, seq_len, sc.shape[0])
    hb = tm // halo
    last_hb = m // halo - 1

    def mod_idx(vec):
        return lambda i, j: (mod_i(i), 0, 0)

    return pl.pallas_call(
        functools.partial(_up_kernel, tm=tm, seq_len=seq_len),
        grid=(m // tm, nj),
        in_specs=[pl.BlockSpec((tm, d), lambda i, j: (i, 0)),
                  pl.BlockSpec((halo, d), lambda i, j: (jnp.maximum(i * hb - 1, 0), 0)),
                  pl.BlockSpec((halo, d), lambda i, j: (jnp.minimum((i + 1) * hb, last_hb), 0)),
                  pl.BlockSpec((1, d), lambda i, j: (0, 0)),
                  pl.BlockSpec((1, 1, d), mod_idx(sc)),
                  pl.BlockSpec((1, 1, d), mod_idx(sh)),
                  pl.BlockSpec((d, tn), lambda i, j: (0, j)),
                  pl.BlockSpec((d, tn), lambda i, j: (0, nj + j)),
                  pl.BlockSpec((3, tn), lambda i, j: (0, j)),
                  pl.BlockSpec((3, tn), lambda i, j: (0, nj + j)),
                  pl.BlockSpec((1, tn), lambda i, j: (0, j)),
                  pl.BlockSpec((1, tn), lambda i, j: (0, nj + j))],
        out_specs=pl.BlockSpec((tm, tn), lambda i, j: (i, j)),
        out_shape=jax.ShapeDtypeStruct((m, d_ff), BF16),
        scratch_shapes=[pltpu.VMEM((tm + 2 * halo, d), BF16)],
        compiler_params=_params("parallel", "arbitrary"),
        name="conv_ffn_up",
    )(x, x, x, g, sc, sh, w_up, w_up, conv_w, conv_w, conv_b, conv_b)


def _discretize(a_re, a_im, log_step):
    dt = jnp.exp(log_step)
    mag = jnp.exp(a_re * dt)
    lb_re = mag * jnp.cos(a_im * dt)
    lb_im = mag * jnp.sin(a_im * dt)
    num_re = lb_re - 1.0
    den = a_re * a_re + a_im * a_im
    f_re = (num_re * a_re + lb_im * a_im) / den
    f_im = (lb_im * a_re - num_re * a_im) / den
    return lb_re, lb_im, f_re, f_im


def _s5_disc_kernel(a_re_ref, a_im_ref, ls_ref, ax_re_ref, ax_im_ref, lsx_ref, b_re_ref, b_im_ref,
                    lb_re_ref, lb_im_ref, bb_re_ref, bb_im_ref):
    lb_re, lb_im, _, _ = _discretize(a_re_ref[...], a_im_ref[...], ls_ref[...])
    lb_re_ref[...] = lb_re
    lb_im_ref[...] = lb_im
    _, _, f_re, f_im = _discretize(ax_re_ref[...], ax_im_ref[...], lsx_ref[...])
    b_re, b_im = b_re_ref[...], b_im_ref[...]
    bb_re_ref[...] = f_re * b_re - f_im * b_im
    bb_im_ref[...] = f_re * b_im + f_im * b_re


def _s5_discretize(a_re, a_im, log_step, b_re, b_im):
    two, g, p, c = b_re.shape
    rows = two * g
    a2 = lambda a: a.reshape(rows, p)
    ax = lambda a: jnp.repeat(a.reshape(rows, p), c, axis=-1)
    ls = jnp.broadcast_to(log_step.reshape(rows, 1), (rows, p))
    lsx = jnp.broadcast_to(log_step.reshape(rows, 1), (rows, p * c))
    outs = pl.pallas_call(
        _s5_disc_kernel,
        out_shape=[jax.ShapeDtypeStruct((rows, p), F32)] * 2 + [jax.ShapeDtypeStruct((rows, p * c), F32)] * 2,
        compiler_params=pltpu.CompilerParams(vmem_limit_bytes=VMEM_LIMIT_BYTES),
        name="s5_discretize",
    )(a2(a_re), a2(a_im), ls, ax(a_re), ax(a_im), lsx, b_re.reshape(rows, p * c), b_im.reshape(rows, p * c))
    return outs


def _s5_in_kernel(x_ref, g_ref, sc_ref, sh_ref, o_ref):
    o_ref[0] = _norm_mod(x_ref[0], g_ref[...], sc_ref[0], sh_ref[0]).astype(o_ref.dtype)


def _s5_input(x3, g, sc, sh):
    b, l, d = x3.shape
    tl = _tile(l, 256, BF16_ROWS)
    return pl.pallas_call(
        _s5_in_kernel,
        grid=(b, l // tl),
        in_specs=[pl.BlockSpec((1, tl, d), lambda n, t: (n, t, 0)),
                  pl.BlockSpec((1, d), lambda n, t: (0, 0)),
                  pl.BlockSpec((1, 1, d), lambda n, t: (n % sc.shape[0], 0, 0)),
                  pl.BlockSpec((1, 1, d), lambda n, t: (n % sh.shape[0], 0, 0))],
        out_specs=pl.BlockSpec((1, tl, d), lambda n, t: (n // SUBLANE, t, n % SUBLANE)),
        out_shape=jax.ShapeDtypeStruct((b // SUBLANE, l, SUBLANE * d), BF16),
        compiler_params=_params("parallel", "parallel"),
        name="s5_input",
    )(x3, g, sc, sh)


def _s5_scan_kernel(u_ref, wb_ref, wc_ref, lam_ref, h0_ref, y_ref, fin_ref, st_ref, bu_ref, *, tc, nt, sw):
    d = pl.program_id(1)
    t_idx = pl.program_id(3)

    @pl.when(t_idx == 0)
    def _():
        st_ref[...] = h0_ref[0, 0, 0]

    bu_ref[...] = jnp.dot(u_ref[0], wb_ref[0, 0], preferred_element_type=F32)
    lam = lam_ref[0, 0]
    a_re = jnp.broadcast_to(lam[:, :sw], (SUBLANE, sw))
    a_im = jnp.broadcast_to(lam[:, sw:], (SUBLANE, sw))

    def step(t, carry):
        h_re, h_im = carry
        row = pl.multiple_of((t + d * (tc - 1 - 2 * t)) * SUBLANE, SUBLANE)
        n_re = a_re * h_re - a_im * h_im + bu_ref[pl.ds(row, SUBLANE), 0:sw]
        n_im = a_re * h_im + a_im * h_re + bu_ref[pl.ds(row, SUBLANE), sw:2 * sw]
        bu_ref[pl.ds(row, SUBLANE), 0:sw] = n_re
        bu_ref[pl.ds(row, SUBLANE), sw:2 * sw] = n_im
        return n_re, n_im

    h_re, h_im = lax.fori_loop(0, tc, step, (st_ref[:, 0:sw], st_ref[:, sw:2 * sw]), unroll=2)
    st_ref[:, 0:sw] = h_re
    st_ref[:, sw:2 * sw] = h_im
    y_ref[0, 0] = jnp.dot(bu_ref[...].astype(BF16), wc_ref[0, 0], preferred_element_type=F32)

    @pl.when(t_idx == nt - 1)
    def _():
        fin_ref[0, 0, 0] = st_ref[...]


def _s5_scan(u, wb, wc, lam, h0):
    ns, rows, d = u.shape
    l = rows // SUBLANE
    ngb = d // MXU_DIM
    sw2 = wb.shape[-1]
    tc = _tile(l, 64, BF16_ROWS // SUBLANE)
    nt = l // tc

    def t_blk(dd, t):
        return t + dd * (nt - 1 - 2 * t)

    return pl.pallas_call(
        functools.partial(_s5_scan_kernel, tc=tc, nt=nt, sw=sw2 // 2),
        grid=(ns, 2, ngb, nt),
        in_specs=[pl.BlockSpec((1, tc * SUBLANE, MXU_DIM), lambda n, dd, gb, t: (n, t_blk(dd, t), gb)),
                  pl.BlockSpec((1, 1, MXU_DIM, sw2), lambda n, dd, gb, t: (dd, gb, 0, 0)),
                  pl.BlockSpec((1, 1, sw2, MXU_DIM), lambda n, dd, gb, t: (dd, gb, 0, 0)),
                  pl.BlockSpec((1, 1, 1, sw2), lambda n, dd, gb, t: (dd, gb, 0, 0)),
                  pl.BlockSpec((1, 1, 1, SUBLANE, sw2), lambda n, dd, gb, t: (n, dd, gb, 0, 0))],
        out_specs=[pl.BlockSpec((1, 1, tc * SUBLANE, MXU_DIM), lambda n, dd, gb, t: (dd, n, t_blk(dd, t), gb)),
                   pl.BlockSpec((1, 1, 1, SUBLANE, sw2), lambda n, dd, gb, t: (n, dd, gb, 0, 0))],
        out_shape=[jax.ShapeDtypeStruct((2, ns, rows, d), F32),
                   jax.ShapeDtypeStruct((ns, 2, ngb, SUBLANE, sw2), F32)],
        scratch_shapes=[pltpu.VMEM((SUBLANE, sw2), F32), pltpu.VMEM((tc * SUBLANE, sw2), F32)],
        compiler_params=_params("parallel", "parallel", "parallel", "arbitrary"),
        name="s5_scan",
    )(u, wb, wc, lam, h0)


def _s5_block_weights(lb_re, lb_im, bb_re, bb_im, c_re, c_im, n_groups, state_dim, group):
    gpb = MXU_DIM // group
    ngb = n_groups // gpb
    eye = jnp.eye(gpb, dtype=F32)

    def b_blocks(bb):
        bb = bb.reshape(2, ngb, gpb, state_dim, group)
        return jnp.einsum('dbgpc,gh->dbgchp', bb, eye).reshape(2, ngb, gpb * group, gpb * state_dim)

    def c_blocks(cc):
        cc = cc.reshape(2, ngb, gpb, group, state_dim)
        return jnp.einsum('dbgcp,gh->dbgphc', cc, eye).reshape(2, ngb, gpb * state_dim, gpb * group)

    wb = jnp.concatenate([b_blocks(bb_re), b_blocks(bb_im)], axis=-1).astype(BF16)
    wc = jnp.concatenate([c_blocks(c_re), -c_blocks(c_im)], axis=-2).astype(BF16)
    lam = jnp.concatenate([lb_re.reshape(2, ngb, 1, gpb * state_dim),
                           lb_im.reshape(2, ngb, 1, gpb * state_dim)], axis=-1)
    return wb, wc, lam


def _s5_state_blocks(s_re, s_im, ngb):
    b, two, g, p = s_re.shape
    sw = g * p // ngb

    def blk(s):
        return s.reshape(b // SUBLANE, SUBLANE, two, ngb, sw).transpose(0, 2, 3, 1, 4)

    return jnp.concatenate([blk(s_re), blk(s_im)], axis=-1)


def _s5_state_unblock(fin, g, p):
    ns, two, ngb, sub, sw2 = fin.shape
    sw = sw2 // 2

    def unblk(s):
        return s.transpose(0, 3, 1, 2, 4).reshape(ns * sub, two, g, p)

    return unblk(fin[..., :sw]), unblk(fin[..., sw:])


def _glu_kernel(x_ref, xj_ref, yf_ref, yb_ref, g_ref, sc_ref, sh_ref, dskip_ref, w_ref, b_ref, gate_ref,
                o_ref, gf_ref, gb_ref, *, tn):
    j = pl.program_id(1)

    @pl.when(j == 0)
    def _():
        xm = _norm_mod(x_ref[...], g_ref[...], sc_ref[0], sh_ref[0])
        z = yf_ref[0, 0] + yb_ref[0, 0] + dskip_ref[...] * xm
        gz = jax.nn.gelu(z)
        gb_ref[...] = gz.astype(BF16)
        for c in range(gf_ref.shape[0]):
            gf_ref[c] = gz[:, c * tn:(c + 1) * tn]

    acc = jnp.dot(gb_ref[...], w_ref[...], preferred_element_type=F32) + b_ref[...]
    o_ref[...] = xj_ref[...] + gate_ref[0] * (gf_ref[j] * jax.nn.sigmoid(acc))


def _s5_glu(x, y, g, sc, sh, d_skip, w_glu, b_glu, gate, seq_len):
    m, d = x.shape
    tm = _tile(seq_len, 256, BF16_ROWS)
    tn = _tile(d, 512, LANE)
    tiles_per_seq = seq_len // tm

    def y_idx(direction):
        def idx(i, j):
            b = i // tiles_per_seq
            return (direction, b // SUBLANE, i % tiles_per_seq, b % SUBLANE)
        return idx

    def mod_idx(vec):
        return lambda i, j: ((i // tiles_per_seq) % vec.shape[0], 0, 0)

    return pl.pallas_call(
        functools.partial(_glu_kernel, tn=tn),
        grid=(m // tm, d // tn),
        in_specs=[pl.BlockSpec((tm, d), lambda i, j: (i, 0)),
                  pl.BlockSpec((tm, tn), lambda i, j: (i, j)),
                  pl.BlockSpec((1, 1, tm, d), y_idx(0)),
                  pl.BlockSpec((1, 1, tm, d), y_idx(1)),
                  pl.BlockSpec((1, d), lambda i, j: (0, 0)),
                  pl.BlockSpec((1, 1, d), mod_idx(sc)),
                  pl.BlockSpec((1, 1, d), mod_idx(sh)),
                  pl.BlockSpec((1, d), lambda i, j: (0, 0)),
                  pl.BlockSpec((d, tn), lambda i, j: (0, j)),
                  pl.BlockSpec((1, tn), lambda i, j: (0, j)),
                  pl.BlockSpec((1, 1, tn), lambda i, j: ((i // tiles_per_seq) % gate.shape[0], 0, j))],
        out_specs=pl.BlockSpec((tm, tn), lambda i, j: (i, j)),
        out_shape=jax.ShapeDtypeStruct((m, d), F32),
        scratch_shapes=[pltpu.VMEM((d // tn, tm, tn), F32), pltpu.VMEM((tm, d), BF16)],
        compiler_params=_params("parallel", "arbitrary"),
        name="s5_glu",
    )(x, x, y, y, g, sc, sh, d_skip, w_glu, b_glu, gate)


def _final_norm_kernel(x_ref, g_ref, o_ref):
    x = x_ref[...]
    r = lax.rsqrt(jnp.mean(x * x, axis=-1, keepdims=True) + NORM_EPS)
    o_ref[...] = x * r * g_ref[...]


def _final_norm(x, g):
    m, d = x.shape
    tm = _tile(m, 512, SUBLANE)
    return pl.pallas_call(
        _final_norm_kernel,
        grid=(m // tm,),
        in_specs=[pl.BlockSpec((tm, d), lambda i: (i, 0)), pl.BlockSpec((1, d), lambda i: (0, 0))],
        out_specs=pl.BlockSpec((tm, d), lambda i: (i, 0)),
        out_shape=jax.ShapeDtypeStruct((m, d), F32),
        compiler_params=_params("parallel"),
        name="final_norm",
    )(x, g)


def kernel(x_prompt, x_sample, cache_k, cache_v, state_re, state_im, c, c_ctx, w_mod, b_mod, norm_g, w_qkv, lam_vecs, subln_g, w_o, ssm_a_re, ssm_a_im, ssm_log_step, ssm_b_re, ssm_b_im, ssm_c_re, ssm_c_im, ssm_d, w_glu, b_glu, w_up, conv_w, conv_b, w_down, final_g):
    bp, lp, d = x_prompt.shape
    bs, ls, _ = x_sample.shape
    depth = w_mod.shape[0]
    n_heads, head_dim = cache_k.shape[3], cache_k.shape[5]
    n_groups, state_dim, group = ssm_b_re.shape[2], ssm_b_re.shape[3], ssm_b_re.shape[4]
    assert bp % SUBLANE == 0 and bs % SUBLANE == 0 and d % MXU_DIM == 0 and MXU_DIM % group == 0
    assert 2 * head_dim == MXU_DIM and ls % GRID_W == 0

    pad_rows = -(bs + 1) % SUBLANE
    cond = jnp.concatenate([c, c_ctx[None], jnp.zeros((pad_rows, d), F32)], axis=0)
    mod = _modulation(cond, w_mod, b_mod).reshape(depth, cond.shape[0], 6, d)

    xp = x_prompt.reshape(bp * lp, d)
    xs = x_sample.reshape(bs * ls, d)
    streams = [dict(x=xp, batch=bp, seq=lp), dict(x=xs, batch=bs, seq=ls)]
    new_k, new_v, new_sre, new_sim = [], [], [], []
    rope = _rope_tables(ls, head_dim)

    for i in range(depth):
        mix, slot = i % N_MIXERS, i // N_MIXERS
        mods = [[mod[i, bs:bs + 1, j][:, None] for j in range(6)],
                [mod[i, :bs, j][:, None] for j in range(6)]]
        g1 = norm_g[i, 0][None]
        g2 = norm_g[i, 1][None]
        if mix == 0:
            lam_init = 0.8 - 0.6 * math.exp(-0.3 * i)
            wq = w_qkv[slot].astype(BF16)
            wo = w_o[slot].astype(BF16)
            sub_g = subln_g[slot][None]
            ck = cache_k.reshape(cache_k.shape[:3] + (-1,))
            cv = cache_v.reshape(cache_v.shape[:3] + (-1,))
            for s, (st, md) in enumerate(zip(streams, mods)):
                sh1, sc1, gt1 = md[0], md[1], md[2]
                if s == 0:
                    qkv = _qkv_proj(st['x'], g1, sc1, sh1, wq, st['seq'], F32)
                    nqk = n_heads * 2 * head_dim
                    new_k.append(qkv[:, nqk:2 * nqk].reshape(bp, lp, n_heads, 2, head_dim))
                    new_v.append(qkv[:, 2 * nqk:].reshape(bp, lp, n_heads, 2 * head_dim))
                    o = _diff_attention(qkv, lam_vecs[slot], sub_g, lam_init, st['batch'], st['seq'],
                                        n_heads, head_dim)
                else:
                    qkv = _qkv_proj(st['x'], g1, sc1, sh1, wq, st['seq'], BF16, rope_tables=rope,
                                    n_rope_cols=2 * n_heads * 2 * head_dim)
                    o = _diff_attention(qkv, lam_vecs[slot], sub_g, lam_init, st['batch'], st['seq'],
                                        n_heads, head_dim, cache_k=ck, cache_v=cv, slot=slot)
                st['x'] = _matmul_gate_residual(o, wo, st['x'], gt1, st['seq'])
        else:
            lb_re, lb_im, bb_re, bb_im = _s5_discretize(ssm_a_re[slot], ssm_a_im[slot], ssm_log_step[slot],
                                                        ssm_b_re[slot], ssm_b_im[slot])
            wb, wc, lam = _s5_block_weights(lb_re, lb_im, bb_re, bb_im, ssm_c_re[slot], ssm_c_im[slot],
                                            n_groups, state_dim, group)
            ngb = d // MXU_DIM
            wg = w_glu[slot].astype(BF16)
            for s, (st, md) in enumerate(zip(streams, mods)):
                sh1, sc1, gt1 = md[0], md[1], md[2]
                b, l = st['batch'], st['seq']
                u = _s5_input(st['x'].reshape(b, l, d), g1, sc1, sh1).reshape(b // SUBLANE, l * SUBLANE, d)
                if s == 0:
                    h0 = jnp.zeros((b // SUBLANE, 2, ngb, SUBLANE, 2 * MXU_DIM // group * state_dim), F32)
                else:
                    h0 = _s5_state_blocks(state_re[:, slot], state_im[:, slot], ngb)
                y, fin = _s5_scan(u, wb, wc, lam, h0)
                if s == 0:
                    f_re, f_im = _s5_state_unblock(fin, n_groups, state_dim)
                    new_sre.append(f_re)
                    new_sim.append(f_im)
                y = y.reshape(2, b // SUBLANE, l, SUBLANE * d)
                st['x'] = _s5_glu(st['x'], y, g1, sc1, sh1, ssm_d[slot][None], wg, b_glu[slot][None], gt1, l)
        wu = w_up[i].astype(BF16)
        wd = w_down[i].astype(BF16)
        for st, md in zip(streams, mods):
            sh2, sc2, gt2 = md[3], md[4], md[5]
            act = _conv_ffn_up(st['x'], g2, sc2, sh2, wu, conv_w[i], conv_b[i][None], st['seq'])
            st['x'] = _matmul_gate_residual(act, wd, st['x'], gt2, st['seq'])

    y_prompt = _final_norm(streams[0]['x'], final_g[None]).reshape(bp, lp, d)
    y_sample = _final_norm(streams[1]['x'], final_g[None]).reshape(bs, ls, d)
    return (y_prompt, y_sample, jnp.stack(new_k, axis=1), jnp.stack(new_v, axis=1),
            jnp.stack(new_sre, axis=1), jnp.stack(new_sim, axis=1))
```

```python
import functools
import math

import jax
import jax.numpy as jnp
from jax import lax
from jax.experimental import pallas as pl
from jax.experimental.pallas import tpu as pltpu

F32 = jnp.float32
BF16 = jnp.bfloat16

GRID_W = 64
ROPE_THETA = 10000.0
NORM_EPS = 1e-6
SUBLN_EPS = 1e-5
N_MIXERS = 2

LANE = 128
SUBLANE = 8
BF16_ROWS = 16
MXU_DIM = 256
VMEM_LIMIT_BYTES = 52 * 1024 * 1024


def _tile(n, pref, align):
    t = min(pref, n)
    t -= t % align
    while t >= align:
        if n % t == 0:
            return t
        t -= align
    return n


def _row_tile(m, seq_len, n_mod, pref, align):
    return _tile(m, pref, align) if n_mod == 1 else _tile(seq_len, pref, align)


def _params(*sem):
    return pltpu.CompilerParams(dimension_semantics=sem, vmem_limit_bytes=VMEM_LIMIT_BYTES)


def _rstd(x):
    return lax.rsqrt(jnp.mean(x * x, axis=-1, keepdims=True) + NORM_EPS)


def _norm_mod(x, g, sc, sh):
    return x * _rstd(x) * (g * (1.0 + sc)) + sh


def _per_seq(x, pattern, op):
    rows, n = x.shape
    return op(x.reshape(rows // SUBLANE, SUBLANE, n), pattern[None]).reshape(rows, n)


def _norm_mod_tm(x, g, sc, sh):
    scaled = _per_seq(x * _rstd(x), g * (1.0 + sc), jnp.multiply)
    return _per_seq(scaled, sh, jnp.add)


def _mod_kernel(c_ref, w_ref, b_ref, o_ref):
    c = c_ref[...]
    s = (c * jax.nn.sigmoid(c)).astype(BF16)
    o_ref[0] = jnp.dot(s, w_ref[0].astype(BF16), preferred_element_type=F32) + b_ref[0]


def _modulation(cond, w_mod, b_mod):
    depth, d, n = w_mod.shape
    rows = cond.shape[0]
    tn = _tile(n, 1024, LANE)
    return pl.pallas_call(
        _mod_kernel,
        grid=(depth, n // tn),
        in_specs=[pl.BlockSpec((rows, d), lambda l, j: (0, 0)),
                  pl.BlockSpec((1, d, tn), lambda l, j: (l, 0, j)),
                  pl.BlockSpec((1, 1, tn), lambda l, j: (l, 0, j))],
        out_specs=pl.BlockSpec((1, rows, tn), lambda l, j: (l, 0, j)),
        out_shape=jax.ShapeDtypeStruct((depth, rows, n), F32),
        compiler_params=_params("parallel", "parallel"),
        name="modulation",
    )(cond, w_mod, b_mod.reshape(depth, 1, n))


def _qkv_kernel(*refs, rope, n_rope_tiles, tn):
    if rope:
        x_ref, g_ref, sc_ref, sh_ref, w_ref, cos_ref, sin_ref, o_ref, xm_ref = refs
    else:
        x_ref, g_ref, sc_ref, sh_ref, w_ref, o_ref, xm_ref = refs
    j = pl.program_id(1)

    @pl.when(j == 0)
    def _():
        xm_ref[...] = _norm_mod(x_ref[...], g_ref[...], sc_ref[0], sh_ref[0]).astype(BF16)

    acc = jnp.dot(xm_ref[...], w_ref[...], preferred_element_type=F32)
    if not rope:
        o_ref[...] = acc.astype(o_ref.dtype)
        return

    @pl.when(j < n_rope_tiles)
    def _():
        cos = cos_ref[...]
        sin = sin_ref[...]
        lane = lax.broadcasted_iota(jnp.int32, cos.shape, 1)
        low_quarter = (lane % (LANE // 2)) < (LANE // 4)
        for c in range(tn // LANE):
            x = acc[:, c * LANE:(c + 1) * LANE]
            rot = jnp.where(low_quarter, pltpu.roll(x, LANE - LANE // 4, 1), pltpu.roll(x, LANE // 4, 1))
            o_ref[:, c * LANE:(c + 1) * LANE] = (x * cos + rot * sin).astype(o_ref.dtype)

    @pl.when(j >= n_rope_tiles)
    def _():
        o_ref[...] = acc.astype(o_ref.dtype)


def _qkv_proj(x, g, sc, sh, w, seq_len, out_dtype, rope_tables=None, n_rope_cols=0):
    m, d = x.shape
    n = w.shape[1]
    n_mod = sc.shape[0]
    tm = _tile(seq_len, 1024, BF16_ROWS) if rope_tables is not None else _row_tile(m, seq_len, n_mod, 1024, BF16_ROWS)
    tn = _tile(n_rope_cols if rope_tables is not None else n, 512, LANE)

    def mod_i(i):
        return (i * tm // seq_len) % n_mod

    in_specs = [pl.BlockSpec((tm, d), lambda i, j: (i, 0)),
                pl.BlockSpec((1, d), lambda i, j: (0, 0)),
                pl.BlockSpec((1, 1, d), lambda i, j: (mod_i(i), 0, 0)),
                pl.BlockSpec((1, 1, d), lambda i, j: (mod_i(i), 0, 0)),
                pl.BlockSpec((d, tn), lambda i, j: (0, j))]
    args = [x, g, sc, sh, w]
    if rope_tables is not None:
        tiles_per_seq = seq_len // tm
        in_specs += [pl.BlockSpec((tm, LANE), lambda i, j: (i % tiles_per_seq, 0))] * 2
        args += list(rope_tables)
    return pl.pallas_call(
        functools.partial(_qkv_kernel, rope=rope_tables is not None,
                          n_rope_tiles=n_rope_cols // tn, tn=tn),
        grid=(m // tm, n // tn),
        in_specs=in_specs,
        out_specs=pl.BlockSpec((tm, tn), lambda i, j: (i, j)),
        out_shape=jax.ShapeDtypeStruct((m, n), out_dtype),
        scratch_shapes=[pltpu.VMEM((tm, d), BF16)],
        compiler_params=_params("parallel", "arbitrary"),
        name="qkv_proj",
    )(*args)


def _rope_tables(length, head_dim):
    pos = jnp.arange(length)
    r = (pos // GRID_W).astype(F32)
    col = (pos % GRID_W).astype(F32)
    half = head_dim // 2
    inv = ROPE_THETA ** (-jnp.arange(0, half, 2, dtype=F32) / half)
    ar = r[:, None] * inv
    ac = col[:, None] * inv
    ang = jnp.concatenate([ar, ar, ac, ac], axis=-1)
    sign = jnp.where((jnp.arange(head_dim) % half) < head_dim // 4, -1.0, 1.0).astype(F32)
    return jnp.cos(ang), jnp.sin(ang) * sign


def _attn_kernel(*refs, has_cache, lam_init, scale, head_dim):
    if has_cache:
        lam_ref, q_ref, k_ref, v_ref, ck_ref, cv_ref, g_ref, o_ref = refs
    else:
        lam_ref, q_ref, k_ref, v_ref, g_ref, o_ref = refs
    lv = lam_ref[...]
    lam = (jnp.exp(jnp.sum(lv[0:1] * lv[1:2], axis=-1, keepdims=True))
           - jnp.exp(jnp.sum(lv[2:3] * lv[3:4], axis=-1, keepdims=True)) + lam_init)
    q = q_ref[...].astype(BF16)
    ks = [k_ref[...].astype(BF16)]
    vs = [v_ref[...].astype(BF16)]
    if has_cache:
        ks.append(ck_ref[0, 0].astype(BF16))
        vs.append(cv_ref[0, 0].astype(BF16))
    exp2_scale = scale * math.log2(math.e)
    outs = []
    for c in range(2):
        qc = q[:, c * head_dim:(c + 1) * head_dim]
        ss = [lax.dot_general(qc, k[:, c * head_dim:(c + 1) * head_dim],
                              (((1,), (1,)), ((), ())), preferred_element_type=F32) for k in ks]
        m = functools.reduce(jnp.maximum, [jnp.max(s, axis=-1, keepdims=True) for s in ss])
        ps = [jnp.exp2((s - m) * exp2_scale) for s in ss]
        denom = functools.reduce(jnp.add, [jnp.sum(p, axis=-1, keepdims=True) for p in ps])
        acc = functools.reduce(jnp.add, [jnp.dot(p.astype(BF16), v, preferred_element_type=F32)
                                         for p, v in zip(ps, vs)])
        outs.append(acc / denom)
    o = outs[0] - lam * outs[1]
    r = lax.rsqrt(jnp.mean(o * o, axis=-1, keepdims=True) + SUBLN_EPS)
    o_ref[...] = (o * r * g_ref[...] * (1.0 - lam_init)).astype(o_ref.dtype)


def _diff_attention(qkv, lam_vecs, subln_g, lam_init, batch, seq_len, n_heads, head_dim,
                    cache_k=None, cache_v=None, slot=0):
    hw = 2 * head_dim
    tq = _tile(seq_len, 512, BF16_ROWS)
    nq = seq_len // tq
    has_cache = cache_k is not None
    in_specs = [pl.BlockSpec(lam_vecs.shape, lambda b, h, i: (0, 0)),
                pl.BlockSpec((tq, hw), lambda b, h, i: (b * nq + i, h)),
                pl.BlockSpec((seq_len, hw), lambda b, h, i: (b, n_heads + h)),
                pl.BlockSpec((seq_len, hw), lambda b, h, i: (b, 2 * n_heads + h))]
    args = [lam_vecs, qkv, qkv, qkv]
    if has_cache:
        past = cache_k.shape[2]
        in_specs += [pl.BlockSpec((1, 1, past, hw), lambda b, h, i: (b, slot, 0, h))] * 2
        args += [cache_k, cache_v]
    in_specs.append(pl.BlockSpec((1, hw), lambda b, h, i: (0, 0)))
    args.append(subln_g)
    return pl.pallas_call(
        functools.partial(_attn_kernel, has_cache=has_cache, lam_init=lam_init,
                          scale=head_dim ** -0.5, head_dim=head_dim),
        grid=(batch, n_heads, nq),
        in_specs=in_specs,
        out_specs=pl.BlockSpec((tq, hw), lambda b, h, i: (b * nq + i, h)),
        out_shape=jax.ShapeDtypeStruct((batch * seq_len, n_heads * hw), BF16),
        compiler_params=_params("parallel", "parallel", "arbitrary"),
        name="diff_attention",
    )(*args)


def _attn_out_kernel(a_ref, w_ref, x_ref, gate_ref, o_ref):
    nb, tl, k = a_ref.shape
    acc = jnp.dot(a_ref[...].reshape(nb * tl, k), w_ref[...], preferred_element_type=F32)
    res = x_ref[...] + gate_ref[...] * acc.reshape(nb, tl, -1)
    o_ref[...] = pltpu.einshape("btn->tbn", res).reshape(tl * nb, -1)


def _attn_out_proj(o, w, x, gate, batch, seq_len):
    k = o.shape[1]
    n = w.shape[1]
    tl = _tile(seq_len, 64, BF16_ROWS)
    tn = _tile(n, 512, LANE)
    nl = seq_len // tl
    return pl.pallas_call(
        _attn_out_kernel,
        grid=(batch // SUBLANE, nl, n // tn),
        in_specs=[pl.BlockSpec((SUBLANE, tl, k), lambda s, t, j: (s, t, 0)),
                  pl.BlockSpec((k, tn), lambda s, t, j: (0, j)),
                  pl.BlockSpec((SUBLANE, tl, tn), lambda s, t, j: (s, t, j)),
                  pl.BlockSpec((SUBLANE, 1, tn), lambda s, t, j: (s, 0, j))],
        out_specs=pl.BlockSpec((tl * SUBLANE, tn), lambda s, t, j: (s * nl + t, j)),
        out_shape=jax.ShapeDtypeStruct((batch * seq_len, n), F32),
        compiler_params=_params("parallel", "parallel", "arbitrary"),
        name="attn_out_proj",
    )(o.reshape(batch, seq_len, k), w, x.reshape(batch, seq_len, n), gate)


def _to_batch_major_kernel(x_ref, o_ref):
    nb, tl, d = o_ref.shape
    o_ref[...] = pltpu.einshape("tbd->btd", x_ref[...].reshape(tl, nb, d))


def _to_batch_major(x, batch, seq_len):
    m, d = x.shape
    tl = _tile(seq_len, 64, SUBLANE)
    nl = seq_len // tl
    return pl.pallas_call(
        _to_batch_major_kernel,
        grid=(batch // SUBLANE, nl),
        in_specs=[pl.BlockSpec((tl * SUBLANE, d), lambda s, t: (s * nl + t, 0))],
        out_specs=pl.BlockSpec((SUBLANE, tl, d), lambda s, t: (s, t, 0)),
        out_shape=jax.ShapeDtypeStruct((batch, seq_len, d), F32),
        compiler_params=_params("parallel", "parallel"),
        name="to_batch_major",
    )(x).reshape(m, d)


def _mm_res_kernel(a_ref, w_ref, x_ref, gate_ref, o_ref):
    acc = jnp.dot(a_ref[...], w_ref[...], preferred_element_type=F32)
    o_ref[...] = x_ref[...] + _per_seq(acc, gate_ref[0], jnp.multiply)


def _matmul_gate_residual(a, w, x, gate, seq_len):
    m, k = a.shape
    n = w.shape[1]
    group_rows = seq_len * SUBLANE
    tm = _tile(group_rows, 1024, BF16_ROWS)
    tn = _tile(n, 512, LANE)
    return pl.pallas_call(
        _mm_res_kernel,
        grid=(m // tm, n // tn),
        in_specs=[pl.BlockSpec((tm, k), lambda i, j: (i, 0)),
                  pl.BlockSpec((k, tn), lambda i, j: (0, j)),
                  pl.BlockSpec((tm, tn), lambda i, j: (i, j)),
                  pl.BlockSpec((1, SUBLANE, tn), lambda i, j: (i * tm // group_rows, 0, j))],
        out_specs=pl.BlockSpec((tm, tn), lambda i, j: (i, j)),
        out_shape=jax.ShapeDtypeStruct((m, n), F32),
        compiler_params=_params("parallel", "arbitrary"),
        name="matmul_gate_residual",
    )(a, w, x, gate)


def _up_kernel(x_ref, xp_ref, xn_ref, g_ref, sc_ref, sh_ref, wg_ref, wv_ref,
               cwg_ref, cwv_ref, cbg_ref, cbv_ref, o_ref, xm_ref, *, tm, group_rows, chunk):
    halo = BF16_ROWS
    i = pl.program_id(0)
    j = pl.program_id(1)

    @pl.when(j == 0)
    def _():
        g, sc, sh = g_ref[...], sc_ref[0], sh_ref[0]
        xm_ref[0:halo] = _norm_mod_tm(xp_ref[...], g, sc, sh).astype(BF16)
        xm_ref[halo:halo + tm] = _norm_mod_tm(x_ref[...], g, sc, sh).astype(BF16)
        xm_ref[halo + tm:] = _norm_mod_tm(xn_ref[...], g, sc, sh).astype(BF16)

        @pl.when((i * tm) % group_rows == 0)
        def _():
            xm_ref[0:halo] = jnp.zeros((halo, xm_ref.shape[1]), BF16)

        @pl.when(((i + 1) * tm) % group_rows == 0)
        def _():
            xm_ref[halo + tm:] = jnp.zeros((halo, xm_ref.shape[1]), BF16)

    xm = xm_ref[...]

    def conv_branch(w_ref, cw_ref, cb_ref, cols):
        h = jnp.dot(xm, w_ref[:, cols], preferred_element_type=F32)
        cw = cw_ref[:, cols]
        lo = halo - SUBLANE
        return (h[lo:lo + tm] * cw[0:1] + h[halo:halo + tm] * cw[1:2]
                + h[halo + SUBLANE:halo + SUBLANE + tm] * cw[2:3] + cb_ref[:, cols])

    for c in range(o_ref.shape[1] // chunk):
        cols = slice(c * chunk, (c + 1) * chunk)
        gate = conv_branch(wg_ref, cwg_ref, cbg_ref, cols)
        val = conv_branch(wv_ref, cwv_ref, cbv_ref, cols)
        o_ref[:, cols] = (gate * jax.nn.sigmoid(gate) * val).astype(o_ref.dtype)


def _conv_ffn_up(x, g, sc, sh, w_up, conv_w, conv_b, seq_len):
    m, d = x.shape
    d_ff = w_up.shape[1] // 2
    halo = BF16_ROWS
    group_rows = seq_len * SUBLANE
    tm = _tile(group_rows, 1024, halo)
    tn = _tile(d_ff, 512, LANE)
    chunk = _tile(tn, MXU_DIM, LANE)
    nj = d_ff // tn
    hb = tm // halo
    last_hb = m // halo - 1

    def mod_idx(i, j):
        return (i * tm // group_rows, 0, 0)

    return pl.pallas_call(
        functools.partial(_up_kernel, tm=tm, group_rows=group_rows, chunk=chunk),
        grid=(m // tm, nj),
        in_specs=[pl.BlockSpec((tm, d), lambda i, j: (i, 0)),
                  pl.BlockSpec((halo, d), lambda i, j: (jnp.maximum(i * hb - 1, 0), 0)),
                  pl.BlockSpec((halo, d), lambda i, j: (jnp.minimum((i + 1) * hb, last_hb), 0)),
                  pl.BlockSpec((1, d), lambda i, j: (0, 0)),
                  pl.BlockSpec((1, SUBLANE, d), mod_idx),
                  pl.BlockSpec((1, SUBLANE, d), mod_idx),
                  pl.BlockSpec((d, tn), lambda i, j: (0, j)),
                  pl.BlockSpec((d, tn), lambda i, j: (0, nj + j)),
                  pl.BlockSpec((3, tn), lambda i, j: (0, j)),
                  pl.BlockSpec((3, tn), lambda i, j: (0, nj + j)),
                  pl.BlockSpec((1, tn), lambda i, j: (0, j)),
                  pl.BlockSpec((1, tn), lambda i, j: (0, nj + j))],
        out_specs=pl.BlockSpec((tm, tn), lambda i, j: (i, j)),
        out_shape=jax.ShapeDtypeStruct((m, d_ff), BF16),
        scratch_shapes=[pltpu.VMEM((tm + 2 * halo, d), BF16)],
        compiler_params=_params("parallel", "arbitrary"),
        name="conv_ffn_up",
    )(x, x, x, g, sc, sh, w_up, w_up, conv_w, conv_w, conv_b, conv_b)


def _discretize(a_re, a_im, log_step):
    dt = jnp.exp(log_step)
    mag = jnp.exp(a_re * dt)
    lb_re = mag * jnp.cos(a_im * dt)
    lb_im = mag * jnp.sin(a_im * dt)
    num_re = lb_re - 1.0
    den = a_re * a_re + a_im * a_im
    f_re = (num_re * a_re + lb_im * a_im) / den
    f_im = (lb_im * a_re - num_re * a_im) / den
    return lb_re, lb_im, f_re, f_im


def _s5_disc_kernel(a_re_ref, a_im_ref, ls_ref, ax_re_ref, ax_im_ref, lsx_ref, b_re_ref, b_im_ref,
                    lb_re_ref, lb_im_ref, bb_re_ref, bb_im_ref):
    lb_re, lb_im, _, _ = _discretize(a_re_ref[...], a_im_ref[...], ls_ref[...])
    lb_re_ref[...] = lb_re
    lb_im_ref[...] = lb_im
    _, _, f_re, f_im = _discretize(ax_re_ref[...], ax_im_ref[...], lsx_ref[...])
    b_re, b_im = b_re_ref[...], b_im_ref[...]
    bb_re_ref[...] = f_re * b_re - f_im * b_im
    bb_im_ref[...] = f_re * b_im + f_im * b_re


def _s5_discretize(a_re, a_im, log_step, b_re, b_im):
    two, g, p, c = b_re.shape
    rows = two * g
    a2 = lambda a: a.reshape(rows, p)
    ax = lambda a: jnp.repeat(a.reshape(rows, p), c, axis=-1)
    ls = jnp.broadcast_to(log_step.reshape(rows, 1), (rows, p))
    lsx = jnp.broadcast_to(log_step.reshape(rows, 1), (rows, p * c))
    return pl.pallas_call(
        _s5_disc_kernel,
        out_shape=[jax.ShapeDtypeStruct((rows, p), F32)] * 2 + [jax.ShapeDtypeStruct((rows, p * c), F32)] * 2,
        compiler_params=pltpu.CompilerParams(vmem_limit_bytes=VMEM_LIMIT_BYTES),
        name="s5_discretize",
    )(a2(a_re), a2(a_im), ls, ax(a_re), ax(a_im), lsx, b_re.reshape(rows, p * c), b_im.reshape(rows, p * c))


def _s5_in_kernel(x_ref, g_ref, sc_ref, sh_ref, o_ref):
    o_ref[...] = _norm_mod_tm(x_ref[...], g_ref[...], sc_ref[0], sh_ref[0]).astype(o_ref.dtype)


def _s5_input(x, g, sc, sh, seq_len):
    m, d = x.shape
    group_rows = seq_len * SUBLANE
    tm = _tile(group_rows, 512, BF16_ROWS)
    mod_spec = pl.BlockSpec((1, SUBLANE, d), lambda i: (i * tm // group_rows, 0, 0))
    return pl.pallas_call(
        _s5_in_kernel,
        grid=(m // tm,),
        in_specs=[pl.BlockSpec((tm, d), lambda i: (i, 0)), pl.BlockSpec((1, d), lambda i: (0, 0)),
                  mod_spec, mod_spec],
        out_specs=pl.BlockSpec((tm, d), lambda i: (i, 0)),
        out_shape=jax.ShapeDtypeStruct((m, d), BF16),
        compiler_params=_params("parallel"),
        name="s5_input",
    )(x, g, sc, sh)


def _s5_scan_kernel(u_ref, wb_ref, wc_ref, lam_ref, h0_ref, y_ref, fin_ref, st_ref, bu_ref, *, tc, nt, sw):
    d = pl.program_id(1)
    t_idx = pl.program_id(3)

    @pl.when(t_idx == 0)
    def _():
        st_ref[...] = h0_ref[0, 0, 0]

    bu_ref[...] = jnp.dot(u_ref[...], wb_ref[0, 0], preferred_element_type=F32)
    lam = lam_ref[0, 0]
    a_re = jnp.broadcast_to(lam[:, :sw], (SUBLANE, sw))
    a_im = jnp.broadcast_to(lam[:, sw:], (SUBLANE, sw))

    def step(t, carry):
        h_re, h_im = carry
        row = pl.multiple_of((t + d * (tc - 1 - 2 * t)) * SUBLANE, SUBLANE)
        n_re = a_re * h_re - a_im * h_im + bu_ref[pl.ds(row, SUBLANE), 0:sw]
        n_im = a_re * h_im + a_im * h_re + bu_ref[pl.ds(row, SUBLANE), sw:2 * sw]
        bu_ref[pl.ds(row, SUBLANE), 0:sw] = n_re
        bu_ref[pl.ds(row, SUBLANE), sw:2 * sw] = n_im
        return n_re, n_im

    h_re, h_im = lax.fori_loop(0, tc, step, (st_ref[:, 0:sw], st_ref[:, sw:2 * sw]), unroll=2)
    st_ref[:, 0:sw] = h_re
    st_ref[:, sw:2 * sw] = h_im
    y_ref[0] = jnp.dot(bu_ref[...].astype(BF16), wc_ref[0, 0], preferred_element_type=F32)

    @pl.when(t_idx == nt - 1)
    def _():
        fin_ref[0, 0, 0] = st_ref[...]


def _s5_scan(u, wb, wc, lam, h0, seq_len):
    rows, d = u.shape
    ns = rows // (seq_len * SUBLANE)
    ngb = d // MXU_DIM
    sw2 = wb.shape[-1]
    tc = _tile(seq_len, 64, BF16_ROWS // SUBLANE)
    nt = seq_len // tc

    def t_blk(n, dd, t):
        return n * nt + t + dd * (nt - 1 - 2 * t)

    return pl.pallas_call(
        functools.partial(_s5_scan_kernel, tc=tc, nt=nt, sw=sw2 // 2),
        grid=(ns, 2, ngb, nt),
        in_specs=[pl.BlockSpec((tc * SUBLANE, MXU_DIM), lambda n, dd, gb, t: (t_blk(n, dd, t), gb)),
                  pl.BlockSpec((1, 1, MXU_DIM, sw2), lambda n, dd, gb, t: (dd, gb, 0, 0)),
                  pl.BlockSpec((1, 1, sw2, MXU_DIM), lambda n, dd, gb, t: (dd, gb, 0, 0)),
                  pl.BlockSpec((1, 1, 1, sw2), lambda n, dd, gb, t: (dd, gb, 0, 0)),
                  pl.BlockSpec((1, 1, 1, SUBLANE, sw2), lambda n, dd, gb, t: (n, dd, gb, 0, 0))],
        out_specs=[pl.BlockSpec((1, tc * SUBLANE, MXU_DIM), lambda n, dd, gb, t: (dd, t_blk(n, dd, t), gb)),
                   pl.BlockSpec((1, 1, 1, SUBLANE, sw2), lambda n, dd, gb, t: (n, dd, gb, 0, 0))],
        out_shape=[jax.ShapeDtypeStruct((2, rows, d), F32),
                   jax.ShapeDtypeStruct((ns, 2, ngb, SUBLANE, sw2), F32)],
        scratch_shapes=[pltpu.VMEM((SUBLANE, sw2), F32), pltpu.VMEM((tc * SUBLANE, sw2), F32)],
        compiler_params=_params("parallel", "parallel", "parallel", "arbitrary"),
        name="s5_scan",
    )(u, wb, wc, lam, h0)


def _s5_block_weights(lb_re, lb_im, bb_re, bb_im, c_re, c_im, n_groups, state_dim, group):
    gpb = MXU_DIM // group
    ngb = n_groups // gpb
    eye = jnp.eye(gpb, dtype=F32)

    def b_blocks(bb):
        bb = bb.reshape(2, ngb, gpb, state_dim, group)
        return jnp.einsum('dbgpc,gh->dbgchp', bb, eye).reshape(2, ngb, gpb * group, gpb * state_dim)

    def c_blocks(cc):
        cc = cc.reshape(2, ngb, gpb, group, state_dim)
        return jnp.einsum('dbgcp,gh->dbgphc', cc, eye).reshape(2, ngb, gpb * state_dim, gpb * group)

    wb = jnp.concatenate([b_blocks(bb_re), b_blocks(bb_im)], axis=-1).astype(BF16)
    wc = jnp.concatenate([c_blocks(c_re), -c_blocks(c_im)], axis=-2).astype(BF16)
    lam = jnp.concatenate([lb_re.reshape(2, ngb, 1, gpb * state_dim),
                           lb_im.reshape(2, ngb, 1, gpb * state_dim)], axis=-1)
    return wb, wc, lam


def _s5_state_blocks(s_re, s_im, ngb):
    b, two, g, p = s_re.shape
    sw = g * p // ngb

    def blk(s):
        return s.reshape(b // SUBLANE, SUBLANE, two, ngb, sw).transpose(0, 2, 3, 1, 4)

    return jnp.concatenate([blk(s_re), blk(s_im)], axis=-1)


def _s5_state_unblock(fin, g, p):
    ns, two, ngb, sub, sw2 = fin.shape
    sw = sw2 // 2

    def unblk(s):
        return s.transpose(0, 3, 1, 2, 4).reshape(ns * sub, two, g, p)

    return unblk(fin[..., :sw]), unblk(fin[..., sw:])


def _glu_kernel(x_ref, yf_ref, yb_ref, g_ref, sc_ref, sh_ref, dskip_ref, w_ref, b_ref, gate_ref, o_ref):
    x = x_ref[...]
    xm = _norm_mod_tm(x, g_ref[...], sc_ref[0], sh_ref[0])
    gz = jax.nn.gelu(yf_ref[0] + yb_ref[0] + dskip_ref[...] * xm)
    acc = jnp.dot(gz.astype(BF16), w_ref[...], preferred_element_type=F32) + b_ref[...]
    o_ref[...] = x + _per_seq(gz * jax.nn.sigmoid(acc), gate_ref[0], jnp.multiply)


def _s5_glu(x, y, g, sc, sh, d_skip, w_glu, b_glu, gate, seq_len):
    m, d = x.shape
    group_rows = seq_len * SUBLANE
    tm = _tile(group_rows, 256, BF16_ROWS)
    mod_spec = pl.BlockSpec((1, SUBLANE, d), lambda i: (i * tm // group_rows, 0, 0))
    vec_spec = pl.BlockSpec((1, d), lambda i: (0, 0))
    return pl.pallas_call(
        _glu_kernel,
        grid=(m // tm,),
        in_specs=[pl.BlockSpec((tm, d), lambda i: (i, 0)),
                  pl.BlockSpec((1, tm, d), lambda i: (0, i, 0)),
                  pl.BlockSpec((1, tm, d), lambda i: (1, i, 0)),
                  vec_spec, mod_spec, mod_spec, vec_spec,
                  pl.BlockSpec((d, d), lambda i: (0, 0)),
                  vec_spec, mod_spec],
        out_specs=pl.BlockSpec((tm, d), lambda i: (i, 0)),
        out_shape=jax.ShapeDtypeStruct((m, d), F32),
        compiler_params=_params("parallel"),
        name="s5_glu",
    )(x, y, y, g, sc, sh, d_skip, w_glu, b_glu, gate)


def _final_norm_kernel(x_ref, g_ref, o_ref):
    nb, tl, d = o_ref.shape
    x = x_ref[...]
    y = x * _rstd(x) * g_ref[...]
    o_ref[...] = pltpu.einshape("tbd->btd", y.reshape(tl, nb, d))


def _final_norm(x, g, batch, seq_len):
    m, d = x.shape
    tl = _tile(seq_len, 64, SUBLANE)
    nl = seq_len // tl
    return pl.pallas_call(
        _final_norm_kernel,
        grid=(batch // SUBLANE, nl),
        in_specs=[pl.BlockSpec((tl * SUBLANE, d), lambda s, t: (s * nl + t, 0)),
                  pl.BlockSpec((1, d), lambda s, t: (0, 0))],
        out_specs=pl.BlockSpec((SUBLANE, tl, d), lambda s, t: (s, t, 0)),
        out_shape=jax.ShapeDtypeStruct((batch, seq_len, d), F32),
        compiler_params=_params("parallel", "parallel"),
        name="final_norm",
    )(x, g)


def kernel(x_prompt, x_sample, cache_k, cache_v, state_re, state_im, c, c_ctx, w_mod, b_mod, norm_g, w_qkv, lam_vecs, subln_g, w_o, ssm_a_re, ssm_a_im, ssm_log_step, ssm_b_re, ssm_b_im, ssm_c_re, ssm_c_im, ssm_d, w_glu, b_glu, w_up, conv_w, conv_b, w_down, final_g):
    bp, lp, d = x_prompt.shape
    bs, ls, _ = x_sample.shape
    depth = w_mod.shape[0]
    n_heads, head_dim = cache_k.shape[3], cache_k.shape[5]
    n_groups, state_dim, group = ssm_b_re.shape[2], ssm_b_re.shape[3], ssm_b_re.shape[4]
    assert bp % SUBLANE == 0 and bs % SUBLANE == 0 and d % MXU_DIM == 0 and MXU_DIM % group == 0
    assert 2 * head_dim == MXU_DIM and ls % GRID_W == 0

    pad_rows = -(bs + 1) % SUBLANE
    cond = jnp.concatenate([c, c_ctx[None], jnp.zeros((pad_rows, d), F32)], axis=0)
    mod = _modulation(cond, w_mod, b_mod).reshape(depth, cond.shape[0], 6, d)

    streams = [dict(x=x_prompt.reshape(bp * lp, d), batch=bp, seq=lp, time_major=False),
               dict(x=x_sample.reshape(bs * ls, d), batch=bs, seq=ls, time_major=False)]
    new_k, new_v, new_sre, new_sim = [], [], [], []
    rope = _rope_tables(ls, head_dim)
    ngb = d // MXU_DIM

    for i in range(depth):
        mix, slot = i % N_MIXERS, i // N_MIXERS
        mods = [[jnp.broadcast_to(mod[i, bs, j], (bp, d)) for j in range(6)],
                [mod[i, :bs, j] for j in range(6)]]
        g1 = norm_g[i, 0][None]
        g2 = norm_g[i, 1][None]

        def per_group(v):
            return v.reshape(v.shape[0] // SUBLANE, SUBLANE, d)

        if mix == 0:
            lam_init = 0.8 - 0.6 * math.exp(-0.3 * i)
            wq = w_qkv[slot].astype(BF16)
            wo = w_o[slot].astype(BF16)
            sub_g = subln_g[slot][None]
            ck = cache_k.reshape(cache_k.shape[:3] + (-1,))
            cv = cache_v.reshape(cache_v.shape[:3] + (-1,))
            for s, (st, md) in enumerate(zip(streams, mods)):
                b, l = st['batch'], st['seq']
                if st['time_major']:
                    st['x'] = _to_batch_major(st['x'], b, l)
                sh1, sc1 = (v[:1, None] if s == 0 else v[:, None] for v in (md[0], md[1]))
                if s == 0:
                    qkv = _qkv_proj(st['x'], g1, sc1, sh1, wq, l, F32)
                    nqk = n_heads * 2 * head_dim
                    new_k.append(qkv[:, nqk:2 * nqk].reshape(b, l, n_heads, 2, head_dim))
                    new_v.append(qkv[:, 2 * nqk:].reshape(b, l, n_heads, 2 * head_dim))
                    o = _diff_attention(qkv, lam_vecs[slot], sub_g, lam_init, b, l, n_heads, head_dim)
                else:
                    qkv = _qkv_proj(st['x'], g1, sc1, sh1, wq, l, BF16, rope_tables=rope,
                                    n_rope_cols=2 * n_heads * 2 * head_dim)
                    o = _diff_attention(qkv, lam_vecs[slot], sub_g, lam_init, b, l, n_heads, head_dim,
                                        cache_k=ck, cache_v=cv, slot=slot)
                st['x'] = _attn_out_proj(o, wo, st['x'], md[2][:, None], b, l)
                st['time_major'] = True
        else:
            lb_re, lb_im, bb_re, bb_im = _s5_discretize(ssm_a_re[slot], ssm_a_im[slot], ssm_log_step[slot],
                                                        ssm_b_re[slot], ssm_b_im[slot])
            wb, wc, lam = _s5_block_weights(lb_re, lb_im, bb_re, bb_im, ssm_c_re[slot], ssm_c_im[slot],
                                            n_groups, state_dim, group)
            wg = w_glu[slot].astype(BF16)
            for s, (st, md) in enumerate(zip(streams, mods)):
                b, l = st['batch'], st['seq']
                sh1, sc1, gt1 = per_group(md[0]), per_group(md[1]), per_group(md[2])
                u = _s5_input(st['x'], g1, sc1, sh1, l)
                if s == 0:
                    h0 = jnp.zeros((b // SUBLANE, 2, ngb, SUBLANE, 2 * (MXU_DIM // group) * state_dim), F32)
                else:
                    h0 = _s5_state_blocks(state_re[:, slot], state_im[:, slot], ngb)
                y, fin = _s5_scan(u, wb, wc, lam, h0, l)
                if s == 0:
                    f_re, f_im = _s5_state_unblock(fin, n_groups, state_dim)
                    new_sre.append(f_re)
                    new_sim.append(f_im)
                st['x'] = _s5_glu(st['x'], y, g1, sc1, sh1, ssm_d[slot][None], wg, b_glu[slot][None], gt1, l)
        wu = w_up[i].astype(BF16)
        wd = w_down[i].astype(BF16)
        for st, md in zip(streams, mods):
            sh2, sc2, gt2 = per_group(md[3]), per_group(md[4]), per_group(md[5])
            act = _conv_ffn_up(st['x'], g2, sc2, sh2, wu, conv_w[i], conv_b[i][None], st['seq'])
            st['x'] = _matmul_gate_residual(act, wd, st['x'], gt2, st['seq'])

    y_prompt = _final_norm(streams[0]['x'], final_g[None], bp, lp)
    y_sample = _final_norm(streams[1]['x'], final_g[None], bs, ls)
    return (y_prompt, y_sample, jnp.stack(new_k, axis=1), jnp.stack(new_v, axis=1),
            jnp.stack(new_sre, axis=1), jnp.stack(new_sim, axis=1))
```

```python
import functools
import math

import jax
import jax.numpy as jnp
from jax import lax
from jax.experimental import pallas as pl
from jax.experimental.pallas import tpu as pltpu

F32 = jnp.float32
BF16 = jnp.bfloat16

GRID_W = 64
ROPE_THETA = 10000.0
NORM_EPS = 1e-6
SUBLN_EPS = 1e-5
N_MIXERS = 2

LANE = 128
SUBLANE = 8
BF16_ROWS = 16
MXU_DIM = 256
VMEM_LIMIT_BYTES = 52 * 1024 * 1024


def _tile(n, pref, align):
    t = min(pref, n)
    t -= t % align
    while t >= align:
        if n % t == 0:
            return t
        t -= align
    return n


def _row_tile(m, seq_len, n_mod, pref, align):
    return _tile(m, pref, align) if n_mod == 1 else _tile(seq_len, pref, align)


def _params(*sem):
    return pltpu.CompilerParams(dimension_semantics=sem, vmem_limit_bytes=VMEM_LIMIT_BYTES)


def _rstd(x):
    return lax.rsqrt(jnp.mean(x * x, axis=-1, keepdims=True) + NORM_EPS)


def _norm_mod(x, g, sc, sh):
    return x * _rstd(x) * (g * (1.0 + sc)) + sh


def _per_seq(x, pattern, op):
    rows, n = x.shape
    return op(x.reshape(rows // SUBLANE, SUBLANE, n), pattern[None]).reshape(rows, n)


def _norm_mod_tm(x, g, sc, sh):
    scaled = _per_seq(x * _rstd(x), g * (1.0 + sc), jnp.multiply)
    return _per_seq(scaled, sh, jnp.add)


def _mod_kernel(c_ref, w_ref, b_ref, o_ref):
    c = c_ref[...]
    s = (c * jax.nn.sigmoid(c)).astype(BF16)
    o_ref[0] = jnp.dot(s, w_ref[0].astype(BF16), preferred_element_type=F32) + b_ref[0]


def _modulation(cond, w_mod, b_mod):
    depth, d, n = w_mod.shape
    rows = cond.shape[0]
    tn = _tile(n, 1024, LANE)
    return pl.pallas_call(
        _mod_kernel,
        grid=(depth, n // tn),
        in_specs=[pl.BlockSpec((rows, d), lambda l, j: (0, 0)),
                  pl.BlockSpec((1, d, tn), lambda l, j: (l, 0, j)),
                  pl.BlockSpec((1, 1, tn), lambda l, j: (l, 0, j))],
        out_specs=pl.BlockSpec((1, rows, tn), lambda l, j: (l, 0, j)),
        out_shape=jax.ShapeDtypeStruct((depth, rows, n), F32),
        compiler_params=_params("parallel", "parallel"),
        name="modulation",
    )(cond, w_mod, b_mod.reshape(depth, 1, n))


def _qkv_kernel(*refs, rope, n_rope_tiles, tn):
    if rope:
        x_ref, g_ref, sc_ref, sh_ref, w_ref, cos_ref, sin_ref, o_ref, xm_ref = refs
    else:
        x_ref, g_ref, sc_ref, sh_ref, w_ref, o_ref, xm_ref = refs
    j = pl.program_id(1)

    @pl.when(j == 0)
    def _():
        xm_ref[...] = _norm_mod(x_ref[...], g_ref[...], sc_ref[0], sh_ref[0]).astype(BF16)

    acc = jnp.dot(xm_ref[...], w_ref[...], preferred_element_type=F32)
    if not rope:
        o_ref[...] = acc.astype(o_ref.dtype)
        return

    @pl.when(j < n_rope_tiles)
    def _():
        cos = cos_ref[...]
        sin = sin_ref[...]
        lane = lax.broadcasted_iota(jnp.int32, cos.shape, 1)
        low_quarter = (lane % (LANE // 2)) < (LANE // 4)
        for c in range(tn // LANE):
            x = acc[:, c * LANE:(c + 1) * LANE]
            rot = jnp.where(low_quarter, pltpu.roll(x, LANE - LANE // 4, 1), pltpu.roll(x, LANE // 4, 1))
            o_ref[:, c * LANE:(c + 1) * LANE] = (x * cos + rot * sin).astype(o_ref.dtype)

    @pl.when(j >= n_rope_tiles)
    def _():
        o_ref[...] = acc.astype(o_ref.dtype)


def _qkv_proj(x, g, sc, sh, w, seq_len, out_dtype, rope_tables=None, n_rope_cols=0):
    m, d = x.shape
    n = w.shape[1]
    n_mod = sc.shape[0]
    tm = _tile(seq_len, 1024, BF16_ROWS) if rope_tables is not None else _row_tile(m, seq_len, n_mod, 1024, BF16_ROWS)
    tn = _tile(n_rope_cols if rope_tables is not None else n, 512, LANE)

    def mod_i(i):
        return (i * tm // seq_len) % n_mod

    in_specs = [pl.BlockSpec((tm, d), lambda i, j: (i, 0)),
                pl.BlockSpec((1, d), lambda i, j: (0, 0)),
                pl.BlockSpec((1, 1, d), lambda i, j: (mod_i(i), 0, 0)),
                pl.BlockSpec((1, 1, d), lambda i, j: (mod_i(i), 0, 0)),
                pl.BlockSpec((d, tn), lambda i, j: (0, j))]
    args = [x, g, sc, sh, w]
    if rope_tables is not None:
        tiles_per_seq = seq_len // tm
        in_specs += [pl.BlockSpec((tm, LANE), lambda i, j: (i % tiles_per_seq, 0))] * 2
        args += list(rope_tables)
    return pl.pallas_call(
        functools.partial(_qkv_kernel, rope=rope_tables is not None,
                          n_rope_tiles=n_rope_cols // tn, tn=tn),
        grid=(m // tm, n // tn),
        in_specs=in_specs,
        out_specs=pl.BlockSpec((tm, tn), lambda i, j: (i, j)),
        out_shape=jax.ShapeDtypeStruct((m, n), out_dtype),
        scratch_shapes=[pltpu.VMEM((tm, d), BF16)],
        compiler_params=_params("parallel", "arbitrary"),
        name="qkv_proj",
    )(*args)


def _rope_tables(length, head_dim):
    pos = jnp.arange(length)
    r = (pos // GRID_W).astype(F32)
    col = (pos % GRID_W).astype(F32)
    half = head_dim // 2
    inv = ROPE_THETA ** (-jnp.arange(0, half, 2, dtype=F32) / half)
    ar = r[:, None] * inv
    ac = col[:, None] * inv
    ang = jnp.concatenate([ar, ar, ac, ac], axis=-1)
    sign = jnp.where((jnp.arange(head_dim) % half) < head_dim // 4, -1.0, 1.0).astype(F32)
    return jnp.cos(ang), jnp.sin(ang) * sign


def _attn_kernel(*refs, has_cache, lam_init, scale, head_dim, kv_chunk):
    if has_cache:
        lam_ref, q_ref, k_ref, v_ref, ck_ref, cv_ref, g_ref, o_ref = refs
    else:
        lam_ref, q_ref, k_ref, v_ref, g_ref, o_ref = refs
    lv = lam_ref[...]
    lam = (jnp.exp(jnp.sum(lv[0:1] * lv[1:2], axis=-1, keepdims=True))
           - jnp.exp(jnp.sum(lv[2:3] * lv[3:4], axis=-1, keepdims=True)) + lam_init)
    q = q_ref[...].astype(BF16)
    ks = [k_ref[...].astype(BF16)]
    vs = [v_ref[...].astype(BF16)]
    if has_cache:
        ks.append(ck_ref[0, 0].astype(BF16))
        vs.append(cv_ref[0, 0].astype(BF16))
    exp2_scale = scale * math.log2(math.e)
    kv = [(k[r:r + kv_chunk], v[r:r + kv_chunk]) for k, v in zip(ks, vs) for r in range(0, k.shape[0], kv_chunk)]
    outs = []
    for c in range(2):
        qc = q[:, c * head_dim:(c + 1) * head_dim]
        m = denom = acc = None
        for k, v in kv:
            s = lax.dot_general(qc, k[:, c * head_dim:(c + 1) * head_dim],
                                (((1,), (1,)), ((), ())), preferred_element_type=F32)
            s_max = jnp.max(s, axis=-1, keepdims=True)
            m_new = s_max if m is None else jnp.maximum(m, s_max)
            p = jnp.exp2((s - m_new) * exp2_scale)
            p_sum = jnp.sum(p, axis=-1, keepdims=True)
            pv = jnp.dot(p.astype(BF16), v, preferred_element_type=F32)
            if m is None:
                denom, acc = p_sum, pv
            else:
                alpha = jnp.exp2((m - m_new) * exp2_scale)
                denom, acc = denom * alpha + p_sum, acc * alpha + pv
            m = m_new
        outs.append(acc / denom)
    o = outs[0] - lam * outs[1]
    r = lax.rsqrt(jnp.mean(o * o, axis=-1, keepdims=True) + SUBLN_EPS)
    o_ref[...] = (o * r * g_ref[...] * (1.0 - lam_init)).astype(o_ref.dtype)


def _diff_attention(qkv, lam_vecs, subln_g, lam_init, batch, seq_len, n_heads, head_dim,
                    cache_k=None, cache_v=None, slot=0):
    hw = 2 * head_dim
    tq = _tile(seq_len, 1024, BF16_ROWS)
    nq = seq_len // tq
    has_cache = cache_k is not None
    in_specs = [pl.BlockSpec(lam_vecs.shape, lambda b, h, i: (0, 0)),
                pl.BlockSpec((tq, hw), lambda b, h, i: (b * nq + i, h)),
                pl.BlockSpec((seq_len, hw), lambda b, h, i: (b, n_heads + h)),
                pl.BlockSpec((seq_len, hw), lambda b, h, i: (b, 2 * n_heads + h))]
    args = [lam_vecs, qkv, qkv, qkv]
    if has_cache:
        past = cache_k.shape[2]
        in_specs += [pl.BlockSpec((1, 1, past, hw), lambda b, h, i: (b, slot, 0, h))] * 2
        args += [cache_k, cache_v]
    in_specs.append(pl.BlockSpec((1, hw), lambda b, h, i: (0, 0)))
    args.append(subln_g)
    return pl.pallas_call(
        functools.partial(_attn_kernel, has_cache=has_cache, lam_init=lam_init,
                          scale=head_dim ** -0.5, head_dim=head_dim,
                          kv_chunk=_tile(seq_len, 512, BF16_ROWS)),
        grid=(batch, n_heads, nq),
        in_specs=in_specs,
        out_specs=pl.BlockSpec((tq, hw), lambda b, h, i: (b * nq + i, h)),
        out_shape=jax.ShapeDtypeStruct((batch * seq_len, n_heads * hw), BF16),
        compiler_params=_params("parallel", "parallel", "arbitrary"),
        name="diff_attention",
    )(*args)


def _attn_out_kernel(a_ref, w_ref, x_ref, gate_ref, o_ref):
    nb, tl, k = a_ref.shape
    acc = jnp.dot(a_ref[...].reshape(nb * tl, k), w_ref[...], preferred_element_type=F32)
    res = x_ref[...] + gate_ref[...] * acc.reshape(nb, tl, -1)
    o_ref[...] = pltpu.einshape("btn->tbn", res).reshape(tl * nb, -1)


def _attn_out_proj(o, w, x, gate, batch, seq_len):
    k = o.shape[1]
    n = w.shape[1]
    tl = _tile(seq_len, 128, BF16_ROWS)
    tn = _tile(n, 512, LANE)
    nl = seq_len // tl
    return pl.pallas_call(
        _attn_out_kernel,
        grid=(batch // SUBLANE, nl, n // tn),
        in_specs=[pl.BlockSpec((SUBLANE, tl, k), lambda s, t, j: (s, t, 0)),
                  pl.BlockSpec((k, tn), lambda s, t, j: (0, j)),
                  pl.BlockSpec((SUBLANE, tl, tn), lambda s, t, j: (s, t, j)),
                  pl.BlockSpec((SUBLANE, 1, tn), lambda s, t, j: (s, 0, j))],
        out_specs=pl.BlockSpec((tl * SUBLANE, tn), lambda s, t, j: (s * nl + t, j)),
        out_shape=jax.ShapeDtypeStruct((batch * seq_len, n), F32),
        compiler_params=_params("parallel", "parallel", "arbitrary"),
        name="attn_out_proj",
    )(o.reshape(batch, seq_len, k), w, x.reshape(batch, seq_len, n), gate)


def _to_batch_major_kernel(x_ref, o_ref):
    nb, tl, d = o_ref.shape
    o_ref[...] = pltpu.einshape("tbd->btd", x_ref[...].reshape(tl, nb, d))


def _to_batch_major(x, batch, seq_len):
    m, d = x.shape
    tl = _tile(seq_len, 64, SUBLANE)
    nl = seq_len // tl
    return pl.pallas_call(
        _to_batch_major_kernel,
        grid=(batch // SUBLANE, nl),
        in_specs=[pl.BlockSpec((tl * SUBLANE, d), lambda s, t: (s * nl + t, 0))],
        out_specs=pl.BlockSpec((SUBLANE, tl, d), lambda s, t: (s, t, 0)),
        out_shape=jax.ShapeDtypeStruct((batch, seq_len, d), F32),
        compiler_params=_params("parallel", "parallel"),
        name="to_batch_major",
    )(x).reshape(m, d)


def _mm_res_kernel(a_ref, w_ref, x_ref, gate_ref, o_ref):
    acc = jnp.dot(a_ref[...], w_ref[...], preferred_element_type=F32)
    o_ref[...] = x_ref[...] + _per_seq(acc, gate_ref[0], jnp.multiply)


def _matmul_gate_residual(a, w, x, gate, seq_len):
    m, k = a.shape
    n = w.shape[1]
    group_rows = seq_len * SUBLANE
    tm = _tile(group_rows, 1024, BF16_ROWS)
    tn = _tile(n, 512, LANE)
    return pl.pallas_call(
        _mm_res_kernel,
        grid=(m // tm, n // tn),
        in_specs=[pl.BlockSpec((tm, k), lambda i, j: (i, 0)),
                  pl.BlockSpec((k, tn), lambda i, j: (0, j)),
                  pl.BlockSpec((tm, tn), lambda i, j: (i, j)),
                  pl.BlockSpec((1, SUBLANE, tn), lambda i, j: (i * tm // group_rows, 0, j))],
        out_specs=pl.BlockSpec((tm, tn), lambda i, j: (i, j)),
        out_shape=jax.ShapeDtypeStruct((m, n), F32),
        compiler_params=_params("parallel", "arbitrary"),
        name="matmul_gate_residual",
    )(a, w, x, gate)


def _up_kernel(x_ref, xp_ref, xn_ref, g_ref, sc_ref, sh_ref, wg_ref, wv_ref,
               cwg_ref, cwv_ref, cbg_ref, cbv_ref, o_ref, xm_ref, *, tm, group_rows, chunk):
    halo = BF16_ROWS
    i = pl.program_id(0)
    j = pl.program_id(1)

    @pl.when(j == 0)
    def _():
        g, sc, sh = g_ref[...], sc_ref[0], sh_ref[0]
        xm_ref[0:halo] = _norm_mod_tm(xp_ref[...], g, sc, sh).astype(BF16)
        xm_ref[halo:halo + tm] = _norm_mod_tm(x_ref[...], g, sc, sh).astype(BF16)
        xm_ref[halo + tm:] = _norm_mod_tm(xn_ref[...], g, sc, sh).astype(BF16)

        @pl.when((i * tm) % group_rows == 0)
        def _():
            xm_ref[0:halo] = jnp.zeros((halo, xm_ref.shape[1]), BF16)

        @pl.when(((i + 1) * tm) % group_rows == 0)
        def _():
            xm_ref[halo + tm:] = jnp.zeros((halo, xm_ref.shape[1]), BF16)

    xm = xm_ref[...]

    def conv(h, cw_ref, cb_ref, cols):
        cw = cw_ref[:, cols]
        lo = halo - SUBLANE
        return (h[lo:lo + tm] * cw[0:1] + h[halo:halo + tm] * cw[1:2]
                + h[halo + SUBLANE:halo + SUBLANE + tm] * cw[2:3] + cb_ref[:, cols])

    for c in range(o_ref.shape[1] // chunk):
        cols = slice(c * chunk, (c + 1) * chunk)
        gate = conv(jnp.dot(xm, wg_ref[:, cols], preferred_element_type=F32), cwg_ref, cbg_ref, cols)
        val = conv(jnp.dot(xm, wv_ref[:, cols], preferred_element_type=F32), cwv_ref, cbv_ref, cols)
        o_ref[:, cols] = (gate * jax.nn.sigmoid(gate) * val).astype(o_ref.dtype)


def _conv_ffn_up(x, g, sc, sh, w_up, conv_w, conv_b, seq_len):
    m, d = x.shape
    d_ff = w_up.shape[1] // 2
    halo = BF16_ROWS
    group_rows = seq_len * SUBLANE
    tm = _tile(group_rows, 1024, halo)
    tn = _tile(d_ff, 512, LANE)
    chunk = _tile(tn, MXU_DIM, LANE)
    nj = d_ff // tn
    hb = tm // halo
    last_hb = m // halo - 1

    mod_spec = pl.BlockSpec((1, SUBLANE, d), lambda i, j: (i * tm // group_rows, 0, 0))
    return pl.pallas_call(
        functools.partial(_up_kernel, tm=tm, group_rows=group_rows, chunk=chunk),
        grid=(m // tm, nj),
        in_specs=[pl.BlockSpec((tm, d), lambda i, j: (i, 0)),
                  pl.BlockSpec((halo, d), lambda i, j: (jnp.maximum(i * hb - 1, 0), 0)),
                  pl.BlockSpec((halo, d), lambda i, j: (jnp.minimum((i + 1) * hb, last_hb), 0)),
                  pl.BlockSpec((1, d), lambda i, j: (0, 0)),
                  mod_spec, mod_spec,
                  pl.BlockSpec((d, tn), lambda i, j: (0, j)),
                  pl.BlockSpec((d, tn), lambda i, j: (0, nj + j)),
                  pl.BlockSpec((3, tn), lambda i, j: (0, j)),
                  pl.BlockSpec((3, tn), lambda i, j: (0, nj + j)),
                  pl.BlockSpec((1, tn), lambda i, j: (0, j)),
                  pl.BlockSpec((1, tn), lambda i, j: (0, nj + j))],
        out_specs=pl.BlockSpec((tm, tn), lambda i, j: (i, j)),
        out_shape=jax.ShapeDtypeStruct((m, d_ff), BF16),
        scratch_shapes=[pltpu.VMEM((tm + 2 * halo, d), BF16)],
        compiler_params=_params("parallel", "arbitrary"),
        name="conv_ffn_up",
    )(x, x, x, g, sc, sh, w_up, w_up, conv_w, conv_w, conv_b, conv_b)


def _discretize(a_re, a_im, log_step):
    dt = jnp.exp(log_step)
    mag = jnp.exp(a_re * dt)
    lb_re = mag * jnp.cos(a_im * dt)
    lb_im = mag * jnp.sin(a_im * dt)
    num_re = lb_re - 1.0
    den = a_re * a_re + a_im * a_im
    f_re = (num_re * a_re + lb_im * a_im) / den
    f_im = (lb_im * a_re - num_re * a_im) / den
    return lb_re, lb_im, f_re, f_im


def _s5_disc_kernel(a_re_ref, a_im_ref, ls_ref, ax_re_ref, ax_im_ref, lsx_ref, b_re_ref, b_im_ref,
                    lb_re_ref, lb_im_ref, bb_re_ref, bb_im_ref):
    lb_re, lb_im, _, _ = _discretize(a_re_ref[...], a_im_ref[...], ls_ref[...])
    lb_re_ref[...] = lb_re
    lb_im_ref[...] = lb_im
    _, _, f_re, f_im = _discretize(ax_re_ref[...], ax_im_ref[...], lsx_ref[...])
    b_re, b_im = b_re_ref[...], b_im_ref[...]
    bb_re_ref[...] = f_re * b_re - f_im * b_im
    bb_im_ref[...] = f_re * b_im + f_im * b_re


def _s5_discretize(a_re, a_im, log_step, b_re, b_im):
    two, g, p, c = b_re.shape
    rows = two * g
    a2 = lambda a: a.reshape(rows, p)
    ax = lambda a: jnp.repeat(a.reshape(rows, p), c, axis=-1)
    ls = jnp.broadcast_to(log_step.reshape(rows, 1), (rows, p))
    lsx = jnp.broadcast_to(log_step.reshape(rows, 1), (rows, p * c))
    return pl.pallas_call(
        _s5_disc_kernel,
        out_shape=[jax.ShapeDtypeStruct((rows, p), F32)] * 2 + [jax.ShapeDtypeStruct((rows, p * c), F32)] * 2,
        compiler_params=pltpu.CompilerParams(vmem_limit_bytes=VMEM_LIMIT_BYTES),
        name="s5_discretize",
    )(a2(a_re), a2(a_im), ls, ax(a_re), ax(a_im), lsx, b_re.reshape(rows, p * c), b_im.reshape(rows, p * c))


def _s5_in_kernel(x_ref, g_ref, sc_ref, sh_ref, o_ref):
    o_ref[...] = _norm_mod_tm(x_ref[...], g_ref[...], sc_ref[0], sh_ref[0]).astype(o_ref.dtype)


def _s5_input(x, g, sc, sh, seq_len):
    m, d = x.shape
    group_rows = seq_len * SUBLANE
    tm = _tile(group_rows, 512, BF16_ROWS)
    mod_spec = pl.BlockSpec((1, SUBLANE, d), lambda i: (i * tm // group_rows, 0, 0))
    return pl.pallas_call(
        _s5_in_kernel,
        grid=(m // tm,),
        in_specs=[pl.BlockSpec((tm, d), lambda i: (i, 0)), pl.BlockSpec((1, d), lambda i: (0, 0)),
                  mod_spec, mod_spec],
        out_specs=pl.BlockSpec((tm, d), lambda i: (i, 0)),
        out_shape=jax.ShapeDtypeStruct((m, d), BF16),
        compiler_params=_params("parallel"),
        name="s5_input",
    )(x, g, sc, sh)


def _s5_scan_kernel(u_ref, wb_ref, wc_ref, lam_ref, h0_ref, y_ref, fin_ref, st_ref, buf_ref,
                    *, nsub, nt, sub_t):
    d = pl.program_id(1)
    t_idx = pl.program_id(3)
    n_tiles = st_ref.shape[0]
    half = n_tiles // 2
    sr = sub_t * SUBLANE
    steps = sub_t // n_tiles

    @pl.when(t_idx == 0)
    def _():
        st_ref[...] = h0_ref[0, 0, 0]

    lam = jnp.broadcast_to(lam_ref[0, 0], st_ref.shape)
    a_re, a_im = lam[:half], lam[half:]

    def first_row(p):
        return pl.multiple_of((p + d * (nsub - 1 - 2 * p)) * sr, sr)

    def run_pass(p, state, do_bu, do_scan, do_ch):
        slot_bu, slot_scan, slot_ch = p % 3, (p + 2) % 3, (p + 1) % 3
        if do_ch:
            h_all = jnp.concatenate([buf_ref[slot_ch, i] for i in range(n_tiles)], axis=1)
            y_ref[0, pl.ds(first_row(p - 2), sr), :] = jnp.dot(h_all.astype(BF16), wc_ref[0, 0],
                                                               preferred_element_type=F32)
        if do_bu:
            u_sub = u_ref[pl.ds(first_row(p), sr), :]
        h_re, h_im = state
        for i in range(n_tiles):
            if do_bu:
                buf_ref[slot_bu, i] = jnp.dot(u_sub, wb_ref[0, 0, i], preferred_element_type=F32)
            if do_scan:
                for k in range(steps):
                    t = i * steps + k
                    row = pl.multiple_of((t + d * (sub_t - 1 - 2 * t)) * SUBLANE, SUBLANE)
                    bu = buf_ref[slot_scan, :, pl.ds(row, SUBLANE), :]
                    n_re = a_re * h_re - a_im * h_im + bu[:half]
                    n_im = a_re * h_im + a_im * h_re + bu[half:]
                    buf_ref[slot_scan, 0:half, pl.ds(row, SUBLANE), :] = n_re
                    buf_ref[slot_scan, half:, pl.ds(row, SUBLANE), :] = n_im
                    h_re, h_im = n_re, n_im
        return h_re, h_im

    state = (st_ref[0:half], st_ref[half:])
    steady = range(2, nsub)
    for p in range(min(2, nsub + 2)):
        state = run_pass(p, state, p < nsub, 1 <= p <= nsub, False)
    if len(steady) > 0:
        state = lax.fori_loop(steady.start, steady.stop,
                              lambda p, st: run_pass(p, st, True, True, True), state)
    for p in range(max(2, nsub), nsub + 2):
        state = run_pass(p, state, p < nsub, p <= nsub, True)
    st_ref[0:half] = state[0]
    st_ref[half:] = state[1]

    @pl.when(t_idx == nt - 1)
    def _():
        fin_ref[0, 0, 0] = st_ref[...]


def _s5_scan(u, wb, wc, lam, h0, seq_len):
    rows, d = u.shape
    ns = rows // (seq_len * SUBLANE)
    ngb = d // MXU_DIM
    n_tiles = wb.shape[2]
    sub_t = _tile(seq_len, 64, n_tiles)
    tc = _tile(seq_len, 8 * sub_t, sub_t)
    nt = seq_len // tc
    assert sub_t % n_tiles == 0 and tc % sub_t == 0

    def t_blk(n, dd, t):
        return n * nt + t + dd * (nt - 1 - 2 * t)

    w_spec = pl.BlockSpec((1, 1, n_tiles, MXU_DIM, MXU_DIM), lambda n, dd, gb, t: (dd, gb, 0, 0, 0))
    st_spec = pl.BlockSpec((1, 1, 1, n_tiles, SUBLANE, MXU_DIM), lambda n, dd, gb, t: (n, dd, gb, 0, 0, 0))
    return pl.pallas_call(
        functools.partial(_s5_scan_kernel, nsub=tc // sub_t, nt=nt, sub_t=sub_t),
        grid=(ns, 2, ngb, nt),
        in_specs=[pl.BlockSpec((tc * SUBLANE, MXU_DIM), lambda n, dd, gb, t: (t_blk(n, dd, t), gb)),
                  w_spec,
                  pl.BlockSpec((1, 1, n_tiles * MXU_DIM, MXU_DIM), lambda n, dd, gb, t: (dd, gb, 0, 0)),
                  pl.BlockSpec((1, 1, n_tiles, 1, MXU_DIM), lambda n, dd, gb, t: (dd, gb, 0, 0, 0)),
                  st_spec],
        out_specs=[pl.BlockSpec((1, tc * SUBLANE, MXU_DIM), lambda n, dd, gb, t: (dd, t_blk(n, dd, t), gb)),
                   st_spec],
        out_shape=[jax.ShapeDtypeStruct((2, rows, d), F32),
                   jax.ShapeDtypeStruct((ns, 2, ngb, n_tiles, SUBLANE, MXU_DIM), F32)],
        scratch_shapes=[pltpu.VMEM((n_tiles, SUBLANE, MXU_DIM), F32),
                        pltpu.VMEM((3, n_tiles, sub_t * SUBLANE, MXU_DIM), F32)],
        compiler_params=_params("parallel", "parallel", "parallel", "arbitrary"),
        name="s5_scan",
    )(u, wb, wc, lam, h0)


def _s5_block_weights(lb_re, lb_im, bb_re, bb_im, c_re, c_im, n_groups, state_dim, group):
    gpb = MXU_DIM // group
    ngb = n_groups // gpb
    eye = jnp.eye(gpb, dtype=F32)

    def b_blocks(bb):
        bb = bb.reshape(2, ngb, gpb, state_dim, group)
        return jnp.einsum('dbgpc,gh->dbgchp', bb, eye).reshape(2, ngb, gpb * group, gpb * state_dim)

    def c_blocks(cc):
        cc = cc.reshape(2, ngb, gpb, group, state_dim)
        return jnp.einsum('dbgcp,gh->dbgphc', cc, eye).reshape(2, ngb, gpb * state_dim, gpb * group)

    wb = jnp.concatenate([b_blocks(bb_re), b_blocks(bb_im)], axis=-1).astype(BF16)
    wc = jnp.concatenate([c_blocks(c_re), -c_blocks(c_im)], axis=-2).astype(BF16)
    lam = jnp.concatenate([lb_re.reshape(2, ngb, 1, gpb * state_dim),
                           lb_im.reshape(2, ngb, 1, gpb * state_dim)], axis=-1)
    n_tiles = wb.shape[-1] // MXU_DIM
    wb = wb.reshape(2, ngb, MXU_DIM, n_tiles, MXU_DIM).transpose(0, 1, 3, 2, 4)
    lam = lam.reshape(2, ngb, n_tiles, 1, MXU_DIM)
    return wb, wc, lam


def _s5_state_blocks(s_re, s_im, ngb):
    b, two, g, p = s_re.shape
    sw = g * p // ngb

    def blk(s):
        return s.reshape(b // SUBLANE, SUBLANE, two, ngb, sw).transpose(0, 2, 3, 1, 4)

    st = jnp.concatenate([blk(s_re), blk(s_im)], axis=-1)
    st = st.reshape(st.shape[:4] + (2 * sw // MXU_DIM, MXU_DIM))
    return st.transpose(0, 1, 2, 4, 3, 5)


def _s5_state_unblock(fin, g, p):
    ns, two, ngb, n_tiles, sub, lanes = fin.shape
    fin = fin.transpose(0, 1, 2, 4, 3, 5).reshape(ns, two, ngb, sub, n_tiles * lanes)
    sw = n_tiles * lanes // 2

    def unblk(s):
        return s.transpose(0, 3, 1, 2, 4).reshape(ns * sub, two, g, p)

    return unblk(fin[..., :sw]), unblk(fin[..., sw:])


def _glu_kernel(x_ref, yf_ref, yb_ref, g_ref, sc_ref, sh_ref, dskip_ref, w_ref, b_ref, gate_ref, o_ref):
    x = x_ref[...]
    xm = _norm_mod_tm(x, g_ref[...], sc_ref[0], sh_ref[0])
    gz = jax.nn.gelu(yf_ref[0] + yb_ref[0] + dskip_ref[...] * xm)
    acc = jnp.dot(gz.astype(BF16), w_ref[...], preferred_element_type=F32) + b_ref[...]
    o_ref[...] = x + _per_seq(gz * jax.nn.sigmoid(acc), gate_ref[0], jnp.multiply)


def _s5_glu(x, y, g, sc, sh, d_skip, w_glu, b_glu, gate, seq_len):
    m, d = x.shape
    group_rows = seq_len * SUBLANE
    tm = _tile(group_rows, 256, BF16_ROWS)
    mod_spec = pl.BlockSpec((1, SUBLANE, d), lambda i: (i * tm // group_rows, 0, 0))
    vec_spec = pl.BlockSpec((1, d), lambda i: (0, 0))
    return pl.pallas_call(
        _glu_kernel,
        grid=(m // tm,),
        in_specs=[pl.BlockSpec((tm, d), lambda i: (i, 0)),
                  pl.BlockSpec((1, tm, d), lambda i: (0, i, 0)),
                  pl.BlockSpec((1, tm, d), lambda i: (1, i, 0)),
                  vec_spec, mod_spec, mod_spec, vec_spec,
                  pl.BlockSpec((d, d), lambda i: (0, 0)),
                  vec_spec, mod_spec],
        out_specs=pl.BlockSpec((tm, d), lambda i: (i, 0)),
        out_shape=jax.ShapeDtypeStruct((m, d), F32),
        compiler_params=_params("parallel"),
        name="s5_glu",
    )(x, y, y, g, sc, sh, d_skip, w_glu, b_glu, gate)


def _final_norm_kernel(x_ref, g_ref, o_ref):
    nb, tl, d = o_ref.shape
    x = x_ref[...]
    y = x * _rstd(x) * g_ref[...]
    o_ref[...] = pltpu.einshape("tbd->btd", y.reshape(tl, nb, d))


def _final_norm(x, g, batch, seq_len):
    m, d = x.shape
    tl = _tile(seq_len, 64, SUBLANE)
    nl = seq_len // tl
    return pl.pallas_call(
        _final_norm_kernel,
        grid=(batch // SUBLANE, nl),
        in_specs=[pl.BlockSpec((tl * SUBLANE, d), lambda s, t: (s * nl + t, 0)),
                  pl.BlockSpec((1, d), lambda s, t: (0, 0))],
        out_specs=pl.BlockSpec((SUBLANE, tl, d), lambda s, t: (s, t, 0)),
        out_shape=jax.ShapeDtypeStruct((batch, seq_len, d), F32),
        compiler_params=_params("parallel", "parallel"),
        name="final_norm",
    )(x, g)


def kernel(x_prompt, x_sample, cache_k, cache_v, state_re, state_im, c, c_ctx, w_mod, b_mod, norm_g, w_qkv, lam_vecs, subln_g, w_o, ssm_a_re, ssm_a_im, ssm_log_step, ssm_b_re, ssm_b_im, ssm_c_re, ssm_c_im, ssm_d, w_glu, b_glu, w_up, conv_w, conv_b, w_down, final_g):
    bp, lp, d = x_prompt.shape
    bs, ls, _ = x_sample.shape
    depth = w_mod.shape[0]
    n_heads, head_dim = cache_k.shape[3], cache_k.shape[5]
    n_groups, state_dim, group = ssm_b_re.shape[2], ssm_b_re.shape[3], ssm_b_re.shape[4]
    assert bp % SUBLANE == 0 and bs % SUBLANE == 0 and d % MXU_DIM == 0 and MXU_DIM % group == 0
    assert 2 * head_dim == MXU_DIM and ls % GRID_W == 0

    pad_rows = -(bs + 1) % SUBLANE
    cond = jnp.concatenate([c, c_ctx[None], jnp.zeros((pad_rows, d), F32)], axis=0)
    mod = _modulation(cond, w_mod, b_mod).reshape(depth, cond.shape[0], 6, d)

    streams = [dict(x=x_prompt.reshape(bp * lp, d), batch=bp, seq=lp, time_major=False),
               dict(x=x_sample.reshape(bs * ls, d), batch=bs, seq=ls, time_major=False)]
    new_k, new_v, new_sre, new_sim = [], [], [], []
    rope = _rope_tables(ls, head_dim)
    ngb = d // MXU_DIM

    for i in range(depth):
        mix, slot = i % N_MIXERS, i // N_MIXERS
        mods = [[jnp.broadcast_to(mod[i, bs, j], (bp, d)) for j in range(6)],
                [mod[i, :bs, j] for j in range(6)]]
        g1 = norm_g[i, 0][None]
        g2 = norm_g[i, 1][None]

        def per_group(v):
            return v.reshape(v.shape[0] // SUBLANE, SUBLANE, d)

        if mix == 0:
            lam_init = 0.8 - 0.6 * math.exp(-0.3 * i)
            wq = w_qkv[slot].astype(BF16)
            wo = w_o[slot].astype(BF16)
            sub_g = subln_g[slot][None]
            ck = cache_k.reshape(cache_k.shape[:3] + (-1,))
            cv = cache_v.reshape(cache_v.shape[:3] + (-1,))
            for s, (st, md) in enumerate(zip(streams, mods)):
                b, l = st['batch'], st['seq']
                if st['time_major']:
                    st['x'] = _to_batch_major(st['x'], b, l)
                sh1, sc1 = (v[:1, None] if s == 0 else v[:, None] for v in (md[0], md[1]))
                if s == 0:
                    qkv = _qkv_proj(st['x'], g1, sc1, sh1, wq, l, F32)
                    nqk = n_heads * 2 * head_dim
                    new_k.append(qkv[:, nqk:2 * nqk].reshape(b, l, n_heads, 2, head_dim))
                    new_v.append(qkv[:, 2 * nqk:].reshape(b, l, n_heads, 2 * head_dim))
                    o = _diff_attention(qkv, lam_vecs[slot], sub_g, lam_init, b, l, n_heads, head_dim)
                else:
                    qkv = _qkv_proj(st['x'], g1, sc1, sh1, wq, l, BF16, rope_tables=rope,
                                    n_rope_cols=2 * n_heads * 2 * head_dim)
                    o = _diff_attention(qkv, lam_vecs[slot], sub_g, lam_init, b, l, n_heads, head_dim,
                                        cache_k=ck, cache_v=cv, slot=slot)
                st['x'] = _attn_out_proj(o, wo, st['x'], md[2][:, None], b, l)
                st['time_major'] = True
        else:
            lb_re, lb_im, bb_re, bb_im = _s5_discretize(ssm_a_re[slot], ssm_a_im[slot], ssm_log_step[slot],
                                                        ssm_b_re[slot], ssm_b_im[slot])
            wb, wc, lam = _s5_block_weights(lb_re, lb_im, bb_re, bb_im, ssm_c_re[slot], ssm_c_im[slot],
                                            n_groups, state_dim, group)
            wg = w_glu[slot].astype(BF16)
            for s, (st, md) in enumerate(zip(streams, mods)):
                b, l = st['batch'], st['seq']
                sh1, sc1, gt1 = per_group(md[0]), per_group(md[1]), per_group(md[2])
                u = _s5_input(st['x'], g1, sc1, sh1, l)
                if s == 0:
                    h0 = jnp.zeros((b // SUBLANE, 2, ngb, wb.shape[2], SUBLANE, MXU_DIM), F32)
                else:
                    h0 = _s5_state_blocks(state_re[:, slot], state_im[:, slot], ngb)
                y, fin = _s5_scan(u, wb, wc, lam, h0, l)
                if s == 0:
                    f_re, f_im = _s5_state_unblock(fin, n_groups, state_dim)
                    new_sre.append(f_re)
                    new_sim.append(f_im)
                st['x'] = _s5_glu(st['x'], y, g1, sc1, sh1, ssm_d[slot][None], wg, b_glu[slot][None], gt1, l)
        wu = w_up[i].astype(BF16)
        wd = w_down[i].astype(BF16)
        for st, md in zip(streams, mods):
            sh2, sc2, gt2 = per_group(md[3]), per_group(md[4]), per_group(md[5])
            act = _conv_ffn_up(st['x'], g2, sc2, sh2, wu, conv_w[i], conv_b[i][None], st['seq'])
            st['x'] = _matmul_gate_residual(act, wd, st['x'], gt2, st['seq'])

    y_prompt = _final_norm(streams[0]['x'], final_g[None], bp, lp)
    y_sample = _final_norm(streams[1]['x'], final_g[None], bs, ls)
    return (y_prompt, y_sample, jnp.stack(new_k, axis=1), jnp.stack(new_v, axis=1),
            jnp.stack(new_sre, axis=1), jnp.stack(new_sim, axis=1))
```

```python
import functools
import math

import jax
import jax.numpy as jnp
from jax import lax
from jax.experimental import pallas as pl
from jax.experimental.pallas import tpu as pltpu

F32 = jnp.float32
BF16 = jnp.bfloat16

GRID_W = 64
ROPE_THETA = 10000.0
NORM_EPS = 1e-6
SUBLN_EPS = 1e-5
N_MIXERS = 2

LANE = 128
SUBLANE = 8
BF16_ROWS = 16
MXU_DIM = 256
VMEM_LIMIT_BYTES = 52 * 1024 * 1024


def _tile(n, pref, align):
    t = min(pref, n)
    t -= t % align
    while t >= align:
        if n % t == 0:
            return t
        t -= align
    return n


def _row_tile(m, seq_len, n_mod, pref, align):
    return _tile(m, pref, align) if n_mod == 1 else _tile(seq_len, pref, align)


def _params(*sem):
    return pltpu.CompilerParams(dimension_semantics=sem, vmem_limit_bytes=VMEM_LIMIT_BYTES)


def _rstd(x):
    return lax.rsqrt(jnp.mean(x * x, axis=-1, keepdims=True) + NORM_EPS)


def _norm_mod(x, g, sc, sh):
    return x * _rstd(x) * (g * (1.0 + sc)) + sh


def _per_seq(x, pattern, op):
    rows, n = x.shape
    return op(x.reshape(rows // SUBLANE, SUBLANE, n), pattern[None]).reshape(rows, n)


def _norm_mod_tm(x, g, sc, sh):
    scaled = _per_seq(x * _rstd(x), g * (1.0 + sc), jnp.multiply)
    return _per_seq(scaled, sh, jnp.add)


def _mod_kernel(c_ref, w_ref, b_ref, o_ref):
    c = c_ref[...]
    s = (c * jax.nn.sigmoid(c)).astype(BF16)
    o_ref[0] = jnp.dot(s, w_ref[0].astype(BF16), preferred_element_type=F32) + b_ref[0]


def _modulation(cond, w_mod, b_mod):
    depth, d, n = w_mod.shape
    rows = cond.shape[0]
    tn = _tile(n, 1024, LANE)
    return pl.pallas_call(
        _mod_kernel,
        grid=(depth, n // tn),
        in_specs=[pl.BlockSpec((rows, d), lambda l, j: (0, 0)),
                  pl.BlockSpec((1, d, tn), lambda l, j: (l, 0, j)),
                  pl.BlockSpec((1, 1, tn), lambda l, j: (l, 0, j))],
        out_specs=pl.BlockSpec((1, rows, tn), lambda l, j: (l, 0, j)),
        out_shape=jax.ShapeDtypeStruct((depth, rows, n), F32),
        compiler_params=_params("parallel", "parallel"),
        name="modulation",
    )(cond, w_mod, b_mod.reshape(depth, 1, n))


def _qkv_kernel(*refs, rope, n_rope_tiles, tn):
    if rope:
        x_ref, g_ref, sc_ref, sh_ref, w_ref, cos_ref, sin_ref, o_ref, xm_ref = refs
    else:
        x_ref, g_ref, sc_ref, sh_ref, w_ref, o_ref, xm_ref = refs
    j = pl.program_id(1)

    @pl.when(j == 0)
    def _():
        xm_ref[...] = _norm_mod(x_ref[...], g_ref[...], sc_ref[0], sh_ref[0]).astype(BF16)

    acc = jnp.dot(xm_ref[...], w_ref[...], preferred_element_type=F32)
    if not rope:
        o_ref[...] = acc.astype(o_ref.dtype)
        return

    @pl.when(j < n_rope_tiles)
    def _():
        cos = cos_ref[0]
        sin = sin_ref[0]
        for c in range(tn // LANE):
            x = acc[:, c * LANE:(c + 1) * LANE]
            o_ref[:, c * LANE:(c + 1) * LANE] = (x * cos + pltpu.roll(x, LANE // 2, 1) * sin).astype(o_ref.dtype)

    @pl.when(j >= n_rope_tiles)
    def _():
        o_ref[...] = acc.astype(o_ref.dtype)


def _qkv_proj(x, g, sc, sh, w, seq_len, out_dtype, rope_tables=None, n_rope_cols=0):
    m, d = x.shape
    n = w.shape[1]
    n_mod = sc.shape[0]
    tm = _tile(seq_len, 1024, BF16_ROWS) if rope_tables is not None else _row_tile(m, seq_len, n_mod, 1024, BF16_ROWS)
    tn = _tile(n_rope_cols if rope_tables is not None else n, 512, LANE)

    def mod_i(i):
        return (i * tm // seq_len) % n_mod

    in_specs = [pl.BlockSpec((tm, d), lambda i, j: (i, 0)),
                pl.BlockSpec((1, d), lambda i, j: (0, 0)),
                pl.BlockSpec((1, 1, d), lambda i, j: (mod_i(i), 0, 0)),
                pl.BlockSpec((1, 1, d), lambda i, j: (mod_i(i), 0, 0)),
                pl.BlockSpec((d, tn), lambda i, j: (0, j))]
    args = [x, g, sc, sh, w]
    if rope_tables is not None:
        tiles_per_seq = seq_len // tm
        n_q_tiles = n_rope_cols // (2 * tn)
        in_specs += [pl.BlockSpec((1, tm, LANE),
                                  lambda i, j: (jnp.where(j < n_q_tiles, 0, 1), i % tiles_per_seq, 0))] * 2
        args += list(rope_tables)
    return pl.pallas_call(
        functools.partial(_qkv_kernel, rope=rope_tables is not None,
                          n_rope_tiles=n_rope_cols // tn, tn=tn),
        grid=(m // tm, n // tn),
        in_specs=in_specs,
        out_specs=pl.BlockSpec((tm, tn), lambda i, j: (i, j)),
        out_shape=jax.ShapeDtypeStruct((m, n), out_dtype),
        scratch_shapes=[pltpu.VMEM((tm, d), BF16)],
        compiler_params=_params("parallel", "arbitrary"),
        name="qkv_proj",
    )(*args)


def _swap_middle_quarters(a, head_dim):
    lead = a.shape[:-1]
    a = a.reshape(lead + (a.shape[-1] // head_dim, 2, 2, head_dim // 4))
    return jnp.swapaxes(a, -3, -2).reshape(lead + (-1,))


def _rope_tables(length, head_dim, q_scale):
    pos = jnp.arange(length)
    r = (pos // GRID_W).astype(F32)
    col = (pos % GRID_W).astype(F32)
    half = head_dim // 2
    inv = ROPE_THETA ** (-jnp.arange(0, half, 2, dtype=F32) / half)
    ar = r[:, None] * inv
    ac = col[:, None] * inv
    cos = jnp.concatenate([jnp.cos(ar), jnp.cos(ac)] * 2, axis=-1)
    sin = jnp.concatenate([-jnp.sin(ar), -jnp.sin(ac), jnp.sin(ar), jnp.sin(ac)], axis=-1)
    scales = jnp.array([q_scale, 1.0], F32)[:, None, None]
    return cos[None] * scales, sin[None] * scales


def _attn_kernel(*refs, has_cache, lam_init, q_scale, head_dim, kv_chunk):
    if has_cache:
        lam_ref, q_ref, k_ref, v_ref, ck_ref, cv_ref, g_ref, o_ref = refs
    else:
        lam_ref, q_ref, k_ref, v_ref, g_ref, o_ref = refs
    lv = lam_ref[...]
    lam = (jnp.exp(jnp.sum(lv[0:1] * lv[1:2], axis=-1, keepdims=True))
           - jnp.exp(jnp.sum(lv[2:3] * lv[3:4], axis=-1, keepdims=True)) + lam_init)
    q = q_ref[...] if q_scale is None else (q_ref[...] * q_scale).astype(BF16)
    ks = [k_ref[...].astype(BF16)]
    vs = [v_ref[...].astype(BF16)]
    if has_cache:
        ks.append(ck_ref[0, 0].astype(BF16))
        vs.append(cv_ref[0, 0].astype(BF16))
    kv = [(k[r:r + kv_chunk], v[r:r + kv_chunk]) for k, v in zip(ks, vs) for r in range(0, k.shape[0], kv_chunk)]
    outs = []
    for c in range(2):
        qc = q[:, c * head_dim:(c + 1) * head_dim]
        m = denom = acc = None
        for k, v in kv:
            s = lax.dot_general(qc, k[:, c * head_dim:(c + 1) * head_dim],
                                (((1,), (1,)), ((), ())), preferred_element_type=F32)
            s_max = jnp.max(s, axis=-1, keepdims=True)
            m_new = s_max if m is None else jnp.maximum(m, s_max)
            p = jnp.exp2(s - m_new)
            p_sum = jnp.sum(p, axis=-1, keepdims=True)
            pv = jnp.dot(p.astype(BF16), v, preferred_element_type=F32)
            if m is None:
                denom, acc = p_sum, pv
            else:
                alpha = jnp.exp2(m - m_new)
                denom, acc = denom * alpha + p_sum, acc * alpha + pv
            m = m_new
        outs.append(acc / denom)
    o = outs[0] - lam * outs[1]
    r = lax.rsqrt(jnp.mean(o * o, axis=-1, keepdims=True) + SUBLN_EPS)
    o_ref[...] = (o * r * g_ref[...] * (1.0 - lam_init)).astype(o_ref.dtype)


def _diff_attention(qkv, lam_vecs, subln_g, lam_init, q_scale, batch, seq_len, n_heads, head_dim,
                    cache_k=None, cache_v=None, slot=0):
    hw = 2 * head_dim
    tq = _tile(seq_len, 1024, BF16_ROWS)
    nq = seq_len // tq
    has_cache = cache_k is not None
    in_specs = [pl.BlockSpec(lam_vecs.shape, lambda b, h, i: (0, 0)),
                pl.BlockSpec((tq, hw), lambda b, h, i: (b * nq + i, h)),
                pl.BlockSpec((seq_len, hw), lambda b, h, i: (b, n_heads + h)),
                pl.BlockSpec((seq_len, hw), lambda b, h, i: (b, 2 * n_heads + h))]
    args = [lam_vecs, qkv, qkv, qkv]
    if has_cache:
        past = cache_k.shape[2]
        in_specs += [pl.BlockSpec((1, 1, past, hw), lambda b, h, i: (b, slot, 0, h))] * 2
        args += [cache_k, cache_v]
    in_specs.append(pl.BlockSpec((1, hw), lambda b, h, i: (0, 0)))
    args.append(subln_g)
    return pl.pallas_call(
        functools.partial(_attn_kernel, has_cache=has_cache, lam_init=lam_init,
                          q_scale=q_scale, head_dim=head_dim,
                          kv_chunk=_tile(seq_len, 512, BF16_ROWS)),
        grid=(batch, n_heads, nq),
        in_specs=in_specs,
        out_specs=pl.BlockSpec((tq, hw), lambda b, h, i: (b * nq + i, h)),
        out_shape=jax.ShapeDtypeStruct((batch * seq_len, n_heads * hw), BF16),
        compiler_params=_params("parallel", "parallel", "arbitrary"),
        name="diff_attention",
    )(*args)


def _attn_out_kernel(a_ref, w_ref, x_ref, gate_ref, o_ref):
    nb, tl, k = a_ref.shape
    acc = jnp.dot(a_ref[...].reshape(nb * tl, k), w_ref[...], preferred_element_type=F32)
    res = x_ref[...] + gate_ref[...] * acc.reshape(nb, tl, -1)
    o_ref[...] = pltpu.einshape("btn->tbn", res).reshape(tl * nb, -1)


def _attn_out_proj(o, w, x, gate, batch, seq_len):
    k = o.shape[1]
    n = w.shape[1]
    tl = _tile(seq_len, 128, BF16_ROWS)
    tn = _tile(n, 512, LANE)
    nl = seq_len // tl
    return pl.pallas_call(
        _attn_out_kernel,
        grid=(batch // SUBLANE, nl, n // tn),
        in_specs=[pl.BlockSpec((SUBLANE, tl, k), lambda s, t, j: (s, t, 0)),
                  pl.BlockSpec((k, tn), lambda s, t, j: (0, j)),
                  pl.BlockSpec((SUBLANE, tl, tn), lambda s, t, j: (s, t, j)),
                  pl.BlockSpec((SUBLANE, 1, tn), lambda s, t, j: (s, 0, j))],
        out_specs=pl.BlockSpec((tl * SUBLANE, tn), lambda s, t, j: (s * nl + t, j)),
        out_shape=jax.ShapeDtypeStruct((batch * seq_len, n), F32),
        compiler_params=_params("parallel", "parallel", "arbitrary"),
        name="attn_out_proj",
    )(o.reshape(batch, seq_len, k), w, x.reshape(batch, seq_len, n), gate)


def _to_batch_major_kernel(x_ref, o_ref):
    nb, tl, d = o_ref.shape
    o_ref[...] = pltpu.einshape("tbd->btd", x_ref[...].reshape(tl, nb, d))


def _to_batch_major(x, batch, seq_len):
    m, d = x.shape
    tl = _tile(seq_len, 64, SUBLANE)
    nl = seq_len // tl
    return pl.pallas_call(
        _to_batch_major_kernel,
        grid=(batch // SUBLANE, nl),
        in_specs=[pl.BlockSpec((tl * SUBLANE, d), lambda s, t: (s * nl + t, 0))],
        out_specs=pl.BlockSpec((SUBLANE, tl, d), lambda s, t: (s, t, 0)),
        out_shape=jax.ShapeDtypeStruct((batch, seq_len, d), F32),
        compiler_params=_params("parallel", "parallel"),
        name="to_batch_major",
    )(x).reshape(m, d)


def _mm_res_kernel(a_ref, w_ref, x_ref, gate_ref, o_ref):
    acc = jnp.dot(a_ref[...], w_ref[...], preferred_element_type=F32)
    o_ref[...] = x_ref[...] + _per_seq(acc, gate_ref[0], jnp.multiply)


def _matmul_gate_residual(a, w, x, gate, seq_len):
    m, k = a.shape
    n = w.shape[1]
    group_rows = seq_len * SUBLANE
    tm = _tile(group_rows, 1024, BF16_ROWS)
    tn = _tile(n, 512, LANE)
    return pl.pallas_call(
        _mm_res_kernel,
        grid=(m // tm, n // tn),
        in_specs=[pl.BlockSpec((tm, k), lambda i, j: (i, 0)),
                  pl.BlockSpec((k, tn), lambda i, j: (0, j)),
                  pl.BlockSpec((tm, tn), lambda i, j: (i, j)),
                  pl.BlockSpec((1, SUBLANE, tn), lambda i, j: (i * tm // group_rows, 0, j))],
        out_specs=pl.BlockSpec((tm, tn), lambda i, j: (i, j)),
        out_shape=jax.ShapeDtypeStruct((m, n), F32),
        compiler_params=_params("parallel", "arbitrary"),
        name="matmul_gate_residual",
    )(a, w, x, gate)


def _up_kernel(x_ref, xp_ref, xn_ref, g_ref, sc_ref, sh_ref, wg_ref, wv_ref,
               cwg_ref, cwv_ref, cbg_ref, cbv_ref, o_ref, xm_ref, *, tm, group_rows, chunk):
    halo = BF16_ROWS
    i = pl.program_id(0)
    j = pl.program_id(1)

    @pl.when(j == 0)
    def _():
        g, sc, sh = g_ref[...], sc_ref[0], sh_ref[0]
        xm_ref[0:halo] = _norm_mod_tm(xp_ref[...], g, sc, sh).astype(BF16)
        xm_ref[halo:halo + tm] = _norm_mod_tm(x_ref[...], g, sc, sh).astype(BF16)
        xm_ref[halo + tm:] = _norm_mod_tm(xn_ref[...], g, sc, sh).astype(BF16)

        @pl.when((i * tm) % group_rows == 0)
        def _():
            xm_ref[0:halo] = jnp.zeros((halo, xm_ref.shape[1]), BF16)

        @pl.when(((i + 1) * tm) % group_rows == 0)
        def _():
            xm_ref[halo + tm:] = jnp.zeros((halo, xm_ref.shape[1]), BF16)

    xm = xm_ref[...]

    def conv(h, cw_ref, cb_ref, cols):
        cw = cw_ref[:, cols]
        lo = halo - SUBLANE
        return (h[lo:lo + tm] * cw[0:1] + h[halo:halo + tm] * cw[1:2]
                + h[halo + SUBLANE:halo + SUBLANE + tm] * cw[2:3] + cb_ref[:, cols])

    for c in range(o_ref.shape[1] // chunk):
        cols = slice(c * chunk, (c + 1) * chunk)
        gate = conv(jnp.dot(xm, wg_ref[:, cols], preferred_element_type=F32), cwg_ref, cbg_ref, cols)
        val = conv(jnp.dot(xm, wv_ref[:, cols], preferred_element_type=F32), cwv_ref, cbv_ref, cols)
        o_ref[:, cols] = (gate * jax.nn.sigmoid(gate) * val).astype(o_ref.dtype)


def _conv_ffn_up(x, g, sc, sh, w_up, conv_w, conv_b, seq_len):
    m, d = x.shape
    d_ff = w_up.shape[1] // 2
    halo = BF16_ROWS
    group_rows = seq_len * SUBLANE
    tm = _tile(group_rows, 1024, halo)
    tn = _tile(d_ff, 512, LANE)
    chunk = _tile(tn, MXU_DIM, LANE)
    nj = d_ff // tn
    hb = tm // halo
    last_hb = m // halo - 1

    mod_spec = pl.BlockSpec((1, SUBLANE, d), lambda i, j: (i * tm // group_rows, 0, 0))
    return pl.pallas_call(
        functools.partial(_up_kernel, tm=tm, group_rows=group_rows, chunk=chunk),
        grid=(m // tm, nj),
        in_specs=[pl.BlockSpec((tm, d), lambda i, j: (i, 0)),
                  pl.BlockSpec((halo, d), lambda i, j: (jnp.maximum(i * hb - 1, 0), 0)),
                  pl.BlockSpec((halo, d), lambda i, j: (jnp.minimum((i + 1) * hb, last_hb), 0)),
                  pl.BlockSpec((1, d), lambda i, j: (0, 0)),
                  mod_spec, mod_spec,
                  pl.BlockSpec((d, tn), lambda i, j: (0, j)),
                  pl.BlockSpec((d, tn), lambda i, j: (0, nj + j)),
                  pl.BlockSpec((3, tn), lambda i, j: (0, j)),
                  pl.BlockSpec((3, tn), lambda i, j: (0, nj + j)),
                  pl.BlockSpec((1, tn), lambda i, j: (0, j)),
                  pl.BlockSpec((1, tn), lambda i, j: (0, nj + j))],
        out_specs=pl.BlockSpec((tm, tn), lambda i, j: (i, j)),
        out_shape=jax.ShapeDtypeStruct((m, d_ff), BF16),
        scratch_shapes=[pltpu.VMEM((tm + 2 * halo, d), BF16)],
        compiler_params=_params("parallel", "arbitrary"),
        name="conv_ffn_up",
    )(x, x, x, g, sc, sh, w_up, w_up, conv_w, conv_w, conv_b, conv_b)


def _discretize(a_re, a_im, log_step):
    dt = jnp.exp(log_step)
    mag = jnp.exp(a_re * dt)
    lb_re = mag * jnp.cos(a_im * dt)
    lb_im = mag * jnp.sin(a_im * dt)
    num_re = lb_re - 1.0
    den = a_re * a_re + a_im * a_im
    f_re = (num_re * a_re + lb_im * a_im) / den
    f_im = (lb_im * a_re - num_re * a_im) / den
    return lb_re, lb_im, f_re, f_im


def _s5_disc_kernel(a_re_ref, a_im_ref, ls_ref, ax_re_ref, ax_im_ref, lsx_ref, b_re_ref, b_im_ref,
                    lb_re_ref, lb_im_ref, bb_re_ref, bb_im_ref):
    lb_re, lb_im, _, _ = _discretize(a_re_ref[...], a_im_ref[...], ls_ref[...])
    lb_re_ref[...] = lb_re
    lb_im_ref[...] = lb_im
    _, _, f_re, f_im = _discretize(ax_re_ref[...], ax_im_ref[...], lsx_ref[...])
    b_re, b_im = b_re_ref[...], b_im_ref[...]
    bb_re_ref[...] = f_re * b_re - f_im * b_im
    bb_im_ref[...] = f_re * b_im + f_im * b_re


def _s5_discretize(a_re, a_im, log_step, b_re, b_im):
    two, g, p, c = b_re.shape
    rows = two * g
    a2 = lambda a: a.reshape(rows, p)
    ax = lambda a: jnp.repeat(a.reshape(rows, p), c, axis=-1)
    ls = jnp.broadcast_to(log_step.reshape(rows, 1), (rows, p))
    lsx = jnp.broadcast_to(log_step.reshape(rows, 1), (rows, p * c))
    return pl.pallas_call(
        _s5_disc_kernel,
        out_shape=[jax.ShapeDtypeStruct((rows, p), F32)] * 2 + [jax.ShapeDtypeStruct((rows, p * c), F32)] * 2,
        compiler_params=pltpu.CompilerParams(vmem_limit_bytes=VMEM_LIMIT_BYTES),
        name="s5_discretize",
    )(a2(a_re), a2(a_im), ls, ax(a_re), ax(a_im), lsx, b_re.reshape(rows, p * c), b_im.reshape(rows, p * c))


def _s5_in_kernel(x_ref, g_ref, sc_ref, sh_ref, o_ref):
    o_ref[...] = _norm_mod_tm(x_ref[...], g_ref[...], sc_ref[0], sh_ref[0]).astype(o_ref.dtype)


def _s5_input(x, g, sc, sh, seq_len):
    m, d = x.shape
    group_rows = seq_len * SUBLANE
    tm = _tile(group_rows, 512, BF16_ROWS)
    mod_spec = pl.BlockSpec((1, SUBLANE, d), lambda i: (i * tm // group_rows, 0, 0))
    return pl.pallas_call(
        _s5_in_kernel,
        grid=(m // tm,),
        in_specs=[pl.BlockSpec((tm, d), lambda i: (i, 0)), pl.BlockSpec((1, d), lambda i: (0, 0)),
                  mod_spec, mod_spec],
        out_specs=pl.BlockSpec((tm, d), lambda i: (i, 0)),
        out_shape=jax.ShapeDtypeStruct((m, d), BF16),
        compiler_params=_params("parallel"),
        name="s5_input",
    )(x, g, sc, sh)


def _s5_scan_kernel(u_ref, wb_ref, wc_ref, lam_ref, h0_ref, y_ref, fin_ref, st_ref, buf_ref,
                    *, nsub, nt, sub_t):
    d = pl.program_id(1)
    t_idx = pl.program_id(3)
    n_tiles = st_ref.shape[0]
    half = n_tiles // 2
    sr = sub_t * SUBLANE
    steps = sub_t // n_tiles

    @pl.when(t_idx == 0)
    def _():
        st_ref[...] = h0_ref[0, 0, 0]

    lam = jnp.broadcast_to(lam_ref[0, 0], st_ref.shape)
    a_re, a_im = lam[:half], lam[half:]

    def first_row(p):
        return pl.multiple_of((p + d * (nsub - 1 - 2 * p)) * sr, sr)

    def run_pass(p, state, do_bu, do_scan, do_ch):
        slot_bu, slot_scan, slot_ch = p % 3, (p + 2) % 3, (p + 1) % 3
        if do_ch:
            h_all = jnp.concatenate([buf_ref[slot_ch, i] for i in range(n_tiles)], axis=1)
            y_ref[0, pl.ds(first_row(p - 2), sr), :] = jnp.dot(h_all.astype(BF16), wc_ref[0, 0],
                                                               preferred_element_type=F32)
        if do_bu:
            u_sub = u_ref[pl.ds(first_row(p), sr), :]
        h_re, h_im = state
        for i in range(n_tiles):
            if do_bu:
                buf_ref[slot_bu, i] = jnp.dot(u_sub, wb_ref[0, 0, i], preferred_element_type=F32)
            if do_scan:
                for k in range(steps):
                    t = i * steps + k
                    row = pl.multiple_of((t + d * (sub_t - 1 - 2 * t)) * SUBLANE, SUBLANE)
                    bu = buf_ref[slot_scan, :, pl.ds(row, SUBLANE), :]
                    n_re = a_re * h_re - a_im * h_im + bu[:half]
                    n_im = a_re * h_im + a_im * h_re + bu[half:]
                    buf_ref[slot_scan, 0:half, pl.ds(row, SUBLANE), :] = n_re
                    buf_ref[slot_scan, half:, pl.ds(row, SUBLANE), :] = n_im
                    h_re, h_im = n_re, n_im
        return h_re, h_im

    state = (st_ref[0:half], st_ref[half:])
    steady = range(2, nsub)
    for p in range(min(2, nsub + 2)):
        state = run_pass(p, state, p < nsub, 1 <= p <= nsub, False)
    if len(steady) > 0:
        state = lax.fori_loop(steady.start, steady.stop,
                              lambda p, st: run_pass(p, st, True, True, True), state)
    for p in range(max(2, nsub), nsub + 2):
        state = run_pass(p, state, p < nsub, p <= nsub, True)
    st_ref[0:half] = state[0]
    st_ref[half:] = state[1]

    @pl.when(t_idx == nt - 1)
    def _():
        fin_ref[0, 0, 0] = st_ref[...]


def _s5_scan(u, wb, wc, lam, h0, seq_len):
    rows, d = u.shape
    ns = rows // (seq_len * SUBLANE)
    ngb = d // MXU_DIM
    n_tiles = wb.shape[2]
    sub_t = _tile(seq_len, 64, n_tiles)
    tc = _tile(seq_len, 16 * sub_t, sub_t)
    nt = seq_len // tc
    assert sub_t % n_tiles == 0 and tc % sub_t == 0

    def t_blk(n, dd, t):
        return n * nt + t + dd * (nt - 1 - 2 * t)

    w_spec = pl.BlockSpec((1, 1, n_tiles, MXU_DIM, MXU_DIM), lambda n, dd, gb, t: (dd, gb, 0, 0, 0))
    st_spec = pl.BlockSpec((1, 1, 1, n_tiles, SUBLANE, MXU_DIM), lambda n, dd, gb, t: (n, dd, gb, 0, 0, 0))
    return pl.pallas_call(
        functools.partial(_s5_scan_kernel, nsub=tc // sub_t, nt=nt, sub_t=sub_t),
        grid=(ns, 2, ngb, nt),
        in_specs=[pl.BlockSpec((tc * SUBLANE, MXU_DIM), lambda n, dd, gb, t: (t_blk(n, dd, t), gb)),
                  w_spec,
                  pl.BlockSpec((1, 1, n_tiles * MXU_DIM, MXU_DIM), lambda n, dd, gb, t: (dd, gb, 0, 0)),
                  pl.BlockSpec((1, 1, n_tiles, 1, MXU_DIM), lambda n, dd, gb, t: (dd, gb, 0, 0, 0)),
                  st_spec],
        out_specs=[pl.BlockSpec((1, tc * SUBLANE, MXU_DIM), lambda n, dd, gb, t: (dd, t_blk(n, dd, t), gb)),
                   st_spec],
        out_shape=[jax.ShapeDtypeStruct((2, rows, d), F32),
                   jax.ShapeDtypeStruct((ns, 2, ngb, n_tiles, SUBLANE, MXU_DIM), F32)],
        scratch_shapes=[pltpu.VMEM((n_tiles, SUBLANE, MXU_DIM), F32),
                        pltpu.VMEM((3, n_tiles, sub_t * SUBLANE, MXU_DIM), F32)],
        compiler_params=_params("parallel", "parallel", "parallel", "arbitrary"),
        name="s5_scan",
    )(u, wb, wc, lam, h0)


def _s5_block_weights(lb_re, lb_im, bb_re, bb_im, c_re, c_im, n_groups, state_dim, group):
    gpb = MXU_DIM // group
    ngb = n_groups // gpb
    eye = jnp.eye(gpb, dtype=F32)

    def b_blocks(bb):
        bb = bb.reshape(2, ngb, gpb, state_dim, group)
        return jnp.einsum('dbgpc,gh->dbgchp', bb, eye).reshape(2, ngb, gpb * group, gpb * state_dim)

    def c_blocks(cc):
        cc = cc.reshape(2, ngb, gpb, group, state_dim)
        return jnp.einsum('dbgcp,gh->dbgphc', cc, eye).reshape(2, ngb, gpb * state_dim, gpb * group)

    wb = jnp.concatenate([b_blocks(bb_re), b_blocks(bb_im)], axis=-1).astype(BF16)
    wc = jnp.concatenate([c_blocks(c_re), -c_blocks(c_im)], axis=-2).astype(BF16)
    lam = jnp.concatenate([lb_re.reshape(2, ngb, 1, gpb * state_dim),
                           lb_im.reshape(2, ngb, 1, gpb * state_dim)], axis=-1)
    n_tiles = wb.shape[-1] // MXU_DIM
    wb = wb.reshape(2, ngb, MXU_DIM, n_tiles, MXU_DIM).transpose(0, 1, 3, 2, 4)
    lam = lam.reshape(2, ngb, n_tiles, 1, MXU_DIM)
    return wb, wc, lam


def _s5_state_blocks(s_re, s_im, ngb):
    b, two, g, p = s_re.shape
    sw = g * p // ngb

    def blk(s):
        return s.reshape(b // SUBLANE, SUBLANE, two, ngb, sw).transpose(0, 2, 3, 1, 4)

    st = jnp.concatenate([blk(s_re), blk(s_im)], axis=-1)
    st = st.reshape(st.shape[:4] + (2 * sw // MXU_DIM, MXU_DIM))
    return st.transpose(0, 1, 2, 4, 3, 5)


def _s5_state_unblock(fin, g, p):
    ns, two, ngb, n_tiles, sub, lanes = fin.shape
    fin = fin.transpose(0, 1, 2, 4, 3, 5).reshape(ns, two, ngb, sub, n_tiles * lanes)
    sw = n_tiles * lanes // 2

    def unblk(s):
        return s.transpose(0, 3, 1, 2, 4).reshape(ns * sub, two, g, p)

    return unblk(fin[..., :sw]), unblk(fin[..., sw:])


def _glu_kernel(x_ref, yf_ref, yb_ref, g_ref, sc_ref, sh_ref, dskip_ref, w_ref, b_ref, gate_ref, o_ref):
    x = x_ref[...]
    xm = _norm_mod_tm(x, g_ref[...], sc_ref[0], sh_ref[0])
    gz = jax.nn.gelu(yf_ref[0] + yb_ref[0] + dskip_ref[...] * xm)
    acc = jnp.dot(gz.astype(BF16), w_ref[...], preferred_element_type=F32) + b_ref[...]
    o_ref[...] = x + _per_seq(gz * jax.nn.sigmoid(acc), gate_ref[0], jnp.multiply)


def _s5_glu(x, y, g, sc, sh, d_skip, w_glu, b_glu, gate, seq_len):
    m, d = x.shape
    group_rows = seq_len * SUBLANE
    tm = _tile(group_rows, 256, BF16_ROWS)
    mod_spec = pl.BlockSpec((1, SUBLANE, d), lambda i: (i * tm // group_rows, 0, 0))
    vec_spec = pl.BlockSpec((1, d), lambda i: (0, 0))
    return pl.pallas_call(
        _glu_kernel,
        grid=(m // tm,),
        in_specs=[pl.BlockSpec((tm, d), lambda i: (i, 0)),
                  pl.BlockSpec((1, tm, d), lambda i: (0, i, 0)),
                  pl.BlockSpec((1, tm, d), lambda i: (1, i, 0)),
                  vec_spec, mod_spec, mod_spec, vec_spec,
                  pl.BlockSpec((d, d), lambda i: (0, 0)),
                  vec_spec, mod_spec],
        out_specs=pl.BlockSpec((tm, d), lambda i: (i, 0)),
        out_shape=jax.ShapeDtypeStruct((m, d), F32),
        compiler_params=_params("parallel"),
        name="s5_glu",
    )(x, y, y, g, sc, sh, d_skip, w_glu, b_glu, gate)


def _final_norm_kernel(x_ref, g_ref, o_ref):
    nb, tl, d = o_ref.shape
    x = x_ref[...]
    y = x * _rstd(x) * g_ref[...]
    o_ref[...] = pltpu.einshape("tbd->btd", y.reshape(tl, nb, d))


def _final_norm(x, g, batch, seq_len):
    m, d = x.shape
    tl = _tile(seq_len, 64, SUBLANE)
    nl = seq_len // tl
    return pl.pallas_call(
        _final_norm_kernel,
        grid=(batch // SUBLANE, nl),
        in_specs=[pl.BlockSpec((tl * SUBLANE, d), lambda s, t: (s * nl + t, 0)),
                  pl.BlockSpec((1, d), lambda s, t: (0, 0))],
        out_specs=pl.BlockSpec((SUBLANE, tl, d), lambda s, t: (s, t, 0)),
        out_shape=jax.ShapeDtypeStruct((batch, seq_len, d), F32),
        compiler_params=_params("parallel", "parallel"),
        name="final_norm",
    )(x, g)


def kernel(x_prompt, x_sample, cache_k, cache_v, state_re, state_im, c, c_ctx, w_mod, b_mod, norm_g, w_qkv, lam_vecs, subln_g, w_o, ssm_a_re, ssm_a_im, ssm_log_step, ssm_b_re, ssm_b_im, ssm_c_re, ssm_c_im, ssm_d, w_glu, b_glu, w_up, conv_w, conv_b, w_down, final_g):
    bp, lp, d = x_prompt.shape
    bs, ls, _ = x_sample.shape
    depth = w_mod.shape[0]
    n_heads, head_dim = cache_k.shape[3], cache_k.shape[5]
    n_groups, state_dim, group = ssm_b_re.shape[2], ssm_b_re.shape[3], ssm_b_re.shape[4]
    assert bp % SUBLANE == 0 and bs % SUBLANE == 0 and d % MXU_DIM == 0 and MXU_DIM % group == 0
    assert 2 * head_dim == MXU_DIM and ls % GRID_W == 0

    pad_rows = -(bs + 1) % SUBLANE
    cond = jnp.concatenate([c, c_ctx[None], jnp.zeros((pad_rows, d), F32)], axis=0)
    mod = _modulation(cond, w_mod, b_mod).reshape(depth, cond.shape[0], 6, d)

    streams = [dict(x=x_prompt.reshape(bp * lp, d), batch=bp, seq=lp, time_major=False),
               dict(x=x_sample.reshape(bs * ls, d), batch=bs, seq=ls, time_major=False)]
    new_k, new_v, new_sre, new_sim = [], [], [], []
    q_scale = head_dim ** -0.5 * math.log2(math.e)
    rope = _rope_tables(ls, head_dim, q_scale)
    ngb = d // MXU_DIM

    for i in range(depth):
        mix, slot = i % N_MIXERS, i // N_MIXERS
        mods = [[jnp.broadcast_to(mod[i, bs, j], (bp, d)) for j in range(6)],
                [mod[i, :bs, j] for j in range(6)]]
        g1 = norm_g[i, 0][None]
        g2 = norm_g[i, 1][None]

        def per_group(v):
            return v.reshape(v.shape[0] // SUBLANE, SUBLANE, d)

        if mix == 0:
            lam_init = 0.8 - 0.6 * math.exp(-0.3 * i)
            wq = w_qkv[slot].astype(BF16)
            wo = w_o[slot].astype(BF16)
            sub_g = subln_g[slot][None]
            nqk = n_heads * 2 * head_dim
            wq_swapped = jnp.concatenate([_swap_middle_quarters(wq[:, :2 * nqk], head_dim), wq[:, 2 * nqk:]], axis=1)
            ck = _swap_middle_quarters(cache_k.reshape(cache_k.shape[:3] + (-1,)), head_dim)
            cv = cache_v.reshape(cache_v.shape[:3] + (-1,))
            for s, (st, md) in enumerate(zip(streams, mods)):
                b, l = st['batch'], st['seq']
                if st['time_major']:
                    st['x'] = _to_batch_major(st['x'], b, l)
                sh1, sc1 = (v[:1, None] if s == 0 else v[:, None] for v in (md[0], md[1]))
                if s == 0:
                    qkv = _qkv_proj(st['x'], g1, sc1, sh1, wq, l, F32)
                    new_k.append(qkv[:, nqk:2 * nqk].reshape(b, l, n_heads, 2, head_dim))
                    new_v.append(qkv[:, 2 * nqk:].reshape(b, l, n_heads, 2 * head_dim))
                    o = _diff_attention(qkv, lam_vecs[slot], sub_g, lam_init, q_scale, b, l, n_heads, head_dim)
                else:
                    qkv = _qkv_proj(st['x'], g1, sc1, sh1, wq_swapped, l, BF16, rope_tables=rope,
                                    n_rope_cols=2 * nqk)
                    o = _diff_attention(qkv, lam_vecs[slot], sub_g, lam_init, None, b, l, n_heads, head_dim,
                                        cache_k=ck, cache_v=cv, slot=slot)
                st['x'] = _attn_out_proj(o, wo, st['x'], md[2][:, None], b, l)
                st['time_major'] = True
        else:
            lb_re, lb_im, bb_re, bb_im = _s5_discretize(ssm_a_re[slot], ssm_a_im[slot], ssm_log_step[slot],
                                                        ssm_b_re[slot], ssm_b_im[slot])
            wb, wc, lam = _s5_block_weights(lb_re, lb_im, bb_re, bb_im, ssm_c_re[slot], ssm_c_im[slot],
                                            n_groups, state_dim, group)
            wg = w_glu[slot].astype(BF16)
            for s, (st, md) in enumerate(zip(streams, mods)):
                b, l = st['batch'], st['seq']
                sh1, sc1, gt1 = per_group(md[0]), per_group(md[1]), per_group(md[2])
                u = _s5_input(st['x'], g1, sc1, sh1, l)
                if s == 0:
                    h0 = jnp.zeros((b // SUBLANE, 2, ngb, wb.shape[2], SUBLANE, MXU_DIM), F32)
                else:
                    h0 = _s5_state_blocks(state_re[:, slot], state_im[:, slot], ngb)
                y, fin = _s5_scan(u, wb, wc, lam, h0, l)
                if s == 0:
                    f_re, f_im = _s5_state_unblock(fin, n_groups, state_dim)
                    new_sre.append(f_re)
                    new_sim.append(f_im)
                st['x'] = _s5_glu(st['x'], y, g1, sc1, sh1, ssm_d[slot][None], wg, b_glu[slot][None], gt1, l)
        wu = w_up[i].astype(BF16)
        wd = w_down[i].astype(BF16)
        for st, md in zip(streams, mods):
            sh2, sc2, gt2 = per_group(md[3]), per_group(md[4]), per_group(md[5])
            act = _conv_ffn_up(st['x'], g2, sc2, sh2, wu, conv_w[i], conv_b[i][None], st['seq'])
            st['x'] = _matmul_gate_residual(act, wd, st['x'], gt2, st['seq'])

    y_prompt = _final_norm(streams[0]['x'], final_g[None], bp, lp)
    y_sample = _final_norm(streams[1]['x'], final_g[None], bs, ls)
    return (y_prompt, y_sample, jnp.stack(new_k, axis=1), jnp.stack(new_v, axis=1),
            jnp.stack(new_sre, axis=1), jnp.stack(new_sim, axis=1))
```

```python
import functools
import math

import jax
import jax.numpy as jnp
from jax import lax
from jax.experimental import pallas as pl
from jax.experimental.pallas import tpu as pltpu

F32 = jnp.float32
BF16 = jnp.bfloat16

GRID_W = 64
ROPE_THETA = 10000.0
NORM_EPS = 1e-6
SUBLN_EPS = 1e-5
N_MIXERS = 2

LANE = 128
SUBLANE = 8
BF16_ROWS = 16
MXU_DIM = 256
VMEM_LIMIT_BYTES = 52 * 1024 * 1024


def _tile(n, pref, align):
    t = min(pref, n)
    t -= t % align
    while t >= align:
        if n % t == 0:
            return t
        t -= align
    return n


def _row_tile(m, seq_len, n_mod, pref, align):
    return _tile(m, pref, align) if n_mod == 1 else _tile(seq_len, pref, align)


def _params(*sem):
    return pltpu.CompilerParams(dimension_semantics=sem, vmem_limit_bytes=VMEM_LIMIT_BYTES)


def _rstd(x):
    return lax.rsqrt(jnp.mean(x * x, axis=-1, keepdims=True) + NORM_EPS)


def _norm_mod(x, g, sc, sh):
    return x * _rstd(x) * (g * (1.0 + sc)) + sh


def _per_seq(x, pattern, op):
    rows, n = x.shape
    return op(x.reshape(rows // SUBLANE, SUBLANE, n), pattern[None]).reshape(rows, n)


def _norm_mod_tm(x, g, sc, sh):
    scaled = _per_seq(x * _rstd(x), g * (1.0 + sc), jnp.multiply)
    return _per_seq(scaled, sh, jnp.add)


def _mod_kernel(c_ref, w_ref, b_ref, o_ref):
    c = c_ref[...]
    s = (c * jax.nn.sigmoid(c)).astype(BF16)
    o_ref[0] = jnp.dot(s, w_ref[0].astype(BF16), preferred_element_type=F32) + b_ref[0]


def _modulation(cond, w_mod, b_mod):
    depth, d, n = w_mod.shape
    rows = cond.shape[0]
    tn = _tile(n, 1024, LANE)
    return pl.pallas_call(
        _mod_kernel,
        grid=(depth, n // tn),
        in_specs=[pl.BlockSpec((rows, d), lambda l, j: (0, 0)),
                  pl.BlockSpec((1, d, tn), lambda l, j: (l, 0, j)),
                  pl.BlockSpec((1, 1, tn), lambda l, j: (l, 0, j))],
        out_specs=pl.BlockSpec((1, rows, tn), lambda l, j: (l, 0, j)),
        out_shape=jax.ShapeDtypeStruct((depth, rows, n), F32),
        compiler_params=_params("parallel", "parallel"),
        name="modulation",
    )(cond, w_mod, b_mod.reshape(depth, 1, n))


def _qkv_kernel(*refs, rope, n_rope_tiles, tn):
    if rope:
        x_ref, g_ref, sc_ref, sh_ref, w_ref, cos_ref, sin_ref, o_ref, xm_ref = refs
    else:
        x_ref, g_ref, sc_ref, sh_ref, w_ref, o_ref, xm_ref = refs
    j = pl.program_id(1)

    @pl.when(j == 0)
    def _():
        xm_ref[...] = _norm_mod(x_ref[...], g_ref[...], sc_ref[0], sh_ref[0]).astype(BF16)

    acc = jnp.dot(xm_ref[...], w_ref[...], preferred_element_type=F32)
    if not rope:
        o_ref[...] = acc.astype(o_ref.dtype)
        return

    @pl.when(j < n_rope_tiles)
    def _():
        cos = cos_ref[0]
        sin = sin_ref[0]
        for c in range(tn // LANE):
            x = acc[:, c * LANE:(c + 1) * LANE]
            o_ref[:, c * LANE:(c + 1) * LANE] = (x * cos + pltpu.roll(x, LANE // 2, 1) * sin).astype(o_ref.dtype)

    @pl.when(j >= n_rope_tiles)
    def _():
        o_ref[...] = acc.astype(o_ref.dtype)


def _qkv_proj(x, g, sc, sh, w, seq_len, out_dtype, rope_tables=None, n_rope_cols=0):
    m, d = x.shape
    n = w.shape[1]
    n_mod = sc.shape[0]
    tm = _tile(seq_len, 1024, BF16_ROWS) if rope_tables is not None else _row_tile(m, seq_len, n_mod, 1024, BF16_ROWS)
    tn = _tile(n_rope_cols if rope_tables is not None else n, 512, LANE)

    def mod_i(i):
        return (i * tm // seq_len) % n_mod

    in_specs = [pl.BlockSpec((tm, d), lambda i, j: (i, 0)),
                pl.BlockSpec((1, d), lambda i, j: (0, 0)),
                pl.BlockSpec((1, 1, d), lambda i, j: (mod_i(i), 0, 0)),
                pl.BlockSpec((1, 1, d), lambda i, j: (mod_i(i), 0, 0)),
                pl.BlockSpec((d, tn), lambda i, j: (0, j))]
    args = [x, g, sc, sh, w]
    if rope_tables is not None:
        tiles_per_seq = seq_len // tm
        n_q_tiles = n_rope_cols // (2 * tn)
        in_specs += [pl.BlockSpec((1, tm, LANE),
                                  lambda i, j: (jnp.where(j < n_q_tiles, 0, 1), i % tiles_per_seq, 0))] * 2
        args += list(rope_tables)
    return pl.pallas_call(
        functools.partial(_qkv_kernel, rope=rope_tables is not None,
                          n_rope_tiles=n_rope_cols // tn, tn=tn),
        grid=(m // tm, n // tn),
        in_specs=in_specs,
        out_specs=pl.BlockSpec((tm, tn), lambda i, j: (i, j)),
        out_shape=jax.ShapeDtypeStruct((m, n), out_dtype),
        scratch_shapes=[pltpu.VMEM((tm, d), BF16)],
        compiler_params=_params("parallel", "arbitrary"),
        name="qkv_proj",
    )(*args)


def _swap_middle_quarters(a, head_dim):
    lead = a.shape[:-1]
    a = a.reshape(lead + (a.shape[-1] // head_dim, 2, 2, head_dim // 4))
    return jnp.swapaxes(a, -3, -2).reshape(lead + (-1,))


def _rope_tables(length, head_dim, q_scale):
    pos = jnp.arange(length)
    r = (pos // GRID_W).astype(F32)
    col = (pos % GRID_W).astype(F32)
    half = head_dim // 2
    inv = ROPE_THETA ** (-jnp.arange(0, half, 2, dtype=F32) / half)
    ar = r[:, None] * inv
    ac = col[:, None] * inv
    cos = jnp.concatenate([jnp.cos(ar), jnp.cos(ac)] * 2, axis=-1)
    sin = jnp.concatenate([-jnp.sin(ar), -jnp.sin(ac), jnp.sin(ar), jnp.sin(ac)], axis=-1)
    scales = jnp.array([q_scale, 1.0], F32)[:, None, None]
    return cos[None] * scales, sin[None] * scales


def _attn_kernel(*refs, has_cache, lam_init, q_scale, head_dim, kv_chunk, heads):
    if has_cache:
        lam_ref, q_ref, k_ref, v_ref, ck_ref, cv_ref, g_ref, o_ref = refs
    else:
        lam_ref, q_ref, k_ref, v_ref, g_ref, o_ref = refs
    hw = 2 * head_dim
    lv = lam_ref[...]
    lam = (jnp.exp(jnp.sum(lv[0:1] * lv[1:2], axis=-1, keepdims=True))
           - jnp.exp(jnp.sum(lv[2:3] * lv[3:4], axis=-1, keepdims=True)) + lam_init)
    for hh in range(heads):
        cols = slice(hh * hw, (hh + 1) * hw)
        q = q_ref[:, cols] if q_scale is None else (q_ref[:, cols] * q_scale).astype(BF16)
        ks = [k_ref[:, cols].astype(BF16)]
        vs = [v_ref[:, cols].astype(BF16)]
        if has_cache:
            ks.append(ck_ref[0, 0, :, cols].astype(BF16))
            vs.append(cv_ref[0, 0, :, cols].astype(BF16))
        kv = [(k[r:r + kv_chunk], v[r:r + kv_chunk]) for k, v in zip(ks, vs) for r in range(0, k.shape[0], kv_chunk)]
        outs = []
        for c in range(2):
            qc = q[:, c * head_dim:(c + 1) * head_dim]
            m = denom = acc = None
            for k, v in kv:
                s = lax.dot_general(qc, k[:, c * head_dim:(c + 1) * head_dim],
                                    (((1,), (1,)), ((), ())), preferred_element_type=F32)
                s_max = jnp.max(s, axis=-1, keepdims=True)
                m_new = s_max if m is None else jnp.maximum(m, s_max)
                p = jnp.exp2(s - m_new)
                p_sum = jnp.sum(p, axis=-1, keepdims=True)
                pv = jnp.dot(p.astype(BF16), v, preferred_element_type=F32)
                if m is None:
                    denom, acc = p_sum, pv
                else:
                    alpha = jnp.exp2(m - m_new)
                    denom, acc = denom * alpha + p_sum, acc * alpha + pv
                m = m_new
            outs.append(acc / denom)
        o = outs[0] - lam * outs[1]
        r = lax.rsqrt(jnp.mean(o * o, axis=-1, keepdims=True) + SUBLN_EPS)
        o_ref[:, cols] = (o * r * g_ref[...] * (1.0 - lam_init)).astype(o_ref.dtype)


def _diff_attention(qkv, lam_vecs, subln_g, lam_init, q_scale, batch, seq_len, n_heads, head_dim,
                    cache_k=None, cache_v=None, slot=0):
    hw = 2 * head_dim
    tq = _tile(seq_len, 1024, BF16_ROWS)
    nq = seq_len // tq
    heads = _tile(n_heads, max(2048 // seq_len, 1), 1)
    bw = heads * hw
    ng = n_heads // heads
    has_cache = cache_k is not None
    in_specs = [pl.BlockSpec(lam_vecs.shape, lambda b, h, i: (0, 0)),
                pl.BlockSpec((tq, bw), lambda b, h, i: (b * nq + i, h)),
                pl.BlockSpec((seq_len, bw), lambda b, h, i: (b, ng + h)),
                pl.BlockSpec((seq_len, bw), lambda b, h, i: (b, 2 * ng + h))]
    args = [lam_vecs, qkv, qkv, qkv]
    if has_cache:
        past = cache_k.shape[2]
        in_specs += [pl.BlockSpec((1, 1, past, bw), lambda b, h, i: (b, slot, 0, h))] * 2
        args += [cache_k, cache_v]
    in_specs.append(pl.BlockSpec((1, hw), lambda b, h, i: (0, 0)))
    args.append(subln_g)
    return pl.pallas_call(
        functools.partial(_attn_kernel, has_cache=has_cache, lam_init=lam_init,
                          q_scale=q_scale, head_dim=head_dim,
                          kv_chunk=_tile(seq_len, 512, BF16_ROWS), heads=heads),
        grid=(batch, ng, nq),
        in_specs=in_specs,
        out_specs=pl.BlockSpec((tq, bw), lambda b, h, i: (b * nq + i, h)),
        out_shape=jax.ShapeDtypeStruct((batch * seq_len, n_heads * hw), BF16),
        compiler_params=_params("parallel", "parallel", "arbitrary"),
        name="diff_attention",
    )(*args)


def _attn_out_kernel(a_ref, w_ref, x_ref, gate_ref, o_ref):
    nb, tl, k = a_ref.shape
    acc = jnp.dot(a_ref[...].reshape(nb * tl, k), w_ref[...], preferred_element_type=F32)
    res = x_ref[...] + gate_ref[...] * acc.reshape(nb, tl, -1)
    o_ref[...] = pltpu.einshape("btn->tbn", res).reshape(tl * nb, -1)


def _attn_out_proj(o, w, x, gate, batch, seq_len):
    k = o.shape[1]
    n = w.shape[1]
    tl = _tile(seq_len, 128, BF16_ROWS)
    tn = _tile(n, 512, LANE)
    nl = seq_len // tl
    return pl.pallas_call(
        _attn_out_kernel,
        grid=(batch // SUBLANE, nl, n // tn),
        in_specs=[pl.BlockSpec((SUBLANE, tl, k), lambda s, t, j: (s, t, 0)),
                  pl.BlockSpec((k, tn), lambda s, t, j: (0, j)),
                  pl.BlockSpec((SUBLANE, tl, tn), lambda s, t, j: (s, t, j)),
                  pl.BlockSpec((SUBLANE, 1, tn), lambda s, t, j: (s, 0, j))],
        out_specs=pl.BlockSpec((tl * SUBLANE, tn), lambda s, t, j: (s * nl + t, j)),
        out_shape=jax.ShapeDtypeStruct((batch * seq_len, n), F32),
        compiler_params=_params("parallel", "parallel", "arbitrary"),
        name="attn_out_proj",
    )(o.reshape(batch, seq_len, k), w, x.reshape(batch, seq_len, n), gate)


def _to_batch_major_kernel(x_ref, o_ref):
    nb, tl, d = o_ref.shape
    o_ref[...] = pltpu.einshape("tbd->btd", x_ref[...].reshape(tl, nb, d))


def _to_batch_major(x, batch, seq_len):
    m, d = x.shape
    tl = _tile(seq_len, 64, SUBLANE)
    nl = seq_len // tl
    return pl.pallas_call(
        _to_batch_major_kernel,
        grid=(batch // SUBLANE, nl),
        in_specs=[pl.BlockSpec((tl * SUBLANE, d), lambda s, t: (s * nl + t, 0))],
        out_specs=pl.BlockSpec((SUBLANE, tl, d), lambda s, t: (s, t, 0)),
        out_shape=jax.ShapeDtypeStruct((batch, seq_len, d), F32),
        compiler_params=_params("parallel", "parallel"),
        name="to_batch_major",
    )(x).reshape(m, d)


def _mm_res_kernel(a_ref, w_ref, x_ref, gate_ref, o_ref):
    acc = jnp.dot(a_ref[...], w_ref[...], preferred_element_type=F32)
    o_ref[...] = x_ref[...] + _per_seq(acc, gate_ref[0], jnp.multiply)


def _matmul_gate_residual(a, w, x, gate, seq_len):
    m, k = a.shape
    n = w.shape[1]
    group_rows = seq_len * SUBLANE
    tm = _tile(group_rows, 1024, BF16_ROWS)
    tn = _tile(n, 512, LANE)
    return pl.pallas_call(
        _mm_res_kernel,
        grid=(m // tm, n // tn),
        in_specs=[pl.BlockSpec((tm, k), lambda i, j: (i, 0)),
                  pl.BlockSpec((k, tn), lambda i, j: (0, j)),
                  pl.BlockSpec((tm, tn), lambda i, j: (i, j)),
                  pl.BlockSpec((1, SUBLANE, tn), lambda i, j: (i * tm // group_rows, 0, j))],
        out_specs=pl.BlockSpec((tm, tn), lambda i, j: (i, j)),
        out_shape=jax.ShapeDtypeStruct((m, n), F32),
        compiler_params=_params("parallel", "arbitrary"),
        name="matmul_gate_residual",
    )(a, w, x, gate)


def _up_kernel(x_ref, xp_ref, xn_ref, g_ref, sc_ref, sh_ref, wg_ref, wv_ref,
               cwg_ref, cwv_ref, cbg_ref, cbv_ref, o_ref, xm_ref, *, tm, group_rows, chunk):
    halo = BF16_ROWS
    i = pl.program_id(0)
    j = pl.program_id(1)

    @pl.when(j == 0)
    def _():
        g, sc, sh = g_ref[...], sc_ref[0], sh_ref[0]
        xm_ref[0:halo] = _norm_mod_tm(xp_ref[...], g, sc, sh).astype(BF16)
        xm_ref[halo:halo + tm] = _norm_mod_tm(x_ref[...], g, sc, sh).astype(BF16)
        xm_ref[halo + tm:] = _norm_mod_tm(xn_ref[...], g, sc, sh).astype(BF16)

        @pl.when((i * tm) % group_rows == 0)
        def _():
            xm_ref[0:halo] = jnp.zeros((halo, xm_ref.shape[1]), BF16)

        @pl.when(((i + 1) * tm) % group_rows == 0)
        def _():
            xm_ref[halo + tm:] = jnp.zeros((halo, xm_ref.shape[1]), BF16)

    xm = xm_ref[...]

    def conv(h, cw_ref, cb_ref, cols):
        cw = cw_ref[:, cols]
        lo = halo - SUBLANE
        return (h[lo:lo + tm] * cw[0:1] + h[halo:halo + tm] * cw[1:2]
                + h[halo + SUBLANE:halo + SUBLANE + tm] * cw[2:3] + cb_ref[:, cols])

    for c in range(o_ref.shape[1] // chunk):
        cols = slice(c * chunk, (c + 1) * chunk)
        gate = conv(jnp.dot(xm, wg_ref[:, cols], preferred_element_type=F32), cwg_ref, cbg_ref, cols)
        val = conv(jnp.dot(xm, wv_ref[:, cols], preferred_element_type=F32), cwv_ref, cbv_ref, cols)
        o_ref[:, cols] = (gate * jax.nn.sigmoid(gate) * val).astype(o_ref.dtype)


def _conv_ffn_up(x, g, sc, sh, w_up, conv_w, conv_b, seq_len):
    m, d = x.shape
    d_ff = w_up.shape[1] // 2
    halo = BF16_ROWS
    group_rows = seq_len * SUBLANE
    tm = _tile(group_rows, 1024, halo)
    tn = _tile(d_ff, 512, LANE)
    chunk = _tile(tn, MXU_DIM, LANE)
    nj = d_ff // tn
    hb = tm // halo
    last_hb = m // halo - 1

    mod_spec = pl.BlockSpec((1, SUBLANE, d), lambda i, j: (i * tm // group_rows, 0, 0))
    return pl.pallas_call(
        functools.partial(_up_kernel, tm=tm, group_rows=group_rows, chunk=chunk),
        grid=(m // tm, nj),
        in_specs=[pl.BlockSpec((tm, d), lambda i, j: (i, 0)),
                  pl.BlockSpec((halo, d), lambda i, j: (jnp.maximum(i * hb - 1, 0), 0)),
                  pl.BlockSpec((halo, d), lambda i, j: (jnp.minimum((i + 1) * hb, last_hb), 0)),
                  pl.BlockSpec((1, d), lambda i, j: (0, 0)),
                  mod_spec, mod_spec,
                  pl.BlockSpec((d, tn), lambda i, j: (0, j)),
                  pl.BlockSpec((d, tn), lambda i, j: (0, nj + j)),
                  pl.BlockSpec((3, tn), lambda i, j: (0, j)),
                  pl.BlockSpec((3, tn), lambda i, j: (0, nj + j)),
                  pl.BlockSpec((1, tn), lambda i, j: (0, j)),
                  pl.BlockSpec((1, tn), lambda i, j: (0, nj + j))],
        out_specs=pl.BlockSpec((tm, tn), lambda i, j: (i, j)),
        out_shape=jax.ShapeDtypeStruct((m, d_ff), BF16),
        scratch_shapes=[pltpu.VMEM((tm + 2 * halo, d), BF16)],
        compiler_params=_params("parallel", "arbitrary"),
        name="conv_ffn_up",
    )(x, x, x, g, sc, sh, w_up, w_up, conv_w, conv_w, conv_b, conv_b)


def _discretize(a_re, a_im, log_step):
    dt = jnp.exp(log_step)
    mag = jnp.exp(a_re * dt)
    lb_re = mag * jnp.cos(a_im * dt)
    lb_im = mag * jnp.sin(a_im * dt)
    num_re = lb_re - 1.0
    den = a_re * a_re + a_im * a_im
    f_re = (num_re * a_re + lb_im * a_im) / den
    f_im = (lb_im * a_re - num_re * a_im) / den
    return lb_re, lb_im, f_re, f_im


def _s5_disc_kernel(a_re_ref, a_im_ref, ls_ref, ax_re_ref, ax_im_ref, lsx_ref, b_re_ref, b_im_ref,
                    lb_re_ref, lb_im_ref, bb_re_ref, bb_im_ref):
    lb_re, lb_im, _, _ = _discretize(a_re_ref[...], a_im_ref[...], ls_ref[...])
    lb_re_ref[...] = lb_re
    lb_im_ref[...] = lb_im
    _, _, f_re, f_im = _discretize(ax_re_ref[...], ax_im_ref[...], lsx_ref[...])
    b_re, b_im = b_re_ref[...], b_im_ref[...]
    bb_re_ref[...] = f_re * b_re - f_im * b_im
    bb_im_ref[...] = f_re * b_im + f_im * b_re


def _s5_discretize(a_re, a_im, log_step, b_re, b_im):
    two, g, p, c = b_re.shape
    rows = two * g
    a2 = lambda a: a.reshape(rows, p)
    ax = lambda a: jnp.repeat(a.reshape(rows, p), c, axis=-1)
    ls = jnp.broadcast_to(log_step.reshape(rows, 1), (rows, p))
    lsx = jnp.broadcast_to(log_step.reshape(rows, 1), (rows, p * c))
    return pl.pallas_call(
        _s5_disc_kernel,
        out_shape=[jax.ShapeDtypeStruct((rows, p), F32)] * 2 + [jax.ShapeDtypeStruct((rows, p * c), F32)] * 2,
        compiler_params=pltpu.CompilerParams(vmem_limit_bytes=VMEM_LIMIT_BYTES),
        name="s5_discretize",
    )(a2(a_re), a2(a_im), ls, ax(a_re), ax(a_im), lsx, b_re.reshape(rows, p * c), b_im.reshape(rows, p * c))


def _s5_in_kernel(x_ref, g_ref, sc_ref, sh_ref, o_ref):
    o_ref[...] = _norm_mod_tm(x_ref[...], g_ref[...], sc_ref[0], sh_ref[0]).astype(o_ref.dtype)


def _s5_input(x, g, sc, sh, seq_len):
    m, d = x.shape
    group_rows = seq_len * SUBLANE
    tm = _tile(group_rows, 512, BF16_ROWS)
    mod_spec = pl.BlockSpec((1, SUBLANE, d), lambda i: (i * tm // group_rows, 0, 0))
    return pl.pallas_call(
        _s5_in_kernel,
        grid=(m // tm,),
        in_specs=[pl.BlockSpec((tm, d), lambda i: (i, 0)), pl.BlockSpec((1, d), lambda i: (0, 0)),
                  mod_spec, mod_spec],
        out_specs=pl.BlockSpec((tm, d), lambda i: (i, 0)),
        out_shape=jax.ShapeDtypeStruct((m, d), BF16),
        compiler_params=_params("parallel"),
        name="s5_input",
    )(x, g, sc, sh)


def _s5_scan_kernel(u_ref, wb_ref, wc_ref, lam_ref, h0_ref, y_ref, fin_ref, st_ref, buf_ref,
                    *, nsub, nt, sub_t):
    d = pl.program_id(1)
    t_idx = pl.program_id(3)
    n_tiles = st_ref.shape[0]
    half = n_tiles // 2
    sr = sub_t * SUBLANE
    steps = sub_t // n_tiles

    @pl.when(t_idx == 0)
    def _():
        st_ref[...] = h0_ref[0, 0, 0]

    lam = jnp.broadcast_to(lam_ref[0, 0], st_ref.shape)
    a_re, a_im = lam[:half], lam[half:]

    def first_row(p):
        return pl.multiple_of((p + d * (nsub - 1 - 2 * p)) * sr, sr)

    def run_pass(p, state, do_bu, do_scan, do_ch):
        slot_bu, slot_scan, slot_ch = p % 3, (p + 2) % 3, (p + 1) % 3
        if do_ch:
            h_all = jnp.concatenate([buf_ref[slot_ch, i] for i in range(n_tiles)], axis=1)
            y_ref[0, pl.ds(first_row(p - 2), sr), :] = jnp.dot(h_all.astype(BF16), wc_ref[0, 0],
                                                               preferred_element_type=F32)
        if do_bu:
            u_sub = u_ref[pl.ds(first_row(p), sr), :]
        h_re, h_im = state
        for i in range(n_tiles):
            if do_bu:
                buf_ref[slot_bu, i] = jnp.dot(u_sub, wb_ref[0, 0, i], preferred_element_type=F32)
            if do_scan:
                for k in range(steps):
                    t = i * steps + k
                    row = pl.multiple_of((t + d * (sub_t - 1 - 2 * t)) * SUBLANE, SUBLANE)
                    bu = buf_ref[slot_scan, :, pl.ds(row, SUBLANE), :]
                    n_re = a_re * h_re - a_im * h_im + bu[:half]
                    n_im = a_re * h_im + a_im * h_re + bu[half:]
                    buf_ref[slot_scan, 0:half, pl.ds(row, SUBLANE), :] = n_re
                    buf_ref[slot_scan, half:, pl.ds(row, SUBLANE), :] = n_im
                    h_re, h_im = n_re, n_im
        return h_re, h_im

    state = (st_ref[0:half], st_ref[half:])
    steady = range(2, nsub)
    for p in range(min(2, nsub + 2)):
        state = run_pass(p, state, p < nsub, 1 <= p <= nsub, False)
    if len(steady) > 0:
        state = lax.fori_loop(steady.start, steady.stop,
                              lambda p, st: run_pass(p, st, True, True, True), state)
    for p in range(max(2, nsub), nsub + 2):
        state = run_pass(p, state, p < nsub, p <= nsub, True)
    st_ref[0:half] = state[0]
    st_ref[half:] = state[1]

    @pl.when(t_idx == nt - 1)
    def _():
        fin_ref[0, 0, 0] = st_ref[...]


def _s5_scan(u, wb, wc, lam, h0, seq_len):
    rows, d = u.shape
    ns = rows // (seq_len * SUBLANE)
    ngb = d // MXU_DIM
    n_tiles = wb.shape[2]
    sub_t = _tile(seq_len, 64, n_tiles)
    tc = _tile(seq_len, 16 * sub_t, sub_t)
    nt = seq_len // tc
    assert sub_t % n_tiles == 0 and tc % sub_t == 0

    def t_blk(n, dd, t):
        return n * nt + t + dd * (nt - 1 - 2 * t)

    w_spec = pl.BlockSpec((1, 1, n_tiles, MXU_DIM, MXU_DIM), lambda n, dd, gb, t: (dd, gb, 0, 0, 0))
    st_spec = pl.BlockSpec((1, 1, 1, n_tiles, SUBLANE, MXU_DIM), lambda n, dd, gb, t: (n, dd, gb, 0, 0, 0))
    return pl.pallas_call(
        functools.partial(_s5_scan_kernel, nsub=tc // sub_t, nt=nt, sub_t=sub_t),
        grid=(ns, 2, ngb, nt),
        in_specs=[pl.BlockSpec((tc * SUBLANE, MXU_DIM), lambda n, dd, gb, t: (t_blk(n, dd, t), gb)),
                  w_spec,
                  pl.BlockSpec((1, 1, n_tiles * MXU_DIM, MXU_DIM), lambda n, dd, gb, t: (dd, gb, 0, 0)),
                  pl.BlockSpec((1, 1, n_tiles, 1, MXU_DIM), lambda n, dd, gb, t: (dd, gb, 0, 0, 0)),
                  st_spec],
        out_specs=[pl.BlockSpec((1, tc * SUBLANE, MXU_DIM), lambda n, dd, gb, t: (dd, t_blk(n, dd, t), gb)),
                   st_spec],
        out_shape=[jax.ShapeDtypeStruct((2, rows, d), F32),
                   jax.ShapeDtypeStruct((ns, 2, ngb, n_tiles, SUBLANE, MXU_DIM), F32)],
        scratch_shapes=[pltpu.VMEM((n_tiles, SUBLANE, MXU_DIM), F32),
                        pltpu.VMEM((3, n_tiles, sub_t * SUBLANE, MXU_DIM), F32)],
        compiler_params=_params("parallel", "parallel", "parallel", "arbitrary"),
        name="s5_scan",
    )(u, wb, wc, lam, h0)


def _s5_block_weights(lb_re, lb_im, bb_re, bb_im, c_re, c_im, n_groups, state_dim, group):
    gpb = MXU_DIM // group
    ngb = n_groups // gpb
    eye = jnp.eye(gpb, dtype=F32)

    def b_blocks(bb):
        bb = bb.reshape(2, ngb, gpb, state_dim, group)
        return jnp.einsum('dbgpc,gh->dbgchp', bb, eye).reshape(2, ngb, gpb * group, gpb * state_dim)

    def c_blocks(cc):
        cc = cc.reshape(2, ngb, gpb, group, state_dim)
        return jnp.einsum('dbgcp,gh->dbgphc', cc, eye).reshape(2, ngb, gpb * state_dim, gpb * group)

    wb = jnp.concatenate([b_blocks(bb_re), b_blocks(bb_im)], axis=-1).astype(BF16)
    wc = jnp.concatenate([c_blocks(c_re), -c_blocks(c_im)], axis=-2).astype(BF16)
    lam = jnp.concatenate([lb_re.reshape(2, ngb, 1, gpb * state_dim),
                           lb_im.reshape(2, ngb, 1, gpb * state_dim)], axis=-1)
    n_tiles = wb.shape[-1] // MXU_DIM
    wb = wb.reshape(2, ngb, MXU_DIM, n_tiles, MXU_DIM).transpose(0, 1, 3, 2, 4)
    lam = lam.reshape(2, ngb, n_tiles, 1, MXU_DIM)
    return wb, wc, lam


def _s5_state_blocks(s_re, s_im, ngb):
    b, two, g, p = s_re.shape
    sw = g * p // ngb

    def blk(s):
        return s.reshape(b // SUBLANE, SUBLANE, two, ngb, sw).transpose(0, 2, 3, 1, 4)

    st = jnp.concatenate([blk(s_re), blk(s_im)], axis=-1)
    st = st.reshape(st.shape[:4] + (2 * sw // MXU_DIM, MXU_DIM))
    return st.transpose(0, 1, 2, 4, 3, 5)


def _s5_state_unblock(fin, g, p):
    ns, two, ngb, n_tiles, sub, lanes = fin.shape
    fin = fin.transpose(0, 1, 2, 4, 3, 5).reshape(ns, two, ngb, sub, n_tiles * lanes)
    sw = n_tiles * lanes // 2

    def unblk(s):
        return s.transpose(0, 3, 1, 2, 4).reshape(ns * sub, two, g, p)

    return unblk(fin[..., :sw]), unblk(fin[..., sw:])


def _glu_kernel(x_ref, yf_ref, yb_ref, g_ref, sc_ref, sh_ref, dskip_ref, w_ref, b_ref, gate_ref, o_ref):
    x = x_ref[...]
    xm = _norm_mod_tm(x, g_ref[...], sc_ref[0], sh_ref[0])
    gz = jax.nn.gelu(yf_ref[0] + yb_ref[0] + dskip_ref[...] * xm)
    acc = jnp.dot(gz.astype(BF16), w_ref[...], preferred_element_type=F32) + b_ref[...]
    o_ref[...] = x + _per_seq(gz * jax.nn.sigmoid(acc), gate_ref[0], jnp.multiply)


def _s5_glu(x, y, g, sc, sh, d_skip, w_glu, b_glu, gate, seq_len):
    m, d = x.shape
    group_rows = seq_len * SUBLANE
    tm = _tile(group_rows, 256, BF16_ROWS)
    mod_spec = pl.BlockSpec((1, SUBLANE, d), lambda i: (i * tm // group_rows, 0, 0))
    vec_spec = pl.BlockSpec((1, d), lambda i: (0, 0))
    return pl.pallas_call(
        _glu_kernel,
        grid=(m // tm,),
        in_specs=[pl.BlockSpec((tm, d), lambda i: (i, 0)),
                  pl.BlockSpec((1, tm, d), lambda i: (0, i, 0)),
                  pl.BlockSpec((1, tm, d), lambda i: (1, i, 0)),
                  vec_spec, mod_spec, mod_spec, vec_spec,
                  pl.BlockSpec((d, d), lambda i: (0, 0)),
                  vec_spec, mod_spec],
        out_specs=pl.BlockSpec((tm, d), lambda i: (i, 0)),
        out_shape=jax.ShapeDtypeStruct((m, d), F32),
        compiler_params=_params("parallel"),
        name="s5_glu",
    )(x, y, y, g, sc, sh, d_skip, w_glu, b_glu, gate)


def _final_norm_kernel(x_ref, g_ref, o_ref):
    nb, tl, d = o_ref.shape
    x = x_ref[...]
    y = x * _rstd(x) * g_ref[...]
    o_ref[...] = pltpu.einshape("tbd->btd", y.reshape(tl, nb, d))


def _final_norm(x, g, batch, seq_len):
    m, d = x.shape
    tl = _tile(seq_len, 64, SUBLANE)
    nl = seq_len // tl
    return pl.pallas_call(
        _final_norm_kernel,
        grid=(batch // SUBLANE, nl),
        in_specs=[pl.BlockSpec((tl * SUBLANE, d), lambda s, t: (s * nl + t, 0)),
                  pl.BlockSpec((1, d), lambda s, t: (0, 0))],
        out_specs=pl.BlockSpec((SUBLANE, tl, d), lambda s, t: (s, t, 0)),
        out_shape=jax.ShapeDtypeStruct((batch, seq_len, d), F32),
        compiler_params=_params("parallel", "parallel"),
        name="final_norm",
    )(x, g)


def kernel(x_prompt, x_sample, cache_k, cache_v, state_re, state_im, c, c_ctx, w_mod, b_mod, norm_g, w_qkv, lam_vecs, subln_g, w_o, ssm_a_re, ssm_a_im, ssm_log_step, ssm_b_re, ssm_b_im, ssm_c_re, ssm_c_im, ssm_d, w_glu, b_glu, w_up, conv_w, conv_b, w_down, final_g):
    bp, lp, d = x_prompt.shape
    bs, ls, _ = x_sample.shape
    depth = w_mod.shape[0]
    n_heads, head_dim = cache_k.shape[3], cache_k.shape[5]
    n_groups, state_dim, group = ssm_b_re.shape[2], ssm_b_re.shape[3], ssm_b_re.shape[4]
    assert bp % SUBLANE == 0 and bs % SUBLANE == 0 and d % MXU_DIM == 0 and MXU_DIM % group == 0
    assert 2 * head_dim == MXU_DIM and ls % GRID_W == 0

    pad_rows = -(bs + 1) % SUBLANE
    cond = jnp.concatenate([c, c_ctx[None], jnp.zeros((pad_rows, d), F32)], axis=0)
    mod = _modulation(cond, w_mod, b_mod).reshape(depth, cond.shape[0], 6, d)

    streams = [dict(x=x_prompt.reshape(bp * lp, d), batch=bp, seq=lp, time_major=False),
               dict(x=x_sample.reshape(bs * ls, d), batch=bs, seq=ls, time_major=False)]
    new_k, new_v, new_sre, new_sim = [], [], [], []
    q_scale = head_dim ** -0.5 * math.log2(math.e)
    rope = _rope_tables(ls, head_dim, q_scale)
    ngb = d // MXU_DIM

    for i in range(depth):
        mix, slot = i % N_MIXERS, i // N_MIXERS
        mods = [[jnp.broadcast_to(mod[i, bs, j], (bp, d)) for j in range(6)],
                [mod[i, :bs, j] for j in range(6)]]
        g1 = norm_g[i, 0][None]
        g2 = norm_g[i, 1][None]

        def per_group(v):
            return v.reshape(v.shape[0] // SUBLANE, SUBLANE, d)

        if mix == 0:
            lam_init = 0.8 - 0.6 * math.exp(-0.3 * i)
            nqk = n_heads * 2 * head_dim
            wq = jnp.concatenate([_swap_middle_quarters(w_qkv[slot][:, :2 * nqk], head_dim),
                                  w_qkv[slot][:, 2 * nqk:]], axis=1).astype(BF16)
            wo = w_o[slot].astype(BF16)
            sub_g = subln_g[slot][None]
            ck = _swap_middle_quarters(cache_k.reshape(cache_k.shape[:3] + (-1,)), head_dim)
            cv = cache_v.reshape(cache_v.shape[:3] + (-1,))
            for s, (st, md) in enumerate(zip(streams, mods)):
                b, l = st['batch'], st['seq']
                if st['time_major']:
                    st['x'] = _to_batch_major(st['x'], b, l)
                sh1, sc1 = (v[:1, None] if s == 0 else v[:, None] for v in (md[0], md[1]))
                if s == 0:
                    qkv = _qkv_proj(st['x'], g1, sc1, sh1, wq, l, F32)
                    new_k.append(_swap_middle_quarters(qkv[:, nqk:2 * nqk], head_dim)
                                 .reshape(b, l, n_heads, 2, head_dim))
                    new_v.append(qkv[:, 2 * nqk:].reshape(b, l, n_heads, 2 * head_dim))
                    o = _diff_attention(qkv, lam_vecs[slot], sub_g, lam_init, q_scale, b, l, n_heads, head_dim)
                else:
                    qkv = _qkv_proj(st['x'], g1, sc1, sh1, wq, l, BF16, rope_tables=rope,
                                    n_rope_cols=2 * nqk)
                    o = _diff_attention(qkv, lam_vecs[slot], sub_g, lam_init, None, b, l, n_heads, head_dim,
                                        cache_k=ck, cache_v=cv, slot=slot)
                st['x'] = _attn_out_proj(o, wo, st['x'], md[2][:, None], b, l)
                st['time_major'] = True
        else:
            lb_re, lb_im, bb_re, bb_im = _s5_discretize(ssm_a_re[slot], ssm_a_im[slot], ssm_log_step[slot],
                                                        ssm_b_re[slot], ssm_b_im[slot])
            wb, wc, lam = _s5_block_weights(lb_re, lb_im, bb_re, bb_im, ssm_c_re[slot], ssm_c_im[slot],
                                            n_groups, state_dim, group)
            wg = w_glu[slot].astype(BF16)
            for s, (st, md) in enumerate(zip(streams, mods)):
                b, l = st['batch'], st['seq']
                sh1, sc1, gt1 = per_group(md[0]), per_group(md[1]), per_group(md[2])
                u = _s5_input(st['x'], g1, sc1, sh1, l)
                if s == 0:
                    h0 = jnp.zeros((b // SUBLANE, 2, ngb, wb.shape[2], SUBLANE, MXU_DIM), F32)
                else:
                    h0 = _s5_state_blocks(state_re[:, slot], state_im[:, slot], ngb)
                y, fin = _s5_scan(u, wb, wc, lam, h0, l)
                if s == 0:
                    f_re, f_im = _s5_state_unblock(fin, n_groups, state_dim)
                    new_sre.append(f_re)
                    new_sim.append(f_im)
                st['x'] = _s5_glu(st['x'], y, g1, sc1, sh1, ssm_d[slot][None], wg, b_glu[slot][None], gt1, l)
        wu = w_up[i].astype(BF16)
        wd = w_down[i].astype(BF16)
        for st, md in zip(streams, mods):
            sh2, sc2, gt2 = per_group(md[3]), per_group(md[4]), per_group(md[5])
            act = _conv_ffn_up(st['x'], g2, sc2, sh2, wu, conv_w[i], conv_b[i][None], st['seq'])
            st['x'] = _matmul_gate_residual(act, wd, st['x'], gt2, st['seq'])

    y_prompt = _final_norm(streams[0]['x'], final_g[None], bp, lp)
    y_sample = _final_norm(streams[1]['x'], final_g[None], bs, ls)
    return (y_prompt, y_sample, jnp.stack(new_k, axis=1), jnp.stack(new_v, axis=1),
            jnp.stack(new_sre, axis=1), jnp.stack(new_sim, axis=1))
```

```python
import functools
import math

import jax
import jax.numpy as jnp
from jax import lax
from jax.experimental import pallas as pl
from jax.experimental.pallas import tpu as pltpu

F32 = jnp.float32
BF16 = jnp.bfloat16

GRID_W = 64
ROPE_THETA = 10000.0
NORM_EPS = 1e-6
SUBLN_EPS = 1e-5
N_MIXERS = 2

LANE = 128
SUBLANE = 8
BF16_ROWS = 16
MXU_DIM = 256
VMEM_LIMIT_BYTES = 52 * 1024 * 1024


def _tile(n, pref, align):
    t = min(pref, n)
    t -= t % align
    while t >= align:
        if n % t == 0:
            return t
        t -= align
    return n


def _row_tile(m, seq_len, n_mod, pref, align):
    return _tile(m, pref, align) if n_mod == 1 else _tile(seq_len, pref, align)


def _params(*sem):
    return pltpu.CompilerParams(dimension_semantics=sem, vmem_limit_bytes=VMEM_LIMIT_BYTES)


def _rstd(x):
    return lax.rsqrt(jnp.mean(x * x, axis=-1, keepdims=True) + NORM_EPS)


def _norm_mod(x, g, sc, sh):
    return x * _rstd(x) * (g * (1.0 + sc)) + sh


def _per_seq(x, pattern, op):
    rows, n = x.shape
    return op(x.reshape(rows // SUBLANE, SUBLANE, n), pattern[None]).reshape(rows, n)


def _norm_mod_tm(x, g, sc, sh):
    scaled = _per_seq(x * _rstd(x), g * (1.0 + sc), jnp.multiply)
    return _per_seq(scaled, sh, jnp.add)


def _mod_kernel(c_ref, w_ref, b_ref, o_ref):
    c = c_ref[...]
    s = (c * jax.nn.sigmoid(c)).astype(BF16)
    o_ref[0] = jnp.dot(s, w_ref[0].astype(BF16), preferred_element_type=F32) + b_ref[0]


def _modulation(cond, w_mod, b_mod):
    depth, d, n = w_mod.shape
    rows = cond.shape[0]
    tn = _tile(n, 1024, LANE)
    return pl.pallas_call(
        _mod_kernel,
        grid=(depth, n // tn),
        in_specs=[pl.BlockSpec((rows, d), lambda l, j: (0, 0)),
                  pl.BlockSpec((1, d, tn), lambda l, j: (l, 0, j)),
                  pl.BlockSpec((1, 1, tn), lambda l, j: (l, 0, j))],
        out_specs=pl.BlockSpec((1, rows, tn), lambda l, j: (l, 0, j)),
        out_shape=jax.ShapeDtypeStruct((depth, rows, n), F32),
        compiler_params=_params("parallel", "parallel"),
        name="modulation",
    )(cond, w_mod, b_mod.reshape(depth, 1, n))


def _swap_quarters_in_lanes(x, head_dim):
    q4 = head_dim // 4
    parts = []
    for c in range(x.shape[1] // head_dim):
        base = c * head_dim
        parts += [x[:, base:base + q4], x[:, base + 2 * q4:base + 3 * q4],
                  x[:, base + q4:base + 2 * q4], x[:, base + 3 * q4:base + 4 * q4]]
    return jnp.concatenate(parts, axis=1)


def _cast_kernel(w_ref, o_ref, *, swap_tiles, head_dim):
    w = w_ref[0]
    if swap_tiles == 0:
        o_ref[...] = w.astype(BF16)
        return
    j = pl.program_id(1)

    @pl.when(j < swap_tiles)
    def _():
        o_ref[...] = _swap_quarters_in_lanes(w, head_dim).astype(BF16)

    @pl.when(j >= swap_tiles)
    def _():
        o_ref[...] = w.astype(BF16)


def _to_bf16(w_stack, index, swap_cols=0, head_dim=LANE):
    _, k, n = w_stack.shape
    tk = _tile(k, 512, BF16_ROWS)
    tn = _tile(math.gcd(swap_cols, n), 1024, LANE)
    return pl.pallas_call(
        functools.partial(_cast_kernel, swap_tiles=swap_cols // tn, head_dim=head_dim),
        grid=(k // tk, n // tn),
        in_specs=[pl.BlockSpec((1, tk, tn), lambda i, j: (index, i, j))],
        out_specs=pl.BlockSpec((tk, tn), lambda i, j: (i, j)),
        out_shape=jax.ShapeDtypeStruct((k, n), BF16),
        compiler_params=_params("parallel", "parallel"),
        name="weight_to_bf16",
    )(w_stack)


def _qkv_kernel(*refs, rope, n_rope_tiles, tn, q_scale):
    if rope:
        x_ref, g_ref, sc_ref, sh_ref, w_ref, cos_ref, sin_ref, o_ref, xm_ref = refs
    else:
        x_ref, g_ref, sc_ref, sh_ref, w_ref, oq_ref, ok_ref, ov_ref, xm_ref = refs
    j = pl.program_id(1)

    @pl.when(j == 0)
    def _():
        xm_ref[...] = _norm_mod(x_ref[...], g_ref[...], sc_ref[0], sh_ref[0]).astype(BF16)

    acc = jnp.dot(xm_ref[...], w_ref[...], preferred_element_type=F32)
    if not rope:
        n_q = n_rope_tiles // 2

        @pl.when(j < n_q)
        def _():
            oq_ref[...] = (acc * q_scale).astype(oq_ref.dtype)

        @pl.when((j >= n_q) & (j < 2 * n_q))
        def _():
            ok_ref[...] = acc

        @pl.when(j >= 2 * n_q)
        def _():
            ov_ref[...] = acc

        return

    @pl.when(j < n_rope_tiles)
    def _():
        cos = cos_ref[0]
        sin = sin_ref[0]
        for c in range(tn // LANE):
            x = acc[:, c * LANE:(c + 1) * LANE]
            o_ref[:, c * LANE:(c + 1) * LANE] = (x * cos + pltpu.roll(x, LANE // 2, 1) * sin).astype(o_ref.dtype)

    @pl.when(j >= n_rope_tiles)
    def _():
        o_ref[...] = acc.astype(o_ref.dtype)


def _qkv_proj(x, g, sc, sh, w, seq_len, q_scale, rope_tables=None):
    m, d = x.shape
    n = w.shape[1]
    n_rope_cols = 2 * n // 3
    n_mod = sc.shape[0]
    tm = _tile(seq_len, 1024, BF16_ROWS) if rope_tables is not None else _row_tile(m, seq_len, n_mod, 1024, BF16_ROWS)
    tn = _tile(n_rope_cols // 2, 512, LANE)
    nq = n_rope_cols // (2 * tn)

    def mod_i(i):
        return (i * tm // seq_len) % n_mod

    in_specs = [pl.BlockSpec((tm, d), lambda i, j: (i, 0)),
                pl.BlockSpec((1, d), lambda i, j: (0, 0)),
                pl.BlockSpec((1, 1, d), lambda i, j: (mod_i(i), 0, 0)),
                pl.BlockSpec((1, 1, d), lambda i, j: (mod_i(i), 0, 0)),
                pl.BlockSpec((d, tn), lambda i, j: (0, j))]
    args = [x, g, sc, sh, w]
    if rope_tables is not None:
        tiles_per_seq = seq_len // tm
        in_specs += [pl.BlockSpec((1, tm, LANE),
                                  lambda i, j: (jnp.where(j < nq, 0, 1), i % tiles_per_seq, 0))] * 2
        args += list(rope_tables)
        out_specs = pl.BlockSpec((tm, tn), lambda i, j: (i, j))
        out_shape = jax.ShapeDtypeStruct((m, n), BF16)
    else:
        out_specs = [pl.BlockSpec((tm, tn), lambda i, j, part=part: (i, jnp.clip(j - part * nq, 0, nq - 1)))
                     for part in range(3)]
        out_shape = [jax.ShapeDtypeStruct((m, n // 3), dt) for dt in (BF16, F32, F32)]
    return pl.pallas_call(
        functools.partial(_qkv_kernel, rope=rope_tables is not None,
                          n_rope_tiles=n_rope_cols // tn, tn=tn, q_scale=q_scale),
        grid=(m // tm, n // tn),
        in_specs=in_specs,
        out_specs=out_specs,
        out_shape=out_shape,
        scratch_shapes=[pltpu.VMEM((tm, d), BF16)],
        compiler_params=_params("parallel", "arbitrary"),
        name="qkv_proj",
    )(*args)


def _swap_middle_quarters(a, head_dim):
    lead = a.shape[:-1]
    a = a.reshape(lead + (a.shape[-1] // head_dim, 2, 2, head_dim // 4))
    return jnp.swapaxes(a, -3, -2).reshape(lead + (-1,))


def _rope_tables(length, head_dim, q_scale):
    pos = jnp.arange(length)
    r = (pos // GRID_W).astype(F32)
    col = (pos % GRID_W).astype(F32)
    half = head_dim // 2
    inv = ROPE_THETA ** (-jnp.arange(0, half, 2, dtype=F32) / half)
    ar = r[:, None] * inv
    ac = col[:, None] * inv
    cos = jnp.concatenate([jnp.cos(ar), jnp.cos(ac)] * 2, axis=-1)
    sin = jnp.concatenate([-jnp.sin(ar), -jnp.sin(ac), jnp.sin(ar), jnp.sin(ac)], axis=-1)
    scales = jnp.array([q_scale, 1.0], F32)[:, None, None]
    return cos[None] * scales, sin[None] * scales


def _attn_kernel(*refs, has_cache, lam_init, head_dim, kv_chunk, heads):
    if has_cache:
        lam_ref, q_ref, k_ref, v_ref, ck_ref, cv_ref, g_ref, o_ref = refs
    else:
        lam_ref, q_ref, k_ref, v_ref, g_ref, o_ref = refs
    hw = 2 * head_dim
    lv = lam_ref[...]
    lam = (jnp.exp(jnp.sum(lv[0:1] * lv[1:2], axis=-1, keepdims=True))
           - jnp.exp(jnp.sum(lv[2:3] * lv[3:4], axis=-1, keepdims=True)) + lam_init)
    for hh in range(heads):
        cols = slice(hh * hw, (hh + 1) * hw)
        q = q_ref[:, cols]
        ks = [k_ref[:, cols].astype(BF16)]
        vs = [v_ref[:, cols].astype(BF16)]
        if has_cache:
            ks.append(_swap_quarters_in_lanes(ck_ref[0, 0, :, cols], head_dim).astype(BF16))
            vs.append(cv_ref[0, 0, :, cols].astype(BF16))
        kv = [(k[r:r + kv_chunk], v[r:r + kv_chunk]) for k, v in zip(ks, vs) for r in range(0, k.shape[0], kv_chunk)]
        outs = []
        for c in range(2):
            qc = q[:, c * head_dim:(c + 1) * head_dim]
            m = denom = acc = None
            for k, v in kv:
                s = lax.dot_general(qc, k[:, c * head_dim:(c + 1) * head_dim],
                                    (((1,), (1,)), ((), ())), preferred_element_type=F32)
                s_max = jnp.max(s, axis=-1, keepdims=True)
                m_new = s_max if m is None else jnp.maximum(m, s_max)
                p = jnp.exp2(s - m_new)
                p_sum = jnp.sum(p, axis=-1, keepdims=True)
                pv = jnp.dot(p.astype(BF16), v, preferred_element_type=F32)
                if m is None:
                    denom, acc = p_sum, pv
                else:
                    alpha = jnp.exp2(m - m_new)
                    denom, acc = denom * alpha + p_sum, acc * alpha + pv
                m = m_new
            outs.append(acc / denom)
        o = outs[0] - lam * outs[1]
        r = lax.rsqrt(jnp.mean(o * o, axis=-1, keepdims=True) + SUBLN_EPS)
        o_ref[:, cols] = (o * r * g_ref[...] * (1.0 - lam_init)).astype(o_ref.dtype)


def _diff_attention(qkv, lam_vecs, subln_g, lam_init, batch, seq_len, n_heads, head_dim,
                    cache_k=None, cache_v=None, slot=0):
    hw = 2 * head_dim
    tq = _tile(seq_len, 1024, BF16_ROWS)
    nq = seq_len // tq
    heads = _tile(n_heads, max(2048 // seq_len, 1), 1)
    bw = heads * hw
    ng = n_heads // heads
    has_cache = cache_k is not None
    if isinstance(qkv, tuple):
        q_arr, k_arr, v_arr = qkv
        k_off = v_off = 0
    else:
        q_arr = k_arr = v_arr = qkv
        k_off, v_off = ng, 2 * ng
    in_specs = [pl.BlockSpec(lam_vecs.shape, lambda b, h, i: (0, 0)),
                pl.BlockSpec((tq, bw), lambda b, h, i: (b * nq + i, h)),
                pl.BlockSpec((seq_len, bw), lambda b, h, i: (b, k_off + h)),
                pl.BlockSpec((seq_len, bw), lambda b, h, i: (b, v_off + h))]
    args = [lam_vecs, q_arr, k_arr, v_arr]
    if has_cache:
        past = cache_k.shape[2]
        in_specs += [pl.BlockSpec((1, 1, past, bw), lambda b, h, i: (b, slot, 0, h))] * 2
        args += [cache_k, cache_v]
    in_specs.append(pl.BlockSpec((1, hw), lambda b, h, i: (0, 0)))
    args.append(subln_g)
    return pl.pallas_call(
        functools.partial(_attn_kernel, has_cache=has_cache, lam_init=lam_init,
                          head_dim=head_dim, kv_chunk=_tile(seq_len, 512, BF16_ROWS), heads=heads),
        grid=(batch, ng, nq),
        in_specs=in_specs,
        out_specs=pl.BlockSpec((tq, bw), lambda b, h, i: (b * nq + i, h)),
        out_shape=jax.ShapeDtypeStruct((batch * seq_len, n_heads * hw), BF16),
        compiler_params=_params("parallel", "parallel", "arbitrary"),
        name="diff_attention",
    )(*args)


def _attn_out_kernel(a_ref, w_ref, x_ref, gate_ref, o_ref):
    nb, tl, k = a_ref.shape
    acc = jnp.dot(a_ref[...].reshape(nb * tl, k), w_ref[...], preferred_element_type=F32)
    res = x_ref[...] + gate_ref[...] * acc.reshape(nb, tl, -1)
    o_ref[...] = pltpu.einshape("btn->tbn", res).reshape(tl * nb, -1)


def _attn_out_proj(o, w, x, gate, batch, seq_len):
    k = o.shape[1]
    n = w.shape[1]
    tl = _tile(seq_len, 128, BF16_ROWS)
    tn = _tile(n, 512, LANE)
    nl = seq_len // tl
    return pl.pallas_call(
        _attn_out_kernel,
        grid=(batch // SUBLANE, nl, n // tn),
        in_specs=[pl.BlockSpec((SUBLANE, tl, k), lambda s, t, j: (s, t, 0)),
                  pl.BlockSpec((k, tn), lambda s, t, j: (0, j)),
                  pl.BlockSpec((SUBLANE, tl, tn), lambda s, t, j: (s, t, j)),
                  pl.BlockSpec((SUBLANE, 1, tn), lambda s, t, j: (s, 0, j))],
        out_specs=pl.BlockSpec((tl * SUBLANE, tn), lambda s, t, j: (s * nl + t, j)),
        out_shape=jax.ShapeDtypeStruct((batch * seq_len, n), F32),
        compiler_params=_params("parallel", "parallel", "arbitrary"),
        name="attn_out_proj",
    )(o.reshape(batch, seq_len, k), w, x.reshape(batch, seq_len, n), gate)


def _to_batch_major_kernel(x_ref, o_ref):
    nb, tl, d = o_ref.shape
    o_ref[...] = pltpu.einshape("tbd->btd", x_ref[...].reshape(tl, nb, d))


def _to_batch_major(x, batch, seq_len):
    m, d = x.shape
    tl = _tile(seq_len, 64, SUBLANE)
    nl = seq_len // tl
    return pl.pallas_call(
        _to_batch_major_kernel,
        grid=(batch // SUBLANE, nl),
        in_specs=[pl.BlockSpec((tl * SUBLANE, d), lambda s, t: (s * nl + t, 0))],
        out_specs=pl.BlockSpec((SUBLANE, tl, d), lambda s, t: (s, t, 0)),
        out_shape=jax.ShapeDtypeStruct((batch, seq_len, d), F32),
        compiler_params=_params("parallel", "parallel"),
        name="to_batch_major",
    )(x).reshape(m, d)


def _mm_res_kernel(a_ref, w_ref, x_ref, gate_ref, o_ref):
    acc = jnp.dot(a_ref[...], w_ref[...], preferred_element_type=F32)
    o_ref[...] = x_ref[...] + _per_seq(acc, gate_ref[0], jnp.multiply)


def _matmul_gate_residual(a, w, x, gate, seq_len):
    m, k = a.shape
    n = w.shape[1]
    group_rows = seq_len * SUBLANE
    tm = _tile(group_rows, 1024, BF16_ROWS)
    tn = _tile(n, 512, LANE)
    return pl.pallas_call(
        _mm_res_kernel,
        grid=(m // tm, n // tn),
        in_specs=[pl.BlockSpec((tm, k), lambda i, j: (i, 0)),
                  pl.BlockSpec((k, tn), lambda i, j: (0, j)),
                  pl.BlockSpec((tm, tn), lambda i, j: (i, j)),
                  pl.BlockSpec((1, SUBLANE, tn), lambda i, j: (i * tm // group_rows, 0, j))],
        out_specs=pl.BlockSpec((tm, tn), lambda i, j: (i, j)),
        out_shape=jax.ShapeDtypeStruct((m, n), F32),
        compiler_params=_params("parallel", "arbitrary"),
        name="matmul_gate_residual",
    )(a, w, x, gate)


def _up_kernel(x_ref, xp_ref, xn_ref, g_ref, sc_ref, sh_ref, wg_ref, wv_ref,
               cwg_ref, cwv_ref, cbg_ref, cbv_ref, o_ref, xm_ref, *, tm, group_rows, chunk):
    halo = BF16_ROWS
    i = pl.program_id(0)
    j = pl.program_id(1)

    @pl.when(j == 0)
    def _():
        g, sc, sh = g_ref[...], sc_ref[0], sh_ref[0]
        xm_ref[0:halo] = _norm_mod_tm(xp_ref[...], g, sc, sh).astype(BF16)
        xm_ref[halo:halo + tm] = _norm_mod_tm(x_ref[...], g, sc, sh).astype(BF16)
        xm_ref[halo + tm:] = _norm_mod_tm(xn_ref[...], g, sc, sh).astype(BF16)

        @pl.when((i * tm) % group_rows == 0)
        def _():
            xm_ref[0:halo] = jnp.zeros((halo, xm_ref.shape[1]), BF16)

        @pl.when(((i + 1) * tm) % group_rows == 0)
        def _():
            xm_ref[halo + tm:] = jnp.zeros((halo, xm_ref.shape[1]), BF16)

    xm = xm_ref[...]

    def conv(h, cw_ref, cb_ref, cols):
        cw = cw_ref[:, cols]
        lo = halo - SUBLANE
        return (h[lo:lo + tm] * cw[0:1] + h[halo:halo + tm] * cw[1:2]
                + h[halo + SUBLANE:halo + SUBLANE + tm] * cw[2:3] + cb_ref[:, cols])

    for c in range(o_ref.shape[1] // chunk):
        cols = slice(c * chunk, (c + 1) * chunk)
        gate = conv(jnp.dot(xm, wg_ref[:, cols], preferred_element_type=F32), cwg_ref, cbg_ref, cols)
        val = conv(jnp.dot(xm, wv_ref[:, cols], preferred_element_type=F32), cwv_ref, cbv_ref, cols)
        o_ref[:, cols] = (gate * jax.nn.sigmoid(gate) * val).astype(o_ref.dtype)


def _conv_ffn_up(x, g, sc, sh, w_up, conv_w, conv_b, seq_len):
    m, d = x.shape
    d_ff = w_up.shape[1] // 2
    halo = BF16_ROWS
    group_rows = seq_len * SUBLANE
    tm = _tile(group_rows, 1024, halo)
    tn = _tile(d_ff, 512, LANE)
    chunk = _tile(tn, MXU_DIM, LANE)
    nj = d_ff // tn
    hb = tm // halo
    last_hb = m // halo - 1

    mod_spec = pl.BlockSpec((1, SUBLANE, d), lambda i, j: (i * tm // group_rows, 0, 0))
    return pl.pallas_call(
        functools.partial(_up_kernel, tm=tm, group_rows=group_rows, chunk=chunk),
        grid=(m // tm, nj),
        in_specs=[pl.BlockSpec((tm, d), lambda i, j: (i, 0)),
                  pl.BlockSpec((halo, d), lambda i, j: (jnp.maximum(i * hb - 1, 0), 0)),
                  pl.BlockSpec((halo, d), lambda i, j: (jnp.minimum((i + 1) * hb, last_hb), 0)),
                  pl.BlockSpec((1, d), lambda i, j: (0, 0)),
                  mod_spec, mod_spec,
                  pl.BlockSpec((d, tn), lambda i, j: (0, j)),
                  pl.BlockSpec((d, tn), lambda i, j: (0, nj + j)),
                  pl.BlockSpec((3, tn), lambda i, j: (0, j)),
                  pl.BlockSpec((3, tn), lambda i, j: (0, nj + j)),
                  pl.BlockSpec((1, tn), lambda i, j: (0, j)),
                  pl.BlockSpec((1, tn), lambda i, j: (0, nj + j))],
        out_specs=pl.BlockSpec((tm, tn), lambda i, j: (i, j)),
        out_shape=jax.ShapeDtypeStruct((m, d_ff), BF16),
        scratch_shapes=[pltpu.VMEM((tm + 2 * halo, d), BF16)],
        compiler_params=_params("parallel", "arbitrary"),
        name="conv_ffn_up",
    )(x, x, x, g, sc, sh, w_up, w_up, conv_w, conv_w, conv_b, conv_b)


def _discretize(a_re, a_im, log_step):
    dt = jnp.exp(log_step)
    mag = jnp.exp(a_re * dt)
    lb_re = mag * jnp.cos(a_im * dt)
    lb_im = mag * jnp.sin(a_im * dt)
    num_re = lb_re - 1.0
    den = a_re * a_re + a_im * a_im
    f_re = (num_re * a_re + lb_im * a_im) / den
    f_im = (lb_im * a_re - num_re * a_im) / den
    return lb_re, lb_im, f_re, f_im


def _s5_disc_kernel(a_re_ref, a_im_ref, ls_ref, ax_re_ref, ax_im_ref, lsx_ref, b_re_ref, b_im_ref,
                    lb_re_ref, lb_im_ref, bb_re_ref, bb_im_ref):
    lb_re, lb_im, _, _ = _discretize(a_re_ref[...], a_im_ref[...], ls_ref[...])
    lb_re_ref[...] = lb_re
    lb_im_ref[...] = lb_im
    _, _, f_re, f_im = _discretize(ax_re_ref[...], ax_im_ref[...], lsx_ref[...])
    b_re, b_im = b_re_ref[...], b_im_ref[...]
    bb_re_ref[...] = f_re * b_re - f_im * b_im
    bb_im_ref[...] = f_re * b_im + f_im * b_re


def _s5_discretize(a_re, a_im, log_step, b_re, b_im):
    two, g, p, c = b_re.shape
    rows = two * g
    a2 = lambda a: a.reshape(rows, p)
    ax = lambda a: jnp.repeat(a.reshape(rows, p), c, axis=-1)
    ls = jnp.broadcast_to(log_step.reshape(rows, 1), (rows, p))
    lsx = jnp.broadcast_to(log_step.reshape(rows, 1), (rows, p * c))
    return pl.pallas_call(
        _s5_disc_kernel,
        out_shape=[jax.ShapeDtypeStruct((rows, p), F32)] * 2 + [jax.ShapeDtypeStruct((rows, p * c), F32)] * 2,
        compiler_params=pltpu.CompilerParams(vmem_limit_bytes=VMEM_LIMIT_BYTES),
        name="s5_discretize",
    )(a2(a_re), a2(a_im), ls, ax(a_re), ax(a_im), lsx, b_re.reshape(rows, p * c), b_im.reshape(rows, p * c))


def _s5_in_kernel(x_ref, g_ref, sc_ref, sh_ref, o_ref):
    o_ref[...] = _norm_mod_tm(x_ref[...], g_ref[...], sc_ref[0], sh_ref[0]).astype(o_ref.dtype)


def _s5_input(x, g, sc, sh, seq_len):
    m, d = x.shape
    group_rows = seq_len * SUBLANE
    tm = _tile(group_rows, 512, BF16_ROWS)
    mod_spec = pl.BlockSpec((1, SUBLANE, d), lambda i: (i * tm // group_rows, 0, 0))
    return pl.pallas_call(
        _s5_in_kernel,
        grid=(m // tm,),
        in_specs=[pl.BlockSpec((tm, d), lambda i: (i, 0)), pl.BlockSpec((1, d), lambda i: (0, 0)),
                  mod_spec, mod_spec],
        out_specs=pl.BlockSpec((tm, d), lambda i: (i, 0)),
        out_shape=jax.ShapeDtypeStruct((m, d), BF16),
        compiler_params=_params("parallel"),
        name="s5_input",
    )(x, g, sc, sh)


def _s5_scan_kernel(u_ref, wb_ref, wc_ref, lam_ref, h0_ref, y_ref, fin_ref, st_ref, buf_ref,
                    *, nsub, nt, sub_t):
    d = pl.program_id(1)
    t_idx = pl.program_id(3)
    n_tiles = st_ref.shape[0]
    half = n_tiles // 2
    sr = sub_t * SUBLANE
    steps = sub_t // n_tiles

    @pl.when(t_idx == 0)
    def _():
        st_ref[...] = h0_ref[0, 0, 0]

    lam = jnp.broadcast_to(lam_ref[0, 0], st_ref.shape)
    a_re, a_im = lam[:half], lam[half:]

    def first_row(p):
        return pl.multiple_of((p + d * (nsub - 1 - 2 * p)) * sr, sr)

    def run_pass(p, state, do_bu, do_scan, do_ch):
        slot_bu, slot_scan, slot_ch = p % 3, (p + 2) % 3, (p + 1) % 3
        if do_ch:
            h_all = jnp.concatenate([buf_ref[slot_ch, i] for i in range(n_tiles)], axis=1)
            y_ref[0, pl.ds(first_row(p - 2), sr), :] = jnp.dot(h_all.astype(BF16), wc_ref[0, 0],
                                                               preferred_element_type=F32)
        if do_bu:
            u_sub = u_ref[pl.ds(first_row(p), sr), :]
        h_re, h_im = state
        for i in range(n_tiles):
            if do_bu:
                buf_ref[slot_bu, i] = jnp.dot(u_sub, wb_ref[0, 0, i], preferred_element_type=F32)
            if do_scan:
                for k in range(steps):
                    t = i * steps + k
                    row = pl.multiple_of((t + d * (sub_t - 1 - 2 * t)) * SUBLANE, SUBLANE)
                    bu = buf_ref[slot_scan, :, pl.ds(row, SUBLANE), :]
                    n_re = a_re * h_re - a_im * h_im + bu[:half]
                    n_im = a_re * h_im + a_im * h_re + bu[half:]
                    buf_ref[slot_scan, 0:half, pl.ds(row, SUBLANE), :] = n_re
                    buf_ref[slot_scan, half:, pl.ds(row, SUBLANE), :] = n_im
                    h_re, h_im = n_re, n_im
        return h_re, h_im

    state = (st_ref[0:half], st_ref[half:])
    steady = range(2, nsub)
    for p in range(min(2, nsub + 2)):
        state = run_pass(p, state, p < nsub, 1 <= p <= nsub, False)
    if len(steady) > 0:
        state = lax.fori_loop(steady.start, steady.stop,
                              lambda p, st: run_pass(p, st, True, True, True), state)
    for p in range(max(2, nsub), nsub + 2):
        state = run_pass(p, state, p < nsub, p <= nsub, True)
    st_ref[0:half] = state[0]
    st_ref[half:] = state[1]

    @pl.when(t_idx == nt - 1)
    def _():
        fin_ref[0, 0, 0] = st_ref[...]


def _s5_scan(u, wb, wc, lam, h0, seq_len):
    rows, d = u.shape
    ns = rows // (seq_len * SUBLANE)
    ngb = d // MXU_DIM
    n_tiles = wb.shape[2]
    sub_t = _tile(seq_len, 64, n_tiles)
    tc = _tile(seq_len, 16 * sub_t, sub_t)
    nt = seq_len // tc
    assert sub_t % n_tiles == 0 and tc % sub_t == 0

    def t_blk(n, dd, t):
        return n * nt + t + dd * (nt - 1 - 2 * t)

    w_spec = pl.BlockSpec((1, 1, n_tiles, MXU_DIM, MXU_DIM), lambda n, dd, gb, t: (dd, gb, 0, 0, 0))
    st_spec = pl.BlockSpec((1, 1, 1, n_tiles, SUBLANE, MXU_DIM), lambda n, dd, gb, t: (n, dd, gb, 0, 0, 0))
    return pl.pallas_call(
        functools.partial(_s5_scan_kernel, nsub=tc // sub_t, nt=nt, sub_t=sub_t),
        grid=(ns, 2, ngb, nt),
        in_specs=[pl.BlockSpec((tc * SUBLANE, MXU_DIM), lambda n, dd, gb, t: (t_blk(n, dd, t), gb)),
                  w_spec,
                  pl.BlockSpec((1, 1, n_tiles * MXU_DIM, MXU_DIM), lambda n, dd, gb, t: (dd, gb, 0, 0)),
                  pl.BlockSpec((1, 1, n_tiles, 1, MXU_DIM), lambda n, dd, gb, t: (dd, gb, 0, 0, 0)),
                  st_spec],
        out_specs=[pl.BlockSpec((1, tc * SUBLANE, MXU_DIM), lambda n, dd, gb, t: (dd, t_blk(n, dd, t), gb)),
                   st_spec],
        out_shape=[jax.ShapeDtypeStruct((2, rows, d), F32),
                   jax.ShapeDtypeStruct((ns, 2, ngb, n_tiles, SUBLANE, MXU_DIM), F32)],
        scratch_shapes=[pltpu.VMEM((n_tiles, SUBLANE, MXU_DIM), F32),
                        pltpu.VMEM((3, n_tiles, sub_t * SUBLANE, MXU_DIM), F32)],
        compiler_params=_params("parallel", "parallel", "parallel", "arbitrary"),
        name="s5_scan",
    )(u, wb, wc, lam, h0)


def _s5_block_weights(lb_re, lb_im, bb_re, bb_im, c_re, c_im, n_groups, state_dim, group):
    gpb = MXU_DIM // group
    ngb = n_groups // gpb
    eye = jnp.eye(gpb, dtype=F32)

    def b_blocks(bb):
        bb = bb.reshape(2, ngb, gpb, state_dim, group)
        return jnp.einsum('dbgpc,gh->dbgchp', bb, eye).reshape(2, ngb, gpb * group, gpb * state_dim)

    def c_blocks(cc):
        cc = cc.reshape(2, ngb, gpb, group, state_dim)
        return jnp.einsum('dbgcp,gh->dbgphc', cc, eye).reshape(2, ngb, gpb * state_dim, gpb * group)

    wb = jnp.concatenate([b_blocks(bb_re), b_blocks(bb_im)], axis=-1).astype(BF16)
    wc = jnp.concatenate([c_blocks(c_re), -c_blocks(c_im)], axis=-2).astype(BF16)
    lam = jnp.concatenate([lb_re.reshape(2, ngb, 1, gpb * state_dim),
                           lb_im.reshape(2, ngb, 1, gpb * state_dim)], axis=-1)
    n_tiles = wb.shape[-1] // MXU_DIM
    wb = wb.reshape(2, ngb, MXU_DIM, n_tiles, MXU_DIM).transpose(0, 1, 3, 2, 4)
    lam = lam.reshape(2, ngb, n_tiles, 1, MXU_DIM)
    return wb, wc, lam


def _s5_state_blocks(s_re, s_im, ngb):
    b, two, g, p = s_re.shape
    sw = g * p // ngb

    def blk(s):
        return s.reshape(b // SUBLANE, SUBLANE, two, ngb, sw).transpose(0, 2, 3, 1, 4)

    st = jnp.concatenate([blk(s_re), blk(s_im)], axis=-1)
    st = st.reshape(st.shape[:4] + (2 * sw // MXU_DIM, MXU_DIM))
    return st.transpose(0, 1, 2, 4, 3, 5)


def _s5_state_unblock(fin, g, p):
    ns, two, ngb, n_tiles, sub, lanes = fin.shape
    fin = fin.transpose(0, 1, 2, 4, 3, 5).reshape(ns, two, ngb, sub, n_tiles * lanes)
    sw = n_tiles * lanes // 2

    def unblk(s):
        return s.transpose(0, 3, 1, 2, 4).reshape(ns * sub, two, g, p)

    return unblk(fin[..., :sw]), unblk(fin[..., sw:])


def _glu_kernel(x_ref, yf_ref, yb_ref, g_ref, sc_ref, sh_ref, dskip_ref, w_ref, b_ref, gate_ref, o_ref):
    x = x_ref[...]
    xm = _norm_mod_tm(x, g_ref[...], sc_ref[0], sh_ref[0])
    gz = jax.nn.gelu(yf_ref[0] + yb_ref[0] + dskip_ref[...] * xm)
    acc = jnp.dot(gz.astype(BF16), w_ref[...], preferred_element_type=F32) + b_ref[...]
    o_ref[...] = x + _per_seq(gz * jax.nn.sigmoid(acc), gate_ref[0], jnp.multiply)


def _s5_glu(x, y, g, sc, sh, d_skip, w_glu, b_glu, gate, seq_len):
    m, d = x.shape
    group_rows = seq_len * SUBLANE
    tm = _tile(group_rows, 256, BF16_ROWS)
    mod_spec = pl.BlockSpec((1, SUBLANE, d), lambda i: (i * tm // group_rows, 0, 0))
    vec_spec = pl.BlockSpec((1, d), lambda i: (0, 0))
    return pl.pallas_call(
        _glu_kernel,
        grid=(m // tm,),
        in_specs=[pl.BlockSpec((tm, d), lambda i: (i, 0)),
                  pl.BlockSpec((1, tm, d), lambda i: (0, i, 0)),
                  pl.BlockSpec((1, tm, d), lambda i: (1, i, 0)),
                  vec_spec, mod_spec, mod_spec, vec_spec,
                  pl.BlockSpec((d, d), lambda i: (0, 0)),
                  vec_spec, mod_spec],
        out_specs=pl.BlockSpec((tm, d), lambda i: (i, 0)),
        out_shape=jax.ShapeDtypeStruct((m, d), F32),
        compiler_params=_params("parallel"),
        name="s5_glu",
    )(x, y, y, g, sc, sh, d_skip, w_glu, b_glu, gate)


def _final_norm_kernel(x_ref, g_ref, o_ref):
    nb, tl, d = o_ref.shape
    x = x_ref[...]
    y = x * _rstd(x) * g_ref[...]
    o_ref[...] = pltpu.einshape("tbd->btd", y.reshape(tl, nb, d))


def _final_norm(x, g, batch, seq_len):
    m, d = x.shape
    tl = _tile(seq_len, 64, SUBLANE)
    nl = seq_len // tl
    return pl.pallas_call(
        _final_norm_kernel,
        grid=(batch // SUBLANE, nl),
        in_specs=[pl.BlockSpec((tl * SUBLANE, d), lambda s, t: (s * nl + t, 0)),
                  pl.BlockSpec((1, d), lambda s, t: (0, 0))],
        out_specs=pl.BlockSpec((SUBLANE, tl, d), lambda s, t: (s, t, 0)),
        out_shape=jax.ShapeDtypeStruct((batch, seq_len, d), F32),
        compiler_params=_params("parallel", "parallel"),
        name="final_norm",
    )(x, g)


def kernel(x_prompt, x_sample, cache_k, cache_v, state_re, state_im, c, c_ctx, w_mod, b_mod, norm_g, w_qkv, lam_vecs, subln_g, w_o, ssm_a_re, ssm_a_im, ssm_log_step, ssm_b_re, ssm_b_im, ssm_c_re, ssm_c_im, ssm_d, w_glu, b_glu, w_up, conv_w, conv_b, w_down, final_g):
    bp, lp, d = x_prompt.shape
    bs, ls, _ = x_sample.shape
    depth = w_mod.shape[0]
    n_heads, head_dim = cache_k.shape[3], cache_k.shape[5]
    n_groups, state_dim, group = ssm_b_re.shape[2], ssm_b_re.shape[3], ssm_b_re.shape[4]
    assert bp % SUBLANE == 0 and bs % SUBLANE == 0 and d % MXU_DIM == 0 and MXU_DIM % group == 0
    assert 2 * head_dim == MXU_DIM and ls % GRID_W == 0

    pad_rows = -(bs + 1) % SUBLANE
    cond = jnp.concatenate([c, c_ctx[None], jnp.zeros((pad_rows, d), F32)], axis=0)
    mod = _modulation(cond, w_mod, b_mod).reshape(depth, cond.shape[0], 6, d)

    streams = [dict(x=x_prompt.reshape(bp * lp, d), batch=bp, seq=lp, time_major=False),
               dict(x=x_sample.reshape(bs * ls, d), batch=bs, seq=ls, time_major=False)]
    new_k, new_v, new_sre, new_sim = [], [], [], []
    q_scale = head_dim ** -0.5 * math.log2(math.e)
    rope = _rope_tables(ls, head_dim, q_scale)
    ngb = d // MXU_DIM

    for i in range(depth):
        mix, slot = i % N_MIXERS, i // N_MIXERS
        mods = [[jnp.broadcast_to(mod[i, bs, j], (bp, d)) for j in range(6)],
                [mod[i, :bs, j] for j in range(6)]]
        g1 = norm_g[i, 0][None]
        g2 = norm_g[i, 1][None]

        def per_group(v):
            return v.reshape(v.shape[0] // SUBLANE, SUBLANE, d)

        if mix == 0:
            lam_init = 0.8 - 0.6 * math.exp(-0.3 * i)
            nqk = n_heads * 2 * head_dim
            wq = _to_bf16(w_qkv, slot, swap_cols=2 * nqk, head_dim=head_dim)
            wo = _to_bf16(w_o, slot)
            sub_g = subln_g[slot][None]
            ck = cache_k.reshape(cache_k.shape[:3] + (-1,))
            cv = cache_v.reshape(cache_v.shape[:3] + (-1,))
            for s, (st, md) in enumerate(zip(streams, mods)):
                b, l = st['batch'], st['seq']
                if st['time_major']:
                    st['x'] = _to_batch_major(st['x'], b, l)
                sh1, sc1 = (v[:1, None] if s == 0 else v[:, None] for v in (md[0], md[1]))
                if s == 0:
                    q_p, k_p, v_p = _qkv_proj(st['x'], g1, sc1, sh1, wq, l, q_scale)
                    new_k.append(_swap_middle_quarters(k_p, head_dim).reshape(b, l, n_heads, 2, head_dim))
                    new_v.append(v_p.reshape(b, l, n_heads, 2 * head_dim))
                    o = _diff_attention((q_p, k_p, v_p), lam_vecs[slot], sub_g, lam_init, b, l, n_heads, head_dim)
                else:
                    qkv = _qkv_proj(st['x'], g1, sc1, sh1, wq, l, q_scale, rope_tables=rope)
                    o = _diff_attention(qkv, lam_vecs[slot], sub_g, lam_init, b, l, n_heads, head_dim,
                                        cache_k=ck, cache_v=cv, slot=slot)
                st['x'] = _attn_out_proj(o, wo, st['x'], md[2][:, None], b, l)
                st['time_major'] = True
        else:
            lb_re, lb_im, bb_re, bb_im = _s5_discretize(ssm_a_re[slot], ssm_a_im[slot], ssm_log_step[slot],
                                                        ssm_b_re[slot], ssm_b_im[slot])
            wb, wc, lam = _s5_block_weights(lb_re, lb_im, bb_re, bb_im, ssm_c_re[slot], ssm_c_im[slot],
                                            n_groups, state_dim, group)
            wg = _to_bf16(w_glu, slot)
            for s, (st, md) in enumerate(zip(streams, mods)):
                b, l = st['batch'], st['seq']
                sh1, sc1, gt1 = per_group(md[0]), per_group(md[1]), per_group(md[2])
                u = _s5_input(st['x'], g1, sc1, sh1, l)
                if s == 0:
                    h0 = jnp.zeros((b // SUBLANE, 2, ngb, wb.shape[2], SUBLANE, MXU_DIM), F32)
                else:
                    h0 = _s5_state_blocks(state_re[:, slot], state_im[:, slot], ngb)
                y, fin = _s5_scan(u, wb, wc, lam, h0, l)
                if s == 0:
                    f_re, f_im = _s5_state_unblock(fin, n_groups, state_dim)
                    new_sre.append(f_re)
                    new_sim.append(f_im)
                st['x'] = _s5_glu(st['x'], y, g1, sc1, sh1, ssm_d[slot][None], wg, b_glu[slot][None], gt1, l)
        wu = _to_bf16(w_up, i)
        wd = _to_bf16(w_down, i)
        for st, md in zip(streams, mods):
            sh2, sc2, gt2 = per_group(md[3]), per_group(md[4]), per_group(md[5])
            act = _conv_ffn_up(st['x'], g2, sc2, sh2, wu, conv_w[i], conv_b[i][None], st['seq'])
            st['x'] = _matmul_gate_residual(act, wd, st['x'], gt2, st['seq'])

    y_prompt = _final_norm(streams[0]['x'], final_g[None], bp, lp)
    y_sample = _final_norm(streams[1]['x'], final_g[None], bs, ls)
    return (y_prompt, y_sample, jnp.stack(new_k, axis=1), jnp.stack(new_v, axis=1),
            jnp.stack(new_sre, axis=1), jnp.stack(new_sim, axis=1))
```

```python
import functools
import math

import jax
import jax.numpy as jnp
from jax import lax
from jax.experimental import pallas as pl
from jax.experimental.pallas import tpu as pltpu

F32 = jnp.float32
BF16 = jnp.bfloat16

GRID_W = 64
ROPE_THETA = 10000.0
NORM_EPS = 1e-6
SUBLN_EPS = 1e-5
N_MIXERS = 2

LANE = 128
SUBLANE = 8
BF16_ROWS = 16
MXU_DIM = 256
VMEM_LIMIT_BYTES = 52 * 1024 * 1024


def _tile(n, pref, align):
    t = min(pref, n)
    t -= t % align
    while t >= align:
        if n % t == 0:
            return t
        t -= align
    return n


def _row_tile(m, seq_len, n_mod, pref, align):
    return _tile(m, pref, align) if n_mod == 1 else _tile(seq_len, pref, align)


def _params(*sem):
    return pltpu.CompilerParams(dimension_semantics=sem, vmem_limit_bytes=VMEM_LIMIT_BYTES)


def _rstd(x):
    return lax.rsqrt(jnp.mean(x * x, axis=-1, keepdims=True) + NORM_EPS)


def _norm_mod(x, g, sc, sh):
    return x * _rstd(x) * (g * (1.0 + sc)) + sh


def _per_seq(x, pattern, op):
    rows, n = x.shape
    return op(x.reshape(rows // SUBLANE, SUBLANE, n), pattern[None]).reshape(rows, n)


def _norm_mod_tm(x, g, sc, sh):
    scaled = _per_seq(x * _rstd(x), g * (1.0 + sc), jnp.multiply)
    return _per_seq(scaled, sh, jnp.add)


def _mod_kernel(c_ref, w_ref, b_ref, o_ref):
    c = c_ref[...]
    s = (c * jax.nn.sigmoid(c)).astype(BF16)
    o_ref[0] = jnp.dot(s, w_ref[0].astype(BF16), preferred_element_type=F32) + b_ref[0]


def _modulation(cond, w_mod, b_mod):
    depth, d, n = w_mod.shape
    rows = cond.shape[0]
    tn = _tile(n, 1024, LANE)
    return pl.pallas_call(
        _mod_kernel,
        grid=(depth, n // tn),
        in_specs=[pl.BlockSpec((rows, d), lambda l, j: (0, 0)),
                  pl.BlockSpec((1, d, tn), lambda l, j: (l, 0, j)),
                  pl.BlockSpec((1, 1, tn), lambda l, j: (l, 0, j))],
        out_specs=pl.BlockSpec((1, rows, tn), lambda l, j: (l, 0, j)),
        out_shape=jax.ShapeDtypeStruct((depth, rows, n), F32),
        compiler_params=_params("parallel", "parallel"),
        name="modulation",
    )(cond, w_mod, b_mod.reshape(depth, 1, n))


def _swap_quarters_in_lanes(x, head_dim):
    q4 = head_dim // 4
    parts = []
    for c in range(x.shape[1] // head_dim):
        base = c * head_dim
        parts += [x[:, base:base + q4], x[:, base + 2 * q4:base + 3 * q4],
                  x[:, base + q4:base + 2 * q4], x[:, base + 3 * q4:base + 4 * q4]]
    return jnp.concatenate(parts, axis=1)


def _cast_kernel(w_ref, o_ref, *, swap_tiles, head_dim):
    w = w_ref[0]
    if swap_tiles == 0:
        o_ref[...] = w.astype(BF16)
        return
    j = pl.program_id(1)

    @pl.when(j < swap_tiles)
    def _():
        o_ref[...] = _swap_quarters_in_lanes(w, head_dim).astype(BF16)

    @pl.when(j >= swap_tiles)
    def _():
        o_ref[...] = w.astype(BF16)


def _to_bf16(w_stack, index, swap_cols=0, head_dim=LANE):
    _, k, n = w_stack.shape
    tk = _tile(k, 1024, BF16_ROWS)
    tn = _tile(math.gcd(swap_cols, n), 2048, LANE)
    return pl.pallas_call(
        functools.partial(_cast_kernel, swap_tiles=swap_cols // tn, head_dim=head_dim),
        grid=(k // tk, n // tn),
        in_specs=[pl.BlockSpec((1, tk, tn), lambda i, j: (index, i, j))],
        out_specs=pl.BlockSpec((tk, tn), lambda i, j: (i, j)),
        out_shape=jax.ShapeDtypeStruct((k, n), BF16),
        compiler_params=_params("parallel", "parallel"),
        name="weight_to_bf16",
    )(w_stack)


def _qkv_kernel(*refs, rope, n_rope_tiles, tn, q_scale):
    if rope:
        x_ref, g_ref, sc_ref, sh_ref, w_ref, cos_ref, sin_ref, o_ref, xm_ref = refs
    else:
        x_ref, g_ref, sc_ref, sh_ref, w_ref, oq_ref, ok_ref, ov_ref, xm_ref = refs
    j = pl.program_id(1)

    @pl.when(j == 0)
    def _():
        xm_ref[...] = _norm_mod(x_ref[...], g_ref[...], sc_ref[0], sh_ref[0]).astype(BF16)

    acc = jnp.dot(xm_ref[...], w_ref[...], preferred_element_type=F32)
    if not rope:
        n_q = n_rope_tiles // 2

        @pl.when(j < n_q)
        def _():
            oq_ref[...] = (acc * q_scale).astype(oq_ref.dtype)

        @pl.when((j >= n_q) & (j < 2 * n_q))
        def _():
            ok_ref[...] = acc

        @pl.when(j >= 2 * n_q)
        def _():
            ov_ref[...] = acc

        return

    @pl.when(j < n_rope_tiles)
    def _():
        cos = cos_ref[0]
        sin = sin_ref[0]
        for c in range(tn // LANE):
            x = acc[:, c * LANE:(c + 1) * LANE]
            o_ref[:, c * LANE:(c + 1) * LANE] = (x * cos + pltpu.roll(x, LANE // 2, 1) * sin).astype(o_ref.dtype)

    @pl.when(j >= n_rope_tiles)
    def _():
        o_ref[...] = acc.astype(o_ref.dtype)


def _qkv_proj(x, g, sc, sh, w, seq_len, q_scale, rope_tables=None):
    m, d = x.shape
    n = w.shape[1]
    n_rope_cols = 2 * n // 3
    n_mod = sc.shape[0]
    tm = _tile(seq_len, 1024, BF16_ROWS) if rope_tables is not None else _row_tile(m, seq_len, n_mod, 1024, BF16_ROWS)
    tn = _tile(n_rope_cols // 2, 1024 if rope_tables is not None else 512, LANE)
    nq = n_rope_cols // (2 * tn)

    def mod_i(i):
        return (i * tm // seq_len) % n_mod

    in_specs = [pl.BlockSpec((tm, d), lambda i, j: (i, 0)),
                pl.BlockSpec((1, d), lambda i, j: (0, 0)),
                pl.BlockSpec((1, 1, d), lambda i, j: (mod_i(i), 0, 0)),
                pl.BlockSpec((1, 1, d), lambda i, j: (mod_i(i), 0, 0)),
                pl.BlockSpec((d, tn), lambda i, j: (0, j))]
    args = [x, g, sc, sh, w]
    if rope_tables is not None:
        tiles_per_seq = seq_len // tm
        in_specs += [pl.BlockSpec((1, tm, LANE),
                                  lambda i, j: (jnp.where(j < nq, 0, 1), i % tiles_per_seq, 0))] * 2
        args += list(rope_tables)
        out_specs = pl.BlockSpec((tm, tn), lambda i, j: (i, j))
        out_shape = jax.ShapeDtypeStruct((m, n), BF16)
    else:
        out_specs = [pl.BlockSpec((tm, tn), lambda i, j, part=part: (i, jnp.clip(j - part * nq, 0, nq - 1)))
                     for part in range(3)]
        out_shape = [jax.ShapeDtypeStruct((m, n // 3), dt) for dt in (BF16, F32, F32)]
    return pl.pallas_call(
        functools.partial(_qkv_kernel, rope=rope_tables is not None,
                          n_rope_tiles=n_rope_cols // tn, tn=tn, q_scale=q_scale),
        grid=(m // tm, n // tn),
        in_specs=in_specs,
        out_specs=out_specs,
        out_shape=out_shape,
        scratch_shapes=[pltpu.VMEM((tm, d), BF16)],
        compiler_params=_params("parallel", "arbitrary"),
        name="qkv_proj",
    )(*args)


def _swap_middle_quarters(a, head_dim):
    lead = a.shape[:-1]
    a = a.reshape(lead + (a.shape[-1] // head_dim, 2, 2, head_dim // 4))
    return jnp.swapaxes(a, -3, -2).reshape(lead + (-1,))


def _rope_tables(length, head_dim, q_scale):
    pos = jnp.arange(length)
    r = (pos // GRID_W).astype(F32)
    col = (pos % GRID_W).astype(F32)
    half = head_dim // 2
    inv = ROPE_THETA ** (-jnp.arange(0, half, 2, dtype=F32) / half)
    ar = r[:, None] * inv
    ac = col[:, None] * inv
    cos = jnp.concatenate([jnp.cos(ar), jnp.cos(ac)] * 2, axis=-1)
    sin = jnp.concatenate([-jnp.sin(ar), -jnp.sin(ac), jnp.sin(ar), jnp.sin(ac)], axis=-1)
    scales = jnp.array([q_scale, 1.0], F32)[:, None, None]
    return cos[None] * scales, sin[None] * scales


def _attn_kernel(*refs, has_cache, lam_init, head_dim, kv_chunk, heads):
    if has_cache:
        lam_ref, q_ref, k_ref, v_ref, ck_ref, cv_ref, g_ref, o_ref = refs
    else:
        lam_ref, q_ref, k_ref, v_ref, g_ref, o_ref = refs
    hw = 2 * head_dim
    lv = lam_ref[...]
    lam = (jnp.exp(jnp.sum(lv[0:1] * lv[1:2], axis=-1, keepdims=True))
           - jnp.exp(jnp.sum(lv[2:3] * lv[3:4], axis=-1, keepdims=True)) + lam_init)
    for hh in range(heads):
        cols = slice(hh * hw, (hh + 1) * hw)
        q = q_ref[:, cols]
        ks = [k_ref[:, cols].astype(BF16)]
        vs = [v_ref[:, cols].astype(BF16)]
        if has_cache:
            ks.append(_swap_quarters_in_lanes(ck_ref[0, 0, :, cols], head_dim).astype(BF16))
            vs.append(cv_ref[0, 0, :, cols].astype(BF16))
        kv = [(k[r:r + kv_chunk], v[r:r + kv_chunk]) for k, v in zip(ks, vs) for r in range(0, k.shape[0], kv_chunk)]
        outs = []
        for c in range(2):
            qc = q[:, c * head_dim:(c + 1) * head_dim]
            m = denom = acc = None
            for k, v in kv:
                s = lax.dot_general(qc, k[:, c * head_dim:(c + 1) * head_dim],
                                    (((1,), (1,)), ((), ())), preferred_element_type=F32)
                s_max = jnp.max(s, axis=-1, keepdims=True)
                m_new = s_max if m is None else jnp.maximum(m, s_max)
                p = jnp.exp2(s - m_new)
                p_sum = jnp.sum(p, axis=-1, keepdims=True)
                pv = jnp.dot(p.astype(BF16), v, preferred_element_type=F32)
                if m is None:
                    denom, acc = p_sum, pv
                else:
                    alpha = jnp.exp2(m - m_new)
                    denom, acc = denom * alpha + p_sum, acc * alpha + pv
                m = m_new
            outs.append(acc / denom)
        o = outs[0] - lam * outs[1]
        r = lax.rsqrt(jnp.mean(o * o, axis=-1, keepdims=True) + SUBLN_EPS)
        o_ref[:, cols] = (o * r * g_ref[...] * (1.0 - lam_init)).astype(o_ref.dtype)


def _diff_attention(qkv, lam_vecs, subln_g, lam_init, batch, seq_len, n_heads, head_dim,
                    cache_k=None, cache_v=None, slot=0):
    hw = 2 * head_dim
    tq = _tile(seq_len, 1024, BF16_ROWS)
    nq = seq_len // tq
    heads = _tile(n_heads, max(2048 // seq_len, 1), 1)
    bw = heads * hw
    ng = n_heads // heads
    has_cache = cache_k is not None
    if isinstance(qkv, tuple):
        q_arr, k_arr, v_arr = qkv
        k_off = v_off = 0
    else:
        q_arr = k_arr = v_arr = qkv
        k_off, v_off = ng, 2 * ng
    in_specs = [pl.BlockSpec(lam_vecs.shape, lambda b, h, i: (0, 0)),
                pl.BlockSpec((tq, bw), lambda b, h, i: (b * nq + i, h)),
                pl.BlockSpec((seq_len, bw), lambda b, h, i: (b, k_off + h)),
                pl.BlockSpec((seq_len, bw), lambda b, h, i: (b, v_off + h))]
    args = [lam_vecs, q_arr, k_arr, v_arr]
    if has_cache:
        past = cache_k.shape[2]
        in_specs += [pl.BlockSpec((1, 1, past, bw), lambda b, h, i: (b, slot, 0, h))] * 2
        args += [cache_k, cache_v]
    in_specs.append(pl.BlockSpec((1, hw), lambda b, h, i: (0, 0)))
    args.append(subln_g)
    return pl.pallas_call(
        functools.partial(_attn_kernel, has_cache=has_cache, lam_init=lam_init,
                          head_dim=head_dim, kv_chunk=_tile(seq_len, 512, BF16_ROWS), heads=heads),
        grid=(batch, ng, nq),
        in_specs=in_specs,
        out_specs=pl.BlockSpec((tq, bw), lambda b, h, i: (b * nq + i, h)),
        out_shape=jax.ShapeDtypeStruct((batch * seq_len, n_heads * hw), BF16),
        compiler_params=_params("parallel", "parallel", "arbitrary"),
        name="diff_attention",
    )(*args)


def _attn_out_kernel(a_ref, w_ref, x_ref, gate_ref, o_ref):
    nb, tl, k = a_ref.shape
    acc = jnp.dot(a_ref[...].reshape(nb * tl, k), w_ref[...], preferred_element_type=F32)
    res = x_ref[...] + gate_ref[...] * acc.reshape(nb, tl, -1)
    o_ref[...] = pltpu.einshape("btn->tbn", res).reshape(tl * nb, -1)


def _attn_out_proj(o, w, x, gate, batch, seq_len):
    k = o.shape[1]
    n = w.shape[1]
    tl = _tile(seq_len, 128, BF16_ROWS)
    tn = _tile(n, 1024, LANE)
    nl = seq_len // tl
    return pl.pallas_call(
        _attn_out_kernel,
        grid=(batch // SUBLANE, nl, n // tn),
        in_specs=[pl.BlockSpec((SUBLANE, tl, k), lambda s, t, j: (s, t, 0)),
                  pl.BlockSpec((k, tn), lambda s, t, j: (0, j)),
                  pl.BlockSpec((SUBLANE, tl, tn), lambda s, t, j: (s, t, j)),
                  pl.BlockSpec((SUBLANE, 1, tn), lambda s, t, j: (s, 0, j))],
        out_specs=pl.BlockSpec((tl * SUBLANE, tn), lambda s, t, j: (s * nl + t, j)),
        out_shape=jax.ShapeDtypeStruct((batch * seq_len, n), F32),
        compiler_params=_params("parallel", "parallel", "arbitrary"),
        name="attn_out_proj",
    )(o.reshape(batch, seq_len, k), w, x.reshape(batch, seq_len, n), gate)


def _to_batch_major_kernel(x_ref, o_ref):
    nb, tl, d = o_ref.shape
    o_ref[...] = pltpu.einshape("tbd->btd", x_ref[...].reshape(tl, nb, d))


def _to_batch_major(x, batch, seq_len):
    m, d = x.shape
    tl = _tile(seq_len, 64, SUBLANE)
    nl = seq_len // tl
    return pl.pallas_call(
        _to_batch_major_kernel,
        grid=(batch // SUBLANE, nl),
        in_specs=[pl.BlockSpec((tl * SUBLANE, d), lambda s, t: (s * nl + t, 0))],
        out_specs=pl.BlockSpec((SUBLANE, tl, d), lambda s, t: (s, t, 0)),
        out_shape=jax.ShapeDtypeStruct((batch, seq_len, d), F32),
        compiler_params=_params("parallel", "parallel"),
        name="to_batch_major",
    )(x).reshape(m, d)


def _mm_res_kernel(a_ref, w_ref, x_ref, gate_ref, o_ref):
    acc = jnp.dot(a_ref[...], w_ref[...], preferred_element_type=F32)
    o_ref[...] = x_ref[...] + _per_seq(acc, gate_ref[0], jnp.multiply)


def _matmul_gate_residual(a, w, x, gate, seq_len):
    m, k = a.shape
    n = w.shape[1]
    group_rows = seq_len * SUBLANE
    tm = _tile(group_rows, 1024, BF16_ROWS)
    tn = _tile(n, 512, LANE)
    return pl.pallas_call(
        _mm_res_kernel,
        grid=(m // tm, n // tn),
        in_specs=[pl.BlockSpec((tm, k), lambda i, j: (i, 0)),
                  pl.BlockSpec((k, tn), lambda i, j: (0, j)),
                  pl.BlockSpec((tm, tn), lambda i, j: (i, j)),
                  pl.BlockSpec((1, SUBLANE, tn), lambda i, j: (i * tm // group_rows, 0, j))],
        out_specs=pl.BlockSpec((tm, tn), lambda i, j: (i, j)),
        out_shape=jax.ShapeDtypeStruct((m, n), F32),
        compiler_params=_params("parallel", "arbitrary"),
        name="matmul_gate_residual",
    )(a, w, x, gate)


def _up_kernel(x_ref, xp_ref, xn_ref, g_ref, sc_ref, sh_ref, wg_ref, wv_ref,
               cwg_ref, cwv_ref, cbg_ref, cbv_ref, o_ref, xm_ref, *, tm, group_rows, chunk):
    halo = BF16_ROWS
    i = pl.program_id(0)
    j = pl.program_id(1)

    @pl.when(j == 0)
    def _():
        g, sc, sh = g_ref[...], sc_ref[0], sh_ref[0]
        xm_ref[0:halo] = _norm_mod_tm(xp_ref[...], g, sc, sh).astype(BF16)
        xm_ref[halo:halo + tm] = _norm_mod_tm(x_ref[...], g, sc, sh).astype(BF16)
        xm_ref[halo + tm:] = _norm_mod_tm(xn_ref[...], g, sc, sh).astype(BF16)

        @pl.when((i * tm) % group_rows == 0)
        def _():
            xm_ref[0:halo] = jnp.zeros((halo, xm_ref.shape[1]), BF16)

        @pl.when(((i + 1) * tm) % group_rows == 0)
        def _():
            xm_ref[halo + tm:] = jnp.zeros((halo, xm_ref.shape[1]), BF16)

    xm = xm_ref[...]

    def conv(h, cw_ref, cb_ref, cols):
        cw = cw_ref[:, cols]
        lo = halo - SUBLANE
        return (h[lo:lo + tm] * cw[0:1] + h[halo:halo + tm] * cw[1:2]
                + h[halo + SUBLANE:halo + SUBLANE + tm] * cw[2:3] + cb_ref[:, cols])

    for c in range(o_ref.shape[1] // chunk):
        cols = slice(c * chunk, (c + 1) * chunk)
        gate = conv(jnp.dot(xm, wg_ref[:, cols], preferred_element_type=F32), cwg_ref, cbg_ref, cols)
        val = conv(jnp.dot(xm, wv_ref[:, cols], preferred_element_type=F32), cwv_ref, cbv_ref, cols)
        o_ref[:, cols] = (gate * jax.nn.sigmoid(gate) * val).astype(o_ref.dtype)


def _conv_ffn_up(x, g, sc, sh, w_up, conv_w, conv_b, seq_len):
    m, d = x.shape
    d_ff = w_up.shape[1] // 2
    halo = BF16_ROWS
    group_rows = seq_len * SUBLANE
    tm = _tile(group_rows, 1024, halo)
    tn = _tile(d_ff, 512, LANE)
    chunk = _tile(tn, MXU_DIM, LANE)
    nj = d_ff // tn
    hb = tm // halo
    last_hb = m // halo - 1

    mod_spec = pl.BlockSpec((1, SUBLANE, d), lambda i, j: (i * tm // group_rows, 0, 0))
    return pl.pallas_call(
        functools.partial(_up_kernel, tm=tm, group_rows=group_rows, chunk=chunk),
        grid=(m // tm, nj),
        in_specs=[pl.BlockSpec((tm, d), lambda i, j: (i, 0)),
                  pl.BlockSpec((halo, d), lambda i, j: (jnp.maximum(i * hb - 1, 0), 0)),
                  pl.BlockSpec((halo, d), lambda i, j: (jnp.minimum((i + 1) * hb, last_hb), 0)),
                  pl.BlockSpec((1, d), lambda i, j: (0, 0)),
                  mod_spec, mod_spec,
                  pl.BlockSpec((d, tn), lambda i, j: (0, j)),
                  pl.BlockSpec((d, tn), lambda i, j: (0, nj + j)),
                  pl.BlockSpec((3, tn), lambda i, j: (0, j)),
                  pl.BlockSpec((3, tn), lambda i, j: (0, nj + j)),
                  pl.BlockSpec((1, tn), lambda i, j: (0, j)),
                  pl.BlockSpec((1, tn), lambda i, j: (0, nj + j))],
        out_specs=pl.BlockSpec((tm, tn), lambda i, j: (i, j)),
        out_shape=jax.ShapeDtypeStruct((m, d_ff), BF16),
        scratch_shapes=[pltpu.VMEM((tm + 2 * halo, d), BF16)],
        compiler_params=_params("parallel", "arbitrary"),
        name="conv_ffn_up",
    )(x, x, x, g, sc, sh, w_up, w_up, conv_w, conv_w, conv_b, conv_b)


def _discretize(a_re, a_im, log_step):
    dt = jnp.exp(log_step)
    mag = jnp.exp(a_re * dt)
    lb_re = mag * jnp.cos(a_im * dt)
    lb_im = mag * jnp.sin(a_im * dt)
    num_re = lb_re - 1.0
    den = a_re * a_re + a_im * a_im
    f_re = (num_re * a_re + lb_im * a_im) / den
    f_im = (lb_im * a_re - num_re * a_im) / den
    return lb_re, lb_im, f_re, f_im


def _s5_disc_kernel(a_re_ref, a_im_ref, ls_ref, ax_re_ref, ax_im_ref, lsx_ref, b_re_ref, b_im_ref,
                    lb_re_ref, lb_im_ref, bb_re_ref, bb_im_ref):
    lb_re, lb_im, _, _ = _discretize(a_re_ref[...], a_im_ref[...], ls_ref[...])
    lb_re_ref[...] = lb_re
    lb_im_ref[...] = lb_im
    _, _, f_re, f_im = _discretize(ax_re_ref[...], ax_im_ref[...], lsx_ref[...])
    b_re, b_im = b_re_ref[...], b_im_ref[...]
    bb_re_ref[...] = f_re * b_re - f_im * b_im
    bb_im_ref[...] = f_re * b_im + f_im * b_re


def _s5_discretize(a_re, a_im, log_step, b_re, b_im):
    two, g, p, c = b_re.shape
    rows = two * g
    a2 = lambda a: a.reshape(rows, p)
    ax = lambda a: jnp.repeat(a.reshape(rows, p), c, axis=-1)
    ls = jnp.broadcast_to(log_step.reshape(rows, 1), (rows, p))
    lsx = jnp.broadcast_to(log_step.reshape(rows, 1), (rows, p * c))
    return pl.pallas_call(
        _s5_disc_kernel,
        out_shape=[jax.ShapeDtypeStruct((rows, p), F32)] * 2 + [jax.ShapeDtypeStruct((rows, p * c), F32)] * 2,
        compiler_params=pltpu.CompilerParams(vmem_limit_bytes=VMEM_LIMIT_BYTES),
        name="s5_discretize",
    )(a2(a_re), a2(a_im), ls, ax(a_re), ax(a_im), lsx, b_re.reshape(rows, p * c), b_im.reshape(rows, p * c))


def _s5_in_kernel(x_ref, g_ref, sc_ref, sh_ref, o_ref):
    o_ref[...] = _norm_mod_tm(x_ref[...], g_ref[...], sc_ref[0], sh_ref[0]).astype(o_ref.dtype)


def _s5_input(x, g, sc, sh, seq_len):
    m, d = x.shape
    group_rows = seq_len * SUBLANE
    tm = _tile(group_rows, 1024, BF16_ROWS)
    mod_spec = pl.BlockSpec((1, SUBLANE, d), lambda i: (i * tm // group_rows, 0, 0))
    return pl.pallas_call(
        _s5_in_kernel,
        grid=(m // tm,),
        in_specs=[pl.BlockSpec((tm, d), lambda i: (i, 0)), pl.BlockSpec((1, d), lambda i: (0, 0)),
                  mod_spec, mod_spec],
        out_specs=pl.BlockSpec((tm, d), lambda i: (i, 0)),
        out_shape=jax.ShapeDtypeStruct((m, d), BF16),
        compiler_params=_params("parallel"),
        name="s5_input",
    )(x, g, sc, sh)


def _s5_scan_kernel(u_ref, wb_ref, wc_ref, lam_ref, h0_ref, y_ref, fin_ref, st_ref, buf_ref,
                    *, nsub, nt, sub_t):
    d = pl.program_id(1)
    t_idx = pl.program_id(3)
    n_tiles = st_ref.shape[0]
    half = n_tiles // 2
    sr = sub_t * SUBLANE
    steps = sub_t // n_tiles

    @pl.when(t_idx == 0)
    def _():
        st_ref[...] = h0_ref[0, 0, 0]

    lam = jnp.broadcast_to(lam_ref[0, 0], st_ref.shape)
    a_re, a_im = lam[:half], lam[half:]

    def first_row(p):
        return pl.multiple_of((p + d * (nsub - 1 - 2 * p)) * sr, sr)

    def run_pass(p, state, do_bu, do_scan, do_ch):
        slot_bu, slot_scan, slot_ch = p % 3, (p + 2) % 3, (p + 1) % 3
        if do_ch:
            h_all = jnp.concatenate([buf_ref[slot_ch, i] for i in range(n_tiles)], axis=1)
            y_ref[0, pl.ds(first_row(p - 2), sr), :] = jnp.dot(h_all.astype(BF16), wc_ref[0, 0],
                                                               preferred_element_type=F32)
        if do_bu:
            u_sub = u_ref[pl.ds(first_row(p), sr), :]
        h_re, h_im = state
        for i in range(n_tiles):
            if do_bu:
                buf_ref[slot_bu, i] = jnp.dot(u_sub, wb_ref[0, 0, i], preferred_element_type=F32)
            if do_scan:
                for k in range(steps):
                    t = i * steps + k
                    row = pl.multiple_of((t + d * (sub_t - 1 - 2 * t)) * SUBLANE, SUBLANE)
                    bu = buf_ref[slot_scan, :, pl.ds(row, SUBLANE), :]
                    n_re = a_re * h_re - a_im * h_im + bu[:half]
                    n_im = a_re * h_im + a_im * h_re + bu[half:]
                    buf_ref[slot_scan, 0:half, pl.ds(row, SUBLANE), :] = n_re
                    buf_ref[slot_scan, half:, pl.ds(row, SUBLANE), :] = n_im
                    h_re, h_im = n_re, n_im
        return h_re, h_im

    state = (st_ref[0:half], st_ref[half:])
    steady = range(2, nsub)
    for p in range(min(2, nsub + 2)):
        state = run_pass(p, state, p < nsub, 1 <= p <= nsub, False)
    if len(steady) > 0:
        state = lax.fori_loop(steady.start, steady.stop,
                              lambda p, st: run_pass(p, st, True, True, True), state)
    for p in range(max(2, nsub), nsub + 2):
        state = run_pass(p, state, p < nsub, p <= nsub, True)
    st_ref[0:half] = state[0]
    st_ref[half:] = state[1]

    @pl.when(t_idx == nt - 1)
    def _():
        fin_ref[0, 0, 0] = st_ref[...]


def _s5_scan(u, wb, wc, lam, h0, seq_len):
    rows, d = u.shape
    ns = rows // (seq_len * SUBLANE)
    ngb = d // MXU_DIM
    n_tiles = wb.shape[2]
    sub_t = _tile(seq_len, 64, n_tiles)
    tc = _tile(seq_len, 16 * sub_t, sub_t)
    nt = seq_len // tc
    assert sub_t % n_tiles == 0 and tc % sub_t == 0

    def t_blk(n, dd, t):
        return n * nt + t + dd * (nt - 1 - 2 * t)

    w_spec = pl.BlockSpec((1, 1, n_tiles, MXU_DIM, MXU_DIM), lambda n, dd, gb, t: (dd, gb, 0, 0, 0))
    st_spec = pl.BlockSpec((1, 1, 1, n_tiles, SUBLANE, MXU_DIM), lambda n, dd, gb, t: (n, dd, gb, 0, 0, 0))
    return pl.pallas_call(
        functools.partial(_s5_scan_kernel, nsub=tc // sub_t, nt=nt, sub_t=sub_t),
        grid=(ns, 2, ngb, nt),
        in_specs=[pl.BlockSpec((tc * SUBLANE, MXU_DIM), lambda n, dd, gb, t: (t_blk(n, dd, t), gb)),
                  w_spec,
                  pl.BlockSpec((1, 1, n_tiles * MXU_DIM, MXU_DIM), lambda n, dd, gb, t: (dd, gb, 0, 0)),
                  pl.BlockSpec((1, 1, n_tiles, 1, MXU_DIM), lambda n, dd, gb, t: (dd, gb, 0, 0, 0)),
                  st_spec],
        out_specs=[pl.BlockSpec((1, tc * SUBLANE, MXU_DIM), lambda n, dd, gb, t: (dd, t_blk(n, dd, t), gb)),
                   st_spec],
        out_shape=[jax.ShapeDtypeStruct((2, rows, d), F32),
                   jax.ShapeDtypeStruct((ns, 2, ngb, n_tiles, SUBLANE, MXU_DIM), F32)],
        scratch_shapes=[pltpu.VMEM((n_tiles, SUBLANE, MXU_DIM), F32),
                        pltpu.VMEM((3, n_tiles, sub_t * SUBLANE, MXU_DIM), F32)],
        compiler_params=_params("parallel", "parallel", "parallel", "arbitrary"),
        name="s5_scan",
    )(u, wb, wc, lam, h0)


def _s5_block_weights(lb_re, lb_im, bb_re, bb_im, c_re, c_im, n_groups, state_dim, group):
    gpb = MXU_DIM // group
    ngb = n_groups // gpb
    eye = jnp.eye(gpb, dtype=F32)

    def b_blocks(bb):
        bb = bb.reshape(2, ngb, gpb, state_dim, group)
        return jnp.einsum('dbgpc,gh->dbgchp', bb, eye).reshape(2, ngb, gpb * group, gpb * state_dim)

    def c_blocks(cc):
        cc = cc.reshape(2, ngb, gpb, group, state_dim)
        return jnp.einsum('dbgcp,gh->dbgphc', cc, eye).reshape(2, ngb, gpb * state_dim, gpb * group)

    wb = jnp.concatenate([b_blocks(bb_re), b_blocks(bb_im)], axis=-1).astype(BF16)
    wc = jnp.concatenate([c_blocks(c_re), -c_blocks(c_im)], axis=-2).astype(BF16)
    lam = jnp.concatenate([lb_re.reshape(2, ngb, 1, gpb * state_dim),
                           lb_im.reshape(2, ngb, 1, gpb * state_dim)], axis=-1)
    n_tiles = wb.shape[-1] // MXU_DIM
    wb = wb.reshape(2, ngb, MXU_DIM, n_tiles, MXU_DIM).transpose(0, 1, 3, 2, 4)
    lam = lam.reshape(2, ngb, n_tiles, 1, MXU_DIM)
    return wb, wc, lam


def _s5_state_blocks(s_re, s_im, ngb):
    b, two, g, p = s_re.shape
    sw = g * p // ngb

    def blk(s):
        return s.reshape(b // SUBLANE, SUBLANE, two, ngb, sw).transpose(0, 2, 3, 1, 4)

    st = jnp.concatenate([blk(s_re), blk(s_im)], axis=-1)
    st = st.reshape(st.shape[:4] + (2 * sw // MXU_DIM, MXU_DIM))
    return st.transpose(0, 1, 2, 4, 3, 5)


def _s5_state_unblock(fin, g, p):
    ns, two, ngb, n_tiles, sub, lanes = fin.shape
    fin = fin.transpose(0, 1, 2, 4, 3, 5).reshape(ns, two, ngb, sub, n_tiles * lanes)
    sw = n_tiles * lanes // 2

    def unblk(s):
        return s.transpose(0, 3, 1, 2, 4).reshape(ns * sub, two, g, p)

    return unblk(fin[..., :sw]), unblk(fin[..., sw:])


def _glu_kernel(x_ref, yf_ref, yb_ref, g_ref, sc_ref, sh_ref, dskip_ref, w_ref, b_ref, gate_ref, o_ref):
    x = x_ref[...]
    xm = _norm_mod_tm(x, g_ref[...], sc_ref[0], sh_ref[0])
    gz = jax.nn.gelu(yf_ref[0] + yb_ref[0] + dskip_ref[...] * xm)
    acc = jnp.dot(gz.astype(BF16), w_ref[...], preferred_element_type=F32) + b_ref[...]
    o_ref[...] = x + _per_seq(gz * jax.nn.sigmoid(acc), gate_ref[0], jnp.multiply)


def _s5_glu(x, y, g, sc, sh, d_skip, w_glu, b_glu, gate, seq_len):
    m, d = x.shape
    group_rows = seq_len * SUBLANE
    tm = _tile(group_rows, 256, BF16_ROWS)
    mod_spec = pl.BlockSpec((1, SUBLANE, d), lambda i: (i * tm // group_rows, 0, 0))
    vec_spec = pl.BlockSpec((1, d), lambda i: (0, 0))
    return pl.pallas_call(
        _glu_kernel,
        grid=(m // tm,),
        in_specs=[pl.BlockSpec((tm, d), lambda i: (i, 0)),
                  pl.BlockSpec((1, tm, d), lambda i: (0, i, 0)),
                  pl.BlockSpec((1, tm, d), lambda i: (1, i, 0)),
                  vec_spec, mod_spec, mod_spec, vec_spec,
                  pl.BlockSpec((d, d), lambda i: (0, 0)),
                  vec_spec, mod_spec],
        out_specs=pl.BlockSpec((tm, d), lambda i: (i, 0)),
        out_shape=jax.ShapeDtypeStruct((m, d), F32),
        compiler_params=_params("parallel"),
        name="s5_glu",
    )(x, y, y, g, sc, sh, d_skip, w_glu, b_glu, gate)


def _final_norm_kernel(x_ref, g_ref, o_ref):
    nb, tl, d = o_ref.shape
    x = x_ref[...]
    y = x * _rstd(x) * g_ref[...]
    o_ref[...] = pltpu.einshape("tbd->btd", y.reshape(tl, nb, d))


def _final_norm(x, g, batch, seq_len):
    m, d = x.shape
    tl = _tile(seq_len, 64, SUBLANE)
    nl = seq_len // tl
    return pl.pallas_call(
        _final_norm_kernel,
        grid=(batch // SUBLANE, nl),
        in_specs=[pl.BlockSpec((tl * SUBLANE, d), lambda s, t: (s * nl + t, 0)),
                  pl.BlockSpec((1, d), lambda s, t: (0, 0))],
        out_specs=pl.BlockSpec((SUBLANE, tl, d), lambda s, t: (s, t, 0)),
        out_shape=jax.ShapeDtypeStruct((batch, seq_len, d), F32),
        compiler_params=_params("parallel", "parallel"),
        name="final_norm",
    )(x, g)


def kernel(x_prompt, x_sample, cache_k, cache_v, state_re, state_im, c, c_ctx, w_mod, b_mod, norm_g, w_qkv, lam_vecs, subln_g, w_o, ssm_a_re, ssm_a_im, ssm_log_step, ssm_b_re, ssm_b_im, ssm_c_re, ssm_c_im, ssm_d, w_glu, b_glu, w_up, conv_w, conv_b, w_down, final_g):
    bp, lp, d = x_prompt.shape
    bs, ls, _ = x_sample.shape
    depth = w_mod.shape[0]
    n_heads, head_dim = cache_k.shape[3], cache_k.shape[5]
    n_groups, state_dim, group = ssm_b_re.shape[2], ssm_b_re.shape[3], ssm_b_re.shape[4]
    assert bp % SUBLANE == 0 and bs % SUBLANE == 0 and d % MXU_DIM == 0 and MXU_DIM % group == 0
    assert 2 * head_dim == MXU_DIM and ls % GRID_W == 0

    pad_rows = -(bs + 1) % SUBLANE
    cond = jnp.concatenate([c, c_ctx[None], jnp.zeros((pad_rows, d), F32)], axis=0)
    mod = _modulation(cond, w_mod, b_mod).reshape(depth, cond.shape[0], 6, d)

    streams = [dict(x=x_prompt.reshape(bp * lp, d), batch=bp, seq=lp, time_major=False),
               dict(x=x_sample.reshape(bs * ls, d), batch=bs, seq=ls, time_major=False)]
    new_k, new_v, new_sre, new_sim = [], [], [], []
    q_scale = head_dim ** -0.5 * math.log2(math.e)
    rope = _rope_tables(ls, head_dim, q_scale)
    ngb = d // MXU_DIM

    for i in range(depth):
        mix, slot = i % N_MIXERS, i // N_MIXERS
        mods = [[jnp.broadcast_to(mod[i, bs, j], (bp, d)) for j in range(6)],
                [mod[i, :bs, j] for j in range(6)]]
        g1 = norm_g[i, 0][None]
        g2 = norm_g[i, 1][None]

        def per_group(v):
            return v.reshape(v.shape[0] // SUBLANE, SUBLANE, d)

        if mix == 0:
            lam_init = 0.8 - 0.6 * math.exp(-0.3 * i)
            nqk = n_heads * 2 * head_dim
            wq = _to_bf16(w_qkv, slot, swap_cols=2 * nqk, head_dim=head_dim)
            wo = _to_bf16(w_o, slot)
            sub_g = subln_g[slot][None]
            ck = cache_k.reshape(cache_k.shape[:3] + (-1,))
            cv = cache_v.reshape(cache_v.shape[:3] + (-1,))
            for s, (st, md) in enumerate(zip(streams, mods)):
                b, l = st['batch'], st['seq']
                if st['time_major']:
                    st['x'] = _to_batch_major(st['x'], b, l)
                sh1, sc1 = (v[:1, None] if s == 0 else v[:, None] for v in (md[0], md[1]))
                if s == 0:
                    q_p, k_p, v_p = _qkv_proj(st['x'], g1, sc1, sh1, wq, l, q_scale)
                    new_k.append(_swap_middle_quarters(k_p, head_dim).reshape(b, l, n_heads, 2, head_dim))
                    new_v.append(v_p.reshape(b, l, n_heads, 2 * head_dim))
                    o = _diff_attention((q_p, k_p, v_p), lam_vecs[slot], sub_g, lam_init, b, l, n_heads, head_dim)
                else:
                    qkv = _qkv_proj(st['x'], g1, sc1, sh1, wq, l, q_scale, rope_tables=rope)
                    o = _diff_attention(qkv, lam_vecs[slot], sub_g, lam_init, b, l, n_heads, head_dim,
                                        cache_k=ck, cache_v=cv, slot=slot)
                st['x'] = _attn_out_proj(o, wo, st['x'], md[2][:, None], b, l)
                st['time_major'] = True
        else:
            lb_re, lb_im, bb_re, bb_im = _s5_discretize(ssm_a_re[slot], ssm_a_im[slot], ssm_log_step[slot],
                                                        ssm_b_re[slot], ssm_b_im[slot])
            wb, wc, lam = _s5_block_weights(lb_re, lb_im, bb_re, bb_im, ssm_c_re[slot], ssm_c_im[slot],
                                            n_groups, state_dim, group)
            wg = _to_bf16(w_glu, slot)
            for s, (st, md) in enumerate(zip(streams, mods)):
                b, l = st['batch'], st['seq']
                sh1, sc1, gt1 = per_group(md[0]), per_group(md[1]), per_group(md[2])
                u = _s5_input(st['x'], g1, sc1, sh1, l)
                if s == 0:
                    h0 = jnp.zeros((b // SUBLANE, 2, ngb, wb.shape[2], SUBLANE, MXU_DIM), F32)
                else:
                    h0 = _s5_state_blocks(state_re[:, slot], state_im[:, slot], ngb)
                y, fin = _s5_scan(u, wb, wc, lam, h0, l)
                if s == 0:
                    f_re, f_im = _s5_state_unblock(fin, n_groups, state_dim)
                    new_sre.append(f_re)
                    new_sim.append(f_im)
                st['x'] = _s5_glu(st['x'], y, g1, sc1, sh1, ssm_d[slot][None], wg, b_glu[slot][None], gt1, l)
        wu = _to_bf16(w_up, i)
        wd = _to_bf16(w_down, i)
        for st, md in zip(streams, mods):
            sh2, sc2, gt2 = per_group(md[3]), per_group(md[4]), per_group(md[5])
            act = _conv_ffn_up(st['x'], g2, sc2, sh2, wu, conv_w[i], conv_b[i][None], st['seq'])
            st['x'] = _matmul_gate_residual(act, wd, st['x'], gt2, st['seq'])

    y_prompt = _final_norm(streams[0]['x'], final_g[None], bp, lp)
    y_sample = _final_norm(streams[1]['x'], final_g[None], bs, ls)
    return (y_prompt, y_sample, jnp.stack(new_k, axis=1), jnp.stack(new_v, axis=1),
            jnp.stack(new_sre, axis=1), jnp.stack(new_sim, axis=1))
```

```python
import functools
import math

import jax
import jax.numpy as jnp
from jax import lax
from jax.experimental import pallas as pl
from jax.experimental.pallas import tpu as pltpu

F32 = jnp.float32
BF16 = jnp.bfloat16

GRID_W = 64
ROPE_THETA = 10000.0
NORM_EPS = 1e-6
SUBLN_EPS = 1e-5
N_MIXERS = 2

LANE = 128
SUBLANE = 8
BF16_ROWS = 16
MXU_DIM = 256
VMEM_LIMIT_BYTES = 52 * 1024 * 1024


def _tile(n, pref, align):
    t = min(pref, n)
    t -= t % align
    while t >= align:
        if n % t == 0:
            return t
        t -= align
    return n


def _row_tile(m, seq_len, n_mod, pref, align):
    return _tile(m, pref, align) if n_mod == 1 else _tile(seq_len, pref, align)


def _params(*sem):
    return pltpu.CompilerParams(dimension_semantics=sem, vmem_limit_bytes=VMEM_LIMIT_BYTES)


def _rstd(x):
    return lax.rsqrt(jnp.mean(x * x, axis=-1, keepdims=True) + NORM_EPS)


def _norm_mod(x, g, sc, sh):
    return x * _rstd(x) * (g * (1.0 + sc)) + sh


def _per_seq(x, pattern, op):
    rows, n = x.shape
    return op(x.reshape(rows // SUBLANE, SUBLANE, n), pattern[None]).reshape(rows, n)


def _norm_mod_tm(x, g, sc, sh):
    scaled = _per_seq(x * _rstd(x), g * (1.0 + sc), jnp.multiply)
    return _per_seq(scaled, sh, jnp.add)


def _mod_kernel(c_ref, w_ref, b_ref, o_ref):
    c = c_ref[...]
    s = (c * jax.nn.sigmoid(c)).astype(BF16)
    o_ref[0] = jnp.dot(s, w_ref[0].astype(BF16), preferred_element_type=F32) + b_ref[0]


def _modulation(cond, w_mod, b_mod):
    depth, d, n = w_mod.shape
    rows = cond.shape[0]
    tn = _tile(n, 1024, LANE)
    return pl.pallas_call(
        _mod_kernel,
        grid=(depth, n // tn),
        in_specs=[pl.BlockSpec((rows, d), lambda l, j: (0, 0)),
                  pl.BlockSpec((1, d, tn), lambda l, j: (l, 0, j)),
                  pl.BlockSpec((1, 1, tn), lambda l, j: (l, 0, j))],
        out_specs=pl.BlockSpec((1, rows, tn), lambda l, j: (l, 0, j)),
        out_shape=jax.ShapeDtypeStruct((depth, rows, n), F32),
        compiler_params=_params("parallel", "parallel"),
        name="modulation",
    )(cond, w_mod, b_mod.reshape(depth, 1, n))


def _swap_quarters_in_lanes(x, head_dim):
    q4 = head_dim // 4
    parts = []
    for c in range(x.shape[1] // head_dim):
        base = c * head_dim
        parts += [x[:, base:base + q4], x[:, base + 2 * q4:base + 3 * q4],
                  x[:, base + q4:base + 2 * q4], x[:, base + 3 * q4:base + 4 * q4]]
    return jnp.concatenate(parts, axis=1)


def _cast_kernel(w_ref, o_ref, *, swap_tiles, head_dim):
    w = w_ref[0]
    if swap_tiles == 0:
        o_ref[...] = w.astype(BF16)
        return
    j = pl.program_id(1)

    @pl.when(j < swap_tiles)
    def _():
        o_ref[...] = _swap_quarters_in_lanes(w, head_dim).astype(BF16)

    @pl.when(j >= swap_tiles)
    def _():
        o_ref[...] = w.astype(BF16)


def _to_bf16(w_stack, index, swap_cols=0, head_dim=LANE):
    _, k, n = w_stack.shape
    tk = _tile(k, 1024, BF16_ROWS)
    tn = _tile(math.gcd(swap_cols, n), 2048, LANE)
    return pl.pallas_call(
        functools.partial(_cast_kernel, swap_tiles=swap_cols // tn, head_dim=head_dim),
        grid=(k // tk, n // tn),
        in_specs=[pl.BlockSpec((1, tk, tn), lambda i, j: (index, i, j))],
        out_specs=pl.BlockSpec((tk, tn), lambda i, j: (i, j)),
        out_shape=jax.ShapeDtypeStruct((k, n), BF16),
        compiler_params=_params("parallel", "parallel"),
        name="weight_to_bf16",
    )(w_stack)


def _qkv_kernel(*refs, rope, n_rope_tiles, tn, q_scale):
    if rope:
        x_ref, g_ref, sc_ref, sh_ref, w_ref, cos_ref, sin_ref, o_ref, xm_ref = refs
    else:
        x_ref, g_ref, sc_ref, sh_ref, w_ref, oq_ref, ok_ref, ov_ref, xm_ref = refs
    j = pl.program_id(1)

    @pl.when(j == 0)
    def _():
        xm_ref[...] = _norm_mod(x_ref[...], g_ref[...], sc_ref[0], sh_ref[0]).astype(BF16)

    acc = jnp.dot(xm_ref[...], w_ref[...], preferred_element_type=F32)
    if not rope:
        n_q = n_rope_tiles // 2

        @pl.when(j < n_q)
        def _():
            oq_ref[...] = (acc * q_scale).astype(oq_ref.dtype)

        @pl.when((j >= n_q) & (j < 2 * n_q))
        def _():
            ok_ref[...] = acc

        @pl.when(j >= 2 * n_q)
        def _():
            ov_ref[...] = acc

        return

    @pl.when(j < n_rope_tiles)
    def _():
        cos = cos_ref[0]
        sin = sin_ref[0]
        for c in range(tn // LANE):
            x = acc[:, c * LANE:(c + 1) * LANE]
            o_ref[:, c * LANE:(c + 1) * LANE] = (x * cos + pltpu.roll(x, LANE // 2, 1) * sin).astype(o_ref.dtype)

    @pl.when(j >= n_rope_tiles)
    def _():
        o_ref[...] = acc.astype(o_ref.dtype)


def _qkv_proj(x, g, sc, sh, w, seq_len, q_scale, rope_tables=None):
    m, d = x.shape
    n = w.shape[1]
    n_rope_cols = 2 * n // 3
    n_mod = sc.shape[0]
    tm = _tile(seq_len, 1024, BF16_ROWS) if rope_tables is not None else _row_tile(m, seq_len, n_mod, 1024, BF16_ROWS)
    tn = _tile(n_rope_cols // 2, 1024 if rope_tables is not None else 512, LANE)
    nq = n_rope_cols // (2 * tn)

    def mod_i(i):
        return (i * tm // seq_len) % n_mod

    in_specs = [pl.BlockSpec((tm, d), lambda i, j: (i, 0)),
                pl.BlockSpec((1, d), lambda i, j: (0, 0)),
                pl.BlockSpec((1, 1, d), lambda i, j: (mod_i(i), 0, 0)),
                pl.BlockSpec((1, 1, d), lambda i, j: (mod_i(i), 0, 0)),
                pl.BlockSpec((d, tn), lambda i, j: (0, j))]
    args = [x, g, sc, sh, w]
    if rope_tables is not None:
        tiles_per_seq = seq_len // tm
        in_specs += [pl.BlockSpec((1, tm, LANE),
                                  lambda i, j: (jnp.where(j < nq, 0, 1), i % tiles_per_seq, 0))] * 2
        args += list(rope_tables)
        out_specs = pl.BlockSpec((tm, tn), lambda i, j: (i, j))
        out_shape = jax.ShapeDtypeStruct((m, n), BF16)
    else:
        out_specs = [pl.BlockSpec((tm, tn), lambda i, j, part=part: (i, jnp.clip(j - part * nq, 0, nq - 1)))
                     for part in range(3)]
        out_shape = [jax.ShapeDtypeStruct((m, n // 3), dt) for dt in (BF16, F32, F32)]
    return pl.pallas_call(
        functools.partial(_qkv_kernel, rope=rope_tables is not None,
                          n_rope_tiles=n_rope_cols // tn, tn=tn, q_scale=q_scale),
        grid=(m // tm, n // tn),
        in_specs=in_specs,
        out_specs=out_specs,
        out_shape=out_shape,
        scratch_shapes=[pltpu.VMEM((tm, d), BF16)],
        compiler_params=_params("parallel", "arbitrary"),
        name="qkv_proj",
    )(*args)


def _swap_middle_quarters(a, head_dim):
    lead = a.shape[:-1]
    a = a.reshape(lead + (a.shape[-1] // head_dim, 2, 2, head_dim // 4))
    return jnp.swapaxes(a, -3, -2).reshape(lead + (-1,))


def _rope_tables(length, head_dim, q_scale):
    pos = jnp.arange(length)
    r = (pos // GRID_W).astype(F32)
    col = (pos % GRID_W).astype(F32)
    half = head_dim // 2
    inv = ROPE_THETA ** (-jnp.arange(0, half, 2, dtype=F32) / half)
    ar = r[:, None] * inv
    ac = col[:, None] * inv
    cos = jnp.concatenate([jnp.cos(ar), jnp.cos(ac)] * 2, axis=-1)
    sin = jnp.concatenate([-jnp.sin(ar), -jnp.sin(ac), jnp.sin(ar), jnp.sin(ac)], axis=-1)
    scales = jnp.array([q_scale, 1.0], F32)[:, None, None]
    return cos[None] * scales, sin[None] * scales


def _attn_kernel(*refs, has_cache, lam_init, head_dim, kv_chunk, heads):
    if has_cache:
        lam_ref, q_ref, k_ref, v_ref, ck_ref, cv_ref, g_ref, o_ref = refs
    else:
        lam_ref, q_ref, k_ref, v_ref, g_ref, o_ref = refs
    hw = 2 * head_dim
    lv = lam_ref[...]
    lam = (jnp.exp(jnp.sum(lv[0:1] * lv[1:2], axis=-1, keepdims=True))
           - jnp.exp(jnp.sum(lv[2:3] * lv[3:4], axis=-1, keepdims=True)) + lam_init)
    for hh in range(heads):
        cols = slice(hh * hw, (hh + 1) * hw)
        q = q_ref[:, cols]
        ks = [k_ref[:, cols].astype(BF16)]
        vs = [v_ref[:, cols].astype(BF16)]
        if has_cache:
            ks.append(_swap_quarters_in_lanes(ck_ref[0, 0, :, cols], head_dim).astype(BF16))
            vs.append(cv_ref[0, 0, :, cols].astype(BF16))
        kv = [(k[r:r + kv_chunk], v[r:r + kv_chunk]) for k, v in zip(ks, vs) for r in range(0, k.shape[0], kv_chunk)]
        outs = []
        for c in range(2):
            qc = q[:, c * head_dim:(c + 1) * head_dim]
            m = denom = acc = None
            for k, v in kv:
                s = lax.dot_general(qc, k[:, c * head_dim:(c + 1) * head_dim],
                                    (((1,), (1,)), ((), ())), preferred_element_type=F32)
                s_max = jnp.max(s, axis=-1, keepdims=True)
                m_new = s_max if m is None else jnp.maximum(m, s_max)
                p = jnp.exp2(s - m_new)
                p_sum = jnp.sum(p, axis=-1, keepdims=True)
                pv = jnp.dot(p.astype(BF16), v, preferred_element_type=F32)
                if m is None:
                    denom, acc = p_sum, pv
                else:
                    alpha = jnp.exp2(m - m_new)
                    denom, acc = denom * alpha + p_sum, acc * alpha + pv
                m = m_new
            outs.append(acc / denom)
        o = outs[0] - lam * outs[1]
        r = lax.rsqrt(jnp.mean(o * o, axis=-1, keepdims=True) + SUBLN_EPS)
        o_ref[:, cols] = (o * r * g_ref[...] * (1.0 - lam_init)).astype(o_ref.dtype)


def _diff_attention(qkv, lam_vecs, subln_g, lam_init, batch, seq_len, n_heads, head_dim,
                    cache_k=None, cache_v=None, slot=0):
    hw = 2 * head_dim
    tq = _tile(seq_len, 1024, BF16_ROWS)
    nq = seq_len // tq
    heads = _tile(n_heads, max(2048 // seq_len, 1), 1)
    bw = heads * hw
    ng = n_heads // heads
    has_cache = cache_k is not None
    if isinstance(qkv, tuple):
        q_arr, k_arr, v_arr = qkv
        k_off = v_off = 0
    else:
        q_arr = k_arr = v_arr = qkv
        k_off, v_off = ng, 2 * ng
    in_specs = [pl.BlockSpec(lam_vecs.shape, lambda b, h, i: (0, 0)),
                pl.BlockSpec((tq, bw), lambda b, h, i: (b * nq + i, h)),
                pl.BlockSpec((seq_len, bw), lambda b, h, i: (b, k_off + h)),
                pl.BlockSpec((seq_len, bw), lambda b, h, i: (b, v_off + h))]
    args = [lam_vecs, q_arr, k_arr, v_arr]
    if has_cache:
        past = cache_k.shape[2]
        in_specs += [pl.BlockSpec((1, 1, past, bw), lambda b, h, i: (b, slot, 0, h))] * 2
        args += [cache_k, cache_v]
    in_specs.append(pl.BlockSpec((1, hw), lambda b, h, i: (0, 0)))
    args.append(subln_g)
    return pl.pallas_call(
        functools.partial(_attn_kernel, has_cache=has_cache, lam_init=lam_init,
                          head_dim=head_dim, kv_chunk=_tile(seq_len, 512, BF16_ROWS), heads=heads),
        grid=(batch, ng, nq),
        in_specs=in_specs,
        out_specs=pl.BlockSpec((tq, bw), lambda b, h, i: (b * nq + i, h)),
        out_shape=jax.ShapeDtypeStruct((batch * seq_len, n_heads * hw), BF16),
        compiler_params=_params("parallel", "parallel", "arbitrary"),
        name="diff_attention",
    )(*args)


def _attn_out_kernel(a_ref, w_ref, x_ref, gate_ref, o_ref):
    nb, tl, k = a_ref.shape
    acc = jnp.dot(a_ref[...].reshape(nb * tl, k), w_ref[...], preferred_element_type=F32)
    res = x_ref[...] + gate_ref[...] * acc.reshape(nb, tl, -1)
    o_ref[...] = pltpu.einshape("btn->tbn", res).reshape(tl * nb, -1)


def _attn_out_proj(o, w, x, gate, batch, seq_len):
    k = o.shape[1]
    n = w.shape[1]
    tl = _tile(seq_len, 128, BF16_ROWS)
    tn = _tile(n, 1024, LANE)
    nl = seq_len // tl
    return pl.pallas_call(
        _attn_out_kernel,
        grid=(batch // SUBLANE, nl, n // tn),
        in_specs=[pl.BlockSpec((SUBLANE, tl, k), lambda s, t, j: (s, t, 0)),
                  pl.BlockSpec((k, tn), lambda s, t, j: (0, j)),
                  pl.BlockSpec((SUBLANE, tl, tn), lambda s, t, j: (s, t, j)),
                  pl.BlockSpec((SUBLANE, 1, tn), lambda s, t, j: (s, 0, j))],
        out_specs=pl.BlockSpec((tl * SUBLANE, tn), lambda s, t, j: (s * nl + t, j)),
        out_shape=jax.ShapeDtypeStruct((batch * seq_len, n), F32),
        compiler_params=_params("parallel", "parallel", "arbitrary"),
        name="attn_out_proj",
    )(o.reshape(batch, seq_len, k), w, x.reshape(batch, seq_len, n), gate)


def _to_batch_major_kernel(x_ref, o_ref):
    nb, tl, d = o_ref.shape
    o_ref[...] = pltpu.einshape("tbd->btd", x_ref[...].reshape(tl, nb, d))


def _to_batch_major(x, batch, seq_len):
    m, d = x.shape
    tl = _tile(seq_len, 64, SUBLANE)
    nl = seq_len // tl
    return pl.pallas_call(
        _to_batch_major_kernel,
        grid=(batch // SUBLANE, nl),
        in_specs=[pl.BlockSpec((tl * SUBLANE, d), lambda s, t: (s * nl + t, 0))],
        out_specs=pl.BlockSpec((SUBLANE, tl, d), lambda s, t: (s, t, 0)),
        out_shape=jax.ShapeDtypeStruct((batch, seq_len, d), F32),
        compiler_params=_params("parallel", "parallel"),
        name="to_batch_major",
    )(x).reshape(m, d)


def _mm_res_kernel(a_ref, w_ref, x_ref, gate_ref, o_ref):
    acc = jnp.dot(a_ref[...], w_ref[...], preferred_element_type=F32)
    o_ref[...] = x_ref[...] + _per_seq(acc, gate_ref[0], jnp.multiply)


def _matmul_gate_residual(a, w, x, gate, seq_len):
    m, k = a.shape
    n = w.shape[1]
    group_rows = seq_len * SUBLANE
    tm = _tile(group_rows, 1024, BF16_ROWS)
    tn = _tile(n, 512, LANE)
    return pl.pallas_call(
        _mm_res_kernel,
        grid=(m // tm, n // tn),
        in_specs=[pl.BlockSpec((tm, k), lambda i, j: (i, 0)),
                  pl.BlockSpec((k, tn), lambda i, j: (0, j)),
                  pl.BlockSpec((tm, tn), lambda i, j: (i, j)),
                  pl.BlockSpec((1, SUBLANE, tn), lambda i, j: (i * tm // group_rows, 0, j))],
        out_specs=pl.BlockSpec((tm, tn), lambda i, j: (i, j)),
        out_shape=jax.ShapeDtypeStruct((m, n), F32),
        compiler_params=_params("parallel", "arbitrary"),
        name="matmul_gate_residual",
    )(a, w, x, gate)


def _up_kernel(x_ref, xp_ref, xn_ref, g_ref, sc_ref, sh_ref, wg_ref, wv_ref,
               cwg_ref, cwv_ref, cbg_ref, cbv_ref, o_ref, xm_ref, *, tm, group_rows, chunk):
    halo = BF16_ROWS
    i = pl.program_id(0)
    j = pl.program_id(1)

    @pl.when(j == 0)
    def _():
        g, sc, sh = g_ref[...], sc_ref[0], sh_ref[0]
        xm_ref[0:halo] = _norm_mod_tm(xp_ref[...], g, sc, sh).astype(BF16)
        xm_ref[halo:halo + tm] = _norm_mod_tm(x_ref[...], g, sc, sh).astype(BF16)
        xm_ref[halo + tm:] = _norm_mod_tm(xn_ref[...], g, sc, sh).astype(BF16)

        @pl.when((i * tm) % group_rows == 0)
        def _():
            xm_ref[0:halo] = jnp.zeros((halo, xm_ref.shape[1]), BF16)

        @pl.when(((i + 1) * tm) % group_rows == 0)
        def _():
            xm_ref[halo + tm:] = jnp.zeros((halo, xm_ref.shape[1]), BF16)

    xm = xm_ref[...]

    def conv(h, cw_ref, cb_ref, cols):
        cw = cw_ref[:, cols]
        lo = halo - SUBLANE
        return (h[lo:lo + tm] * cw[0:1] + h[halo:halo + tm] * cw[1:2]
                + h[halo + SUBLANE:halo + SUBLANE + tm] * cw[2:3] + cb_ref[:, cols])

    for c in range(o_ref.shape[1] // chunk):
        cols = slice(c * chunk, (c + 1) * chunk)
        gate = conv(jnp.dot(xm, wg_ref[:, cols], preferred_element_type=F32), cwg_ref, cbg_ref, cols)
        val = conv(jnp.dot(xm, wv_ref[:, cols], preferred_element_type=F32), cwv_ref, cbv_ref, cols)
        o_ref[:, cols] = (gate * jax.nn.sigmoid(gate) * val).astype(o_ref.dtype)


def _conv_ffn_up(x, g, sc, sh, w_up, conv_w, conv_b, seq_len):
    m, d = x.shape
    d_ff = w_up.shape[1] // 2
    halo = BF16_ROWS
    group_rows = seq_len * SUBLANE
    tm = _tile(group_rows, 1024, halo)
    tn = _tile(d_ff, 512, LANE)
    chunk = _tile(tn, MXU_DIM, LANE)
    nj = d_ff // tn
    hb = tm // halo
    last_hb = m // halo - 1

    mod_spec = pl.BlockSpec((1, SUBLANE, d), lambda i, j: (i * tm // group_rows, 0, 0))
    return pl.pallas_call(
        functools.partial(_up_kernel, tm=tm, group_rows=group_rows, chunk=chunk),
        grid=(m // tm, nj),
        in_specs=[pl.BlockSpec((tm, d), lambda i, j: (i, 0)),
                  pl.BlockSpec((halo, d), lambda i, j: (jnp.maximum(i * hb - 1, 0), 0)),
                  pl.BlockSpec((halo, d), lambda i, j: (jnp.minimum((i + 1) * hb, last_hb), 0)),
                  pl.BlockSpec((1, d), lambda i, j: (0, 0)),
                  mod_spec, mod_spec,
                  pl.BlockSpec((d, tn), lambda i, j: (0, j)),
                  pl.BlockSpec((d, tn), lambda i, j: (0, nj + j)),
                  pl.BlockSpec((3, tn), lambda i, j: (0, j)),
                  pl.BlockSpec((3, tn), lambda i, j: (0, nj + j)),
                  pl.BlockSpec((1, tn), lambda i, j: (0, j)),
                  pl.BlockSpec((1, tn), lambda i, j: (0, nj + j))],
        out_specs=pl.BlockSpec((tm, tn), lambda i, j: (i, j)),
        out_shape=jax.ShapeDtypeStruct((m, d_ff), BF16),
        scratch_shapes=[pltpu.VMEM((tm + 2 * halo, d), BF16)],
        compiler_params=_params("parallel", "arbitrary"),
        name="conv_ffn_up",
    )(x, x, x, g, sc, sh, w_up, w_up, conv_w, conv_w, conv_b, conv_b)


def _discretize(a_re, a_im, log_step):
    dt = jnp.exp(log_step)
    mag = jnp.exp(a_re * dt)
    lb_re = mag * jnp.cos(a_im * dt)
    lb_im = mag * jnp.sin(a_im * dt)
    num_re = lb_re - 1.0
    den = a_re * a_re + a_im * a_im
    f_re = (num_re * a_re + lb_im * a_im) / den
    f_im = (lb_im * a_re - num_re * a_im) / den
    return lb_re, lb_im, f_re, f_im


def _s5_disc_kernel(a_re_ref, a_im_ref, ls_ref, ax_re_ref, ax_im_ref, lsx_ref, b_re_ref, b_im_ref,
                    lb_re_ref, lb_im_ref, bb_re_ref, bb_im_ref):
    lb_re, lb_im, _, _ = _discretize(a_re_ref[...], a_im_ref[...], ls_ref[...])
    lb_re_ref[...] = lb_re
    lb_im_ref[...] = lb_im
    _, _, f_re, f_im = _discretize(ax_re_ref[...], ax_im_ref[...], lsx_ref[...])
    b_re, b_im = b_re_ref[...], b_im_ref[...]
    bb_re_ref[...] = f_re * b_re - f_im * b_im
    bb_im_ref[...] = f_re * b_im + f_im * b_re


def _s5_discretize(a_re, a_im, log_step, b_re, b_im):
    two, g, p, c = b_re.shape
    rows = two * g
    a2 = lambda a: a.reshape(rows, p)
    ax = lambda a: jnp.repeat(a.reshape(rows, p), c, axis=-1)
    ls = jnp.broadcast_to(log_step.reshape(rows, 1), (rows, p))
    lsx = jnp.broadcast_to(log_step.reshape(rows, 1), (rows, p * c))
    return pl.pallas_call(
        _s5_disc_kernel,
        out_shape=[jax.ShapeDtypeStruct((rows, p), F32)] * 2 + [jax.ShapeDtypeStruct((rows, p * c), F32)] * 2,
        compiler_params=pltpu.CompilerParams(vmem_limit_bytes=VMEM_LIMIT_BYTES),
        name="s5_discretize",
    )(a2(a_re), a2(a_im), ls, ax(a_re), ax(a_im), lsx, b_re.reshape(rows, p * c), b_im.reshape(rows, p * c))


def _s5_in_kernel(x_ref, g_ref, sc_ref, sh_ref, o_ref):
    o_ref[...] = _norm_mod_tm(x_ref[...], g_ref[...], sc_ref[0], sh_ref[0]).astype(o_ref.dtype)


def _s5_input(x, g, sc, sh, seq_len):
    m, d = x.shape
    group_rows = seq_len * SUBLANE
    tm = _tile(group_rows, 1024, BF16_ROWS)
    mod_spec = pl.BlockSpec((1, SUBLANE, d), lambda i: (i * tm // group_rows, 0, 0))
    return pl.pallas_call(
        _s5_in_kernel,
        grid=(m // tm,),
        in_specs=[pl.BlockSpec((tm, d), lambda i: (i, 0)), pl.BlockSpec((1, d), lambda i: (0, 0)),
                  mod_spec, mod_spec],
        out_specs=pl.BlockSpec((tm, d), lambda i: (i, 0)),
        out_shape=jax.ShapeDtypeStruct((m, d), BF16),
        compiler_params=_params("parallel"),
        name="s5_input",
    )(x, g, sc, sh)


def _s5_scan_kernel(u_ref, wb_ref, wc_ref, lam_ref, h0_ref, y_ref, fin_ref, st_ref, buf_ref,
                    *, nsub, nt, sub_t):
    d = pl.program_id(1)
    t_idx = pl.program_id(3)
    n_tiles = st_ref.shape[0]
    half = n_tiles // 2
    sr = sub_t * SUBLANE
    steps = sub_t // n_tiles

    @pl.when(t_idx == 0)
    def _():
        st_ref[...] = h0_ref[0, 0, 0]

    lam = jnp.broadcast_to(lam_ref[0, 0], st_ref.shape)
    a_re, a_im = lam[:half], lam[half:]

    def first_row(p):
        return pl.multiple_of((p + d * (nsub - 1 - 2 * p)) * sr, sr)

    def run_pass(p, state, do_bu, do_scan, do_ch, phase=None):
        phase = p % 3 if phase is None else phase
        slot_bu, slot_scan, slot_ch = phase, (phase + 2) % 3, (phase + 1) % 3
        if do_ch:
            h_all = jnp.concatenate([buf_ref[slot_ch, i] for i in range(n_tiles)], axis=1)
            y_ref[0, pl.ds(first_row(p - 2), sr), :] = jnp.dot(h_all.astype(BF16), wc_ref[0, 0],
                                                               preferred_element_type=F32)
        if do_bu:
            u_sub = u_ref[pl.ds(first_row(p), sr), :]
        h_re, h_im = state
        for i in range(n_tiles):
            if do_bu:
                buf_ref[slot_bu, i] = jnp.dot(u_sub, wb_ref[0, 0, i], preferred_element_type=F32)
            if do_scan:
                for k in range(steps):
                    t = i * steps + k
                    row = pl.multiple_of((t + d * (sub_t - 1 - 2 * t)) * SUBLANE, SUBLANE)
                    bu = buf_ref[slot_scan, :, pl.ds(row, SUBLANE), :]
                    n_re = a_re * h_re - a_im * h_im + bu[:half]
                    n_im = a_re * h_im + a_im * h_re + bu[half:]
                    buf_ref[slot_scan, 0:half, pl.ds(row, SUBLANE), :] = n_re
                    buf_ref[slot_scan, half:, pl.ds(row, SUBLANE), :] = n_im
                    h_re, h_im = n_re, n_im
        return h_re, h_im

    state = (st_ref[0:half], st_ref[half:])
    for p in range(min(2, nsub + 2)):
        state = run_pass(p, state, p < nsub, 1 <= p <= nsub, False)
    n_trips = max(nsub - 2, 0) // 3

    def three_passes(trip, st):
        for k in range(3):
            st = run_pass(2 + 3 * trip + k, st, True, True, True, phase=(2 + k) % 3)
        return st

    if n_trips > 0:
        state = lax.fori_loop(0, n_trips, three_passes, state)
    for p in range(2 + 3 * n_trips, nsub):
        state = run_pass(p, state, True, True, True)
    for p in range(max(2, nsub), nsub + 2):
        state = run_pass(p, state, p < nsub, p <= nsub, True)
    st_ref[0:half] = state[0]
    st_ref[half:] = state[1]

    @pl.when(t_idx == nt - 1)
    def _():
        fin_ref[0, 0, 0] = st_ref[...]


def _s5_scan(u, wb, wc, lam, h0, seq_len):
    rows, d = u.shape
    ns = rows // (seq_len * SUBLANE)
    ngb = d // MXU_DIM
    n_tiles = wb.shape[2]
    sub_t = _tile(seq_len, 64, n_tiles)
    tc = _tile(seq_len, 16 * sub_t, sub_t)
    nt = seq_len // tc
    assert sub_t % n_tiles == 0 and tc % sub_t == 0

    def t_blk(n, dd, t):
        return n * nt + t + dd * (nt - 1 - 2 * t)

    w_spec = pl.BlockSpec((1, 1, n_tiles, MXU_DIM, MXU_DIM), lambda n, dd, gb, t: (dd, gb, 0, 0, 0))
    st_spec = pl.BlockSpec((1, 1, 1, n_tiles, SUBLANE, MXU_DIM), lambda n, dd, gb, t: (n, dd, gb, 0, 0, 0))
    return pl.pallas_call(
        functools.partial(_s5_scan_kernel, nsub=tc // sub_t, nt=nt, sub_t=sub_t),
        grid=(ns, 2, ngb, nt),
        in_specs=[pl.BlockSpec((tc * SUBLANE, MXU_DIM), lambda n, dd, gb, t: (t_blk(n, dd, t), gb)),
                  w_spec,
                  pl.BlockSpec((1, 1, n_tiles * MXU_DIM, MXU_DIM), lambda n, dd, gb, t: (dd, gb, 0, 0)),
                  pl.BlockSpec((1, 1, n_tiles, 1, MXU_DIM), lambda n, dd, gb, t: (dd, gb, 0, 0, 0)),
                  st_spec],
        out_specs=[pl.BlockSpec((1, tc * SUBLANE, MXU_DIM), lambda n, dd, gb, t: (dd, t_blk(n, dd, t), gb)),
                   st_spec],
        out_shape=[jax.ShapeDtypeStruct((2, rows, d), F32),
                   jax.ShapeDtypeStruct((ns, 2, ngb, n_tiles, SUBLANE, MXU_DIM), F32)],
        scratch_shapes=[pltpu.VMEM((n_tiles, SUBLANE, MXU_DIM), F32),
                        pltpu.VMEM((3, n_tiles, sub_t * SUBLANE, MXU_DIM), F32)],
        compiler_params=_params("parallel", "parallel", "parallel", "arbitrary"),
        name="s5_scan",
    )(u, wb, wc, lam, h0)


def _s5_block_weights(lb_re, lb_im, bb_re, bb_im, c_re, c_im, n_groups, state_dim, group):
    gpb = MXU_DIM // group
    ngb = n_groups // gpb
    eye = jnp.eye(gpb, dtype=F32)

    def b_blocks(bb):
        bb = bb.reshape(2, ngb, gpb, state_dim, group)
        return jnp.einsum('dbgpc,gh->dbgchp', bb, eye).reshape(2, ngb, gpb * group, gpb * state_dim)

    def c_blocks(cc):
        cc = cc.reshape(2, ngb, gpb, group, state_dim)
        return jnp.einsum('dbgcp,gh->dbgphc', cc, eye).reshape(2, ngb, gpb * state_dim, gpb * group)

    wb = jnp.concatenate([b_blocks(bb_re), b_blocks(bb_im)], axis=-1).astype(BF16)
    wc = jnp.concatenate([c_blocks(c_re), -c_blocks(c_im)], axis=-2).astype(BF16)
    lam = jnp.concatenate([lb_re.reshape(2, ngb, 1, gpb * state_dim),
                           lb_im.reshape(2, ngb, 1, gpb * state_dim)], axis=-1)
    n_tiles = wb.shape[-1] // MXU_DIM
    wb = wb.reshape(2, ngb, MXU_DIM, n_tiles, MXU_DIM).transpose(0, 1, 3, 2, 4)
    lam = lam.reshape(2, ngb, n_tiles, 1, MXU_DIM)
    return wb, wc, lam


def _s5_state_blocks(s_re, s_im, ngb):
    b, two, g, p = s_re.shape
    sw = g * p // ngb

    def blk(s):
        return s.reshape(b // SUBLANE, SUBLANE, two, ngb, sw).transpose(0, 2, 3, 1, 4)

    st = jnp.concatenate([blk(s_re), blk(s_im)], axis=-1)
    st = st.reshape(st.shape[:4] + (2 * sw // MXU_DIM, MXU_DIM))
    return st.transpose(0, 1, 2, 4, 3, 5)


def _s5_state_unblock(fin, g, p):
    ns, two, ngb, n_tiles, sub, lanes = fin.shape
    fin = fin.transpose(0, 1, 2, 4, 3, 5).reshape(ns, two, ngb, sub, n_tiles * lanes)
    sw = n_tiles * lanes // 2

    def unblk(s):
        return s.transpose(0, 3, 1, 2, 4).reshape(ns * sub, two, g, p)

    return unblk(fin[..., :sw]), unblk(fin[..., sw:])


def _glu_kernel(x_ref, yf_ref, yb_ref, g_ref, sc_ref, sh_ref, dskip_ref, w_ref, b_ref, gate_ref, o_ref):
    x = x_ref[...]
    xm = _norm_mod_tm(x, g_ref[...], sc_ref[0], sh_ref[0])
    gz = jax.nn.gelu(yf_ref[0] + yb_ref[0] + dskip_ref[...] * xm)
    acc = jnp.dot(gz.astype(BF16), w_ref[...], preferred_element_type=F32) + b_ref[...]
    o_ref[...] = x + _per_seq(gz * jax.nn.sigmoid(acc), gate_ref[0], jnp.multiply)


def _s5_glu(x, y, g, sc, sh, d_skip, w_glu, b_glu, gate, seq_len):
    m, d = x.shape
    group_rows = seq_len * SUBLANE
    tm = _tile(group_rows, 256, BF16_ROWS)
    mod_spec = pl.BlockSpec((1, SUBLANE, d), lambda i: (i * tm // group_rows, 0, 0))
    vec_spec = pl.BlockSpec((1, d), lambda i: (0, 0))
    return pl.pallas_call(
        _glu_kernel,
        grid=(m // tm,),
        in_specs=[pl.BlockSpec((tm, d), lambda i: (i, 0)),
                  pl.BlockSpec((1, tm, d), lambda i: (0, i, 0)),
                  pl.BlockSpec((1, tm, d), lambda i: (1, i, 0)),
                  vec_spec, mod_spec, mod_spec, vec_spec,
                  pl.BlockSpec((d, d), lambda i: (0, 0)),
                  vec_spec, mod_spec],
        out_specs=pl.BlockSpec((tm, d), lambda i: (i, 0)),
        out_shape=jax.ShapeDtypeStruct((m, d), F32),
        compiler_params=_params("parallel"),
        name="s5_glu",
    )(x, y, y, g, sc, sh, d_skip, w_glu, b_glu, gate)


def _final_norm_kernel(x_ref, g_ref, o_ref):
    nb, tl, d = o_ref.shape
    x = x_ref[...]
    y = x * _rstd(x) * g_ref[...]
    o_ref[...] = pltpu.einshape("tbd->btd", y.reshape(tl, nb, d))


def _final_norm(x, g, batch, seq_len):
    m, d = x.shape
    tl = _tile(seq_len, 64, SUBLANE)
    nl = seq_len // tl
    return pl.pallas_call(
        _final_norm_kernel,
        grid=(batch // SUBLANE, nl),
        in_specs=[pl.BlockSpec((tl * SUBLANE, d), lambda s, t: (s * nl + t, 0)),
                  pl.BlockSpec((1, d), lambda s, t: (0, 0))],
        out_specs=pl.BlockSpec((SUBLANE, tl, d), lambda s, t: (s, t, 0)),
        out_shape=jax.ShapeDtypeStruct((batch, seq_len, d), F32),
        compiler_params=_params("parallel", "parallel"),
        name="final_norm",
    )(x, g)


def kernel(x_prompt, x_sample, cache_k, cache_v, state_re, state_im, c, c_ctx, w_mod, b_mod, norm_g, w_qkv, lam_vecs, subln_g, w_o, ssm_a_re, ssm_a_im, ssm_log_step, ssm_b_re, ssm_b_im, ssm_c_re, ssm_c_im, ssm_d, w_glu, b_glu, w_up, conv_w, conv_b, w_down, final_g):
    bp, lp, d = x_prompt.shape
    bs, ls, _ = x_sample.shape
    depth = w_mod.shape[0]
    n_heads, head_dim = cache_k.shape[3], cache_k.shape[5]
    n_groups, state_dim, group = ssm_b_re.shape[2], ssm_b_re.shape[3], ssm_b_re.shape[4]
    assert bp % SUBLANE == 0 and bs % SUBLANE == 0 and d % MXU_DIM == 0 and MXU_DIM % group == 0
    assert 2 * head_dim == MXU_DIM and ls % GRID_W == 0

    pad_rows = -(bs + 1) % SUBLANE
    cond = jnp.concatenate([c, c_ctx[None], jnp.zeros((pad_rows, d), F32)], axis=0)
    mod = _modulation(cond, w_mod, b_mod).reshape(depth, cond.shape[0], 6, d)

    streams = [dict(x=x_prompt.reshape(bp * lp, d), batch=bp, seq=lp, time_major=False),
               dict(x=x_sample.reshape(bs * ls, d), batch=bs, seq=ls, time_major=False)]
    new_k, new_v, new_sre, new_sim = [], [], [], []
    q_scale = head_dim ** -0.5 * math.log2(math.e)
    rope = _rope_tables(ls, head_dim, q_scale)
    ngb = d // MXU_DIM

    for i in range(depth):
        mix, slot = i % N_MIXERS, i // N_MIXERS
        mods = [[jnp.broadcast_to(mod[i, bs, j], (bp, d)) for j in range(6)],
                [mod[i, :bs, j] for j in range(6)]]
        g1 = norm_g[i, 0][None]
        g2 = norm_g[i, 1][None]

        def per_group(v):
            return v.reshape(v.shape[0] // SUBLANE, SUBLANE, d)

        if mix == 0:
            lam_init = 0.8 - 0.6 * math.exp(-0.3 * i)
            nqk = n_heads * 2 * head_dim
            wq = _to_bf16(w_qkv, slot, swap_cols=2 * nqk, head_dim=head_dim)
            wo = _to_bf16(w_o, slot)
            sub_g = subln_g[slot][None]
            ck = cache_k.reshape(cache_k.shape[:3] + (-1,))
            cv = cache_v.reshape(cache_v.shape[:3] + (-1,))
            for s, (st, md) in enumerate(zip(streams, mods)):
                b, l = st['batch'], st['seq']
                if st['time_major']:
                    st['x'] = _to_batch_major(st['x'], b, l)
                sh1, sc1 = (v[:1, None] if s == 0 else v[:, None] for v in (md[0], md[1]))
                if s == 0:
                    q_p, k_p, v_p = _qkv_proj(st['x'], g1, sc1, sh1, wq, l, q_scale)
                    new_k.append(_swap_middle_quarters(k_p, head_dim).reshape(b, l, n_heads, 2, head_dim))
                    new_v.append(v_p.reshape(b, l, n_heads, 2 * head_dim))
                    o = _diff_attention((q_p, k_p, v_p), lam_vecs[slot], sub_g, lam_init, b, l, n_heads, head_dim)
                else:
                    qkv = _qkv_proj(st['x'], g1, sc1, sh1, wq, l, q_scale, rope_tables=rope)
                    o = _diff_attention(qkv, lam_vecs[slot], sub_g, lam_init, b, l, n_heads, head_dim,
                                        cache_k=ck, cache_v=cv, slot=slot)
                st['x'] = _attn_out_proj(o, wo, st['x'], md[2][:, None], b, l)
                st['time_major'] = True
        else:
            lb_re, lb_im, bb_re, bb_im = _s5_discretize(ssm_a_re[slot], ssm_a_im[slot], ssm_log_step[slot],
                                                        ssm_b_re[slot], ssm_b_im[slot])
            wb, wc, lam = _s5_block_weights(lb_re, lb_im, bb_re, bb_im, ssm_c_re[slot], ssm_c_im[slot],
                                            n_groups, state_dim, group)
            wg = _to_bf16(w_glu, slot)
            for s, (st, md) in enumerate(zip(streams, mods)):
                b, l = st['batch'], st['seq']
                sh1, sc1, gt1 = per_group(md[0]), per_group(md[1]), per_group(md[2])
                u = _s5_input(st['x'], g1, sc1, sh1, l)
                if s == 0:
                    h0 = jnp.zeros((b // SUBLANE, 2, ngb, wb.shape[2], SUBLANE, MXU_DIM), F32)
                else:
                    h0 = _s5_state_blocks(state_re[:, slot], state_im[:, slot], ngb)
                y, fin = _s5_scan(u, wb, wc, lam, h0, l)
                if s == 0:
                    f_re, f_im = _s5_state_unblock(fin, n_groups, state_dim)
                    new_sre.append(f_re)
                    new_sim.append(f_im)
                st['x'] = _s5_glu(st['x'], y, g1, sc1, sh1, ssm_d[slot][None], wg, b_glu[slot][None], gt1, l)
        wu = _to_bf16(w_up, i)
        wd = _to_bf16(w_down, i)
        for st, md in zip(streams, mods):
            sh2, sc2, gt2 = per_group(md[3]), per_group(md[4]), per_group(md[5])
            act = _conv_ffn_up(st['x'], g2, sc2, sh2, wu, conv_w[i], conv_b[i][None], st['seq'])
            st['x'] = _matmul_gate_residual(act, wd, st['x'], gt2, st['seq'])

    y_prompt = _final_norm(streams[0]['x'], final_g[None], bp, lp)
    y_sample = _final_norm(streams[1]['x'], final_g[None], bs, ls)
    return (y_prompt, y_sample, jnp.stack(new_k, axis=1), jnp.stack(new_v, axis=1),
            jnp.stack(new_sre, axis=1), jnp.stack(new_sim, axis=1))
```

```python
import functools
import math

import jax
import jax.numpy as jnp
from jax import lax
from jax.experimental import pallas as pl
from jax.experimental.pallas import tpu as pltpu

F32 = jnp.float32
BF16 = jnp.bfloat16

GRID_W = 64
ROPE_THETA = 10000.0
NORM_EPS = 1e-6
SUBLN_EPS = 1e-5
N_MIXERS = 2

LANE = 128
SUBLANE = 8
BF16_ROWS = 16
MXU_DIM = 256
VMEM_LIMIT_BYTES = 52 * 1024 * 1024


def _tile(n, pref, align):
    t = min(pref, n)
    t -= t % align
    while t >= align:
        if n % t == 0:
            return t
        t -= align
    return n


def _row_tile(m, seq_len, n_mod, pref, align):
    return _tile(m, pref, align) if n_mod == 1 else _tile(seq_len, pref, align)


def _params(*sem):
    return pltpu.CompilerParams(dimension_semantics=sem, vmem_limit_bytes=VMEM_LIMIT_BYTES)


def _rstd(x):
    return lax.rsqrt(jnp.mean(x * x, axis=-1, keepdims=True) + NORM_EPS)


def _norm_mod(x, g, sc, sh):
    return x * _rstd(x) * (g * (1.0 + sc)) + sh


def _per_seq(x, pattern, op):
    rows, n = x.shape
    return op(x.reshape(rows // SUBLANE, SUBLANE, n), pattern[None]).reshape(rows, n)


def _norm_mod_tm(x, g, sc, sh):
    scaled = _per_seq(x * _rstd(x), g * (1.0 + sc), jnp.multiply)
    return _per_seq(scaled, sh, jnp.add)


def _mod_kernel(c_ref, w_ref, b_ref, o_ref):
    c = c_ref[...]
    s = (c * jax.nn.sigmoid(c)).astype(BF16)
    o_ref[0] = jnp.dot(s, w_ref[0].astype(BF16), preferred_element_type=F32) + b_ref[0]


def _modulation(cond, w_mod, b_mod):
    depth, d, n = w_mod.shape
    rows = cond.shape[0]
    tn = _tile(n, 1024, LANE)
    return pl.pallas_call(
        _mod_kernel,
        grid=(depth, n // tn),
        in_specs=[pl.BlockSpec((rows, d), lambda l, j: (0, 0)),
                  pl.BlockSpec((1, d, tn), lambda l, j: (l, 0, j)),
                  pl.BlockSpec((1, 1, tn), lambda l, j: (l, 0, j))],
        out_specs=pl.BlockSpec((1, rows, tn), lambda l, j: (l, 0, j)),
        out_shape=jax.ShapeDtypeStruct((depth, rows, n), F32),
        compiler_params=_params("parallel", "parallel"),
        name="modulation",
    )(cond, w_mod, b_mod.reshape(depth, 1, n))


def _swap_quarters_in_lanes(x, head_dim):
    q4 = head_dim // 4
    parts = []
    for c in range(x.shape[1] // head_dim):
        base = c * head_dim
        parts += [x[:, base:base + q4], x[:, base + 2 * q4:base + 3 * q4],
                  x[:, base + q4:base + 2 * q4], x[:, base + 3 * q4:base + 4 * q4]]
    return jnp.concatenate(parts, axis=1)


def _cast_kernel(w_ref, o_ref, *, swap_tiles, head_dim):
    w = w_ref[0]
    if swap_tiles == 0:
        o_ref[...] = w.astype(BF16)
        return
    j = pl.program_id(1)

    @pl.when(j < swap_tiles)
    def _():
        o_ref[...] = _swap_quarters_in_lanes(w, head_dim).astype(BF16)

    @pl.when(j >= swap_tiles)
    def _():
        o_ref[...] = w.astype(BF16)


def _to_bf16(w_stack, index, swap_cols=0, head_dim=LANE):
    _, k, n = w_stack.shape
    tk = _tile(k, 1024, BF16_ROWS)
    tn = _tile(math.gcd(swap_cols, n), 2048, LANE)
    return pl.pallas_call(
        functools.partial(_cast_kernel, swap_tiles=swap_cols // tn, head_dim=head_dim),
        grid=(k // tk, n // tn),
        in_specs=[pl.BlockSpec((1, tk, tn), lambda i, j: (index, i, j))],
        out_specs=pl.BlockSpec((tk, tn), lambda i, j: (i, j)),
        out_shape=jax.ShapeDtypeStruct((k, n), BF16),
        compiler_params=_params("parallel", "parallel"),
        name="weight_to_bf16",
    )(w_stack)


def _qkv_kernel(*refs, rope, n_rope_tiles, tn, q_scale):
    if rope:
        x_ref, g_ref, sc_ref, sh_ref, w_ref, cos_ref, sin_ref, o_ref, xm_ref = refs
    else:
        x_ref, g_ref, sc_ref, sh_ref, w_ref, oq_ref, ok_ref, ov_ref, xm_ref = refs
    j = pl.program_id(1)

    @pl.when(j == 0)
    def _():
        xm_ref[...] = _norm_mod(x_ref[...], g_ref[...], sc_ref[0], sh_ref[0]).astype(BF16)

    acc = jnp.dot(xm_ref[...], w_ref[...], preferred_element_type=F32)
    if not rope:
        n_q = n_rope_tiles // 2

        @pl.when(j < n_q)
        def _():
            oq_ref[...] = (acc * q_scale).astype(oq_ref.dtype)

        @pl.when((j >= n_q) & (j < 2 * n_q))
        def _():
            ok_ref[...] = acc

        @pl.when(j >= 2 * n_q)
        def _():
            ov_ref[...] = acc

        return

    @pl.when(j < n_rope_tiles)
    def _():
        cos = cos_ref[0]
        sin = sin_ref[0]
        for c in range(tn // LANE):
            x = acc[:, c * LANE:(c + 1) * LANE]
            o_ref[:, c * LANE:(c + 1) * LANE] = (x * cos + pltpu.roll(x, LANE // 2, 1) * sin).astype(o_ref.dtype)

    @pl.when(j >= n_rope_tiles)
    def _():
        o_ref[...] = acc.astype(o_ref.dtype)


def _qkv_proj(x, g, sc, sh, w, seq_len, q_scale, rope_tables=None):
    m, d = x.shape
    n = w.shape[1]
    n_rope_cols = 2 * n // 3
    n_mod = sc.shape[0]
    tm = _tile(seq_len, 1024, BF16_ROWS) if rope_tables is not None else _row_tile(m, seq_len, n_mod, 1024, BF16_ROWS)
    tn = _tile(n_rope_cols // 2, 1024 if rope_tables is not None else 512, LANE)
    nq = n_rope_cols // (2 * tn)

    def mod_i(i):
        return (i * tm // seq_len) % n_mod

    in_specs = [pl.BlockSpec((tm, d), lambda i, j: (i, 0)),
                pl.BlockSpec((1, d), lambda i, j: (0, 0)),
                pl.BlockSpec((1, 1, d), lambda i, j: (mod_i(i), 0, 0)),
                pl.BlockSpec((1, 1, d), lambda i, j: (mod_i(i), 0, 0)),
                pl.BlockSpec((d, tn), lambda i, j: (0, j))]
    args = [x, g, sc, sh, w]
    if rope_tables is not None:
        tiles_per_seq = seq_len // tm
        in_specs += [pl.BlockSpec((1, tm, LANE),
                                  lambda i, j: (jnp.where(j < nq, 0, 1), i % tiles_per_seq, 0))] * 2
        args += list(rope_tables)
        out_specs = pl.BlockSpec((tm, tn), lambda i, j: (i, j))
        out_shape = jax.ShapeDtypeStruct((m, n), BF16)
    else:
        out_specs = [pl.BlockSpec((tm, tn), lambda i, j, part=part: (i, jnp.clip(j - part * nq, 0, nq - 1)))
                     for part in range(3)]
        out_shape = [jax.ShapeDtypeStruct((m, n // 3), dt) for dt in (BF16, F32, F32)]
    return pl.pallas_call(
        functools.partial(_qkv_kernel, rope=rope_tables is not None,
                          n_rope_tiles=n_rope_cols // tn, tn=tn, q_scale=q_scale),
        grid=(m // tm, n // tn),
        in_specs=in_specs,
        out_specs=out_specs,
        out_shape=out_shape,
        scratch_shapes=[pltpu.VMEM((tm, d), BF16)],
        compiler_params=_params("parallel", "arbitrary"),
        name="qkv_proj",
    )(*args)


def _swap_middle_quarters(a, head_dim):
    lead = a.shape[:-1]
    a = a.reshape(lead + (a.shape[-1] // head_dim, 2, 2, head_dim // 4))
    return jnp.swapaxes(a, -3, -2).reshape(lead + (-1,))


def _rope_tables(length, head_dim, q_scale):
    pos = jnp.arange(length)
    r = (pos // GRID_W).astype(F32)
    col = (pos % GRID_W).astype(F32)
    half = head_dim // 2
    inv = ROPE_THETA ** (-jnp.arange(0, half, 2, dtype=F32) / half)
    ar = r[:, None] * inv
    ac = col[:, None] * inv
    cos = jnp.concatenate([jnp.cos(ar), jnp.cos(ac)] * 2, axis=-1)
    sin = jnp.concatenate([-jnp.sin(ar), -jnp.sin(ac), jnp.sin(ar), jnp.sin(ac)], axis=-1)
    scales = jnp.array([q_scale, 1.0], F32)[:, None, None]
    return cos[None] * scales, sin[None] * scales


def _attn_kernel(*refs, has_cache, lam_init, head_dim, kv_chunk, heads):
    if has_cache:
        lam_ref, q_ref, k_ref, v_ref, ck_ref, cv_ref, g_ref, o_ref = refs
    else:
        lam_ref, q_ref, k_ref, v_ref, g_ref, o_ref = refs
    hw = 2 * head_dim
    lv = lam_ref[...]
    lam = (jnp.exp(jnp.sum(lv[0:1] * lv[1:2], axis=-1, keepdims=True))
           - jnp.exp(jnp.sum(lv[2:3] * lv[3:4], axis=-1, keepdims=True)) + lam_init)
    for hh in range(heads):
        cols = slice(hh * hw, (hh + 1) * hw)
        q = q_ref[:, cols]
        ks = [k_ref[:, cols].astype(BF16)]
        vs = [v_ref[:, cols].astype(BF16)]
        if has_cache:
            ks.append(_swap_quarters_in_lanes(ck_ref[0, 0, :, cols], head_dim).astype(BF16))
            vs.append(cv_ref[0, 0, :, cols].astype(BF16))
        kv = [(k[r:r + kv_chunk], v[r:r + kv_chunk]) for k, v in zip(ks, vs) for r in range(0, k.shape[0], kv_chunk)]
        outs = []
        for c in range(2):
            qc = q[:, c * head_dim:(c + 1) * head_dim]
            m = denom = acc = None
            for k, v in kv:
                s = lax.dot_general(qc, k[:, c * head_dim:(c + 1) * head_dim],
                                    (((1,), (1,)), ((), ())), preferred_element_type=F32)
                s_max = jnp.max(s, axis=-1, keepdims=True)
                m_new = s_max if m is None else jnp.maximum(m, s_max)
                p = jnp.exp2(s - m_new)
                p_sum = jnp.sum(p, axis=-1, keepdims=True)
                pv = jnp.dot(p.astype(BF16), v, preferred_element_type=F32)
                if m is None:
                    denom, acc = p_sum, pv
                else:
                    alpha = jnp.exp2(m - m_new)
                    denom, acc = denom * alpha + p_sum, acc * alpha + pv
                m = m_new
            outs.append(acc / denom)
        o = outs[0] - lam * outs[1]
        r = lax.rsqrt(jnp.mean(o * o, axis=-1, keepdims=True) + SUBLN_EPS)
        o_ref[:, cols] = (o * r * g_ref[...] * (1.0 - lam_init)).astype(o_ref.dtype)


def _diff_attention(qkv, lam_vecs, subln_g, lam_init, batch, seq_len, n_heads, head_dim,
                    cache_k=None, cache_v=None, slot=0):
    hw = 2 * head_dim
    tq = _tile(seq_len, 1024, BF16_ROWS)
    nq = seq_len // tq
    heads = _tile(n_heads, max(2048 // seq_len, 1), 1)
    bw = heads * hw
    ng = n_heads // heads
    has_cache = cache_k is not None
    if isinstance(qkv, tuple):
        q_arr, k_arr, v_arr = qkv
        k_off = v_off = 0
    else:
        q_arr = k_arr = v_arr = qkv
        k_off, v_off = ng, 2 * ng
    in_specs = [pl.BlockSpec(lam_vecs.shape, lambda b, h, i: (0, 0)),
                pl.BlockSpec((tq, bw), lambda b, h, i: (b * nq + i, h)),
                pl.BlockSpec((seq_len, bw), lambda b, h, i: (b, k_off + h)),
                pl.BlockSpec((seq_len, bw), lambda b, h, i: (b, v_off + h))]
    args = [lam_vecs, q_arr, k_arr, v_arr]
    if has_cache:
        past = cache_k.shape[2]
        in_specs += [pl.BlockSpec((1, 1, past, bw), lambda b, h, i: (b, slot, 0, h))] * 2
        args += [cache_k, cache_v]
    in_specs.append(pl.BlockSpec((1, hw), lambda b, h, i: (0, 0)))
    args.append(subln_g)
    return pl.pallas_call(
        functools.partial(_attn_kernel, has_cache=has_cache, lam_init=lam_init,
                          head_dim=head_dim, kv_chunk=_tile(seq_len, 512, BF16_ROWS), heads=heads),
        grid=(batch, ng, nq),
        in_specs=in_specs,
        out_specs=pl.BlockSpec((tq, bw), lambda b, h, i: (b * nq + i, h)),
        out_shape=jax.ShapeDtypeStruct((batch * seq_len, n_heads * hw), BF16),
        compiler_params=_params("parallel", "parallel", "arbitrary"),
        name="diff_attention",
    )(*args)


def _attn_out_kernel(a_ref, w_ref, x_ref, gate_ref, o_ref):
    nb, tl, k = a_ref.shape
    acc = jnp.dot(a_ref[...].reshape(nb * tl, k), w_ref[...], preferred_element_type=F32)
    res = x_ref[...] + gate_ref[...] * acc.reshape(nb, tl, -1)
    o_ref[...] = pltpu.einshape("btn->tbn", res).reshape(tl * nb, -1)


def _attn_out_proj(o, w, x, gate, batch, seq_len):
    k = o.shape[1]
    n = w.shape[1]
    tl = _tile(seq_len, 128, BF16_ROWS)
    tn = _tile(n, 1024, LANE)
    nl = seq_len // tl
    return pl.pallas_call(
        _attn_out_kernel,
        grid=(batch // SUBLANE, nl, n // tn),
        in_specs=[pl.BlockSpec((SUBLANE, tl, k), lambda s, t, j: (s, t, 0)),
                  pl.BlockSpec((k, tn), lambda s, t, j: (0, j)),
                  pl.BlockSpec((SUBLANE, tl, tn), lambda s, t, j: (s, t, j)),
                  pl.BlockSpec((SUBLANE, 1, tn), lambda s, t, j: (s, 0, j))],
        out_specs=pl.BlockSpec((tl * SUBLANE, tn), lambda s, t, j: (s * nl + t, j)),
        out_shape=jax.ShapeDtypeStruct((batch * seq_len, n), F32),
        compiler_params=_params("parallel", "parallel", "arbitrary"),
        name="attn_out_proj",
    )(o.reshape(batch, seq_len, k), w, x.reshape(batch, seq_len, n), gate)


def _to_batch_major_kernel(x_ref, o_ref):
    nb, tl, d = o_ref.shape
    o_ref[...] = pltpu.einshape("tbd->btd", x_ref[...].reshape(tl, nb, d))


def _to_batch_major(x, batch, seq_len):
    m, d = x.shape
    tl = _tile(seq_len, 64, SUBLANE)
    nl = seq_len // tl
    return pl.pallas_call(
        _to_batch_major_kernel,
        grid=(batch // SUBLANE, nl),
        in_specs=[pl.BlockSpec((tl * SUBLANE, d), lambda s, t: (s * nl + t, 0))],
        out_specs=pl.BlockSpec((SUBLANE, tl, d), lambda s, t: (s, t, 0)),
        out_shape=jax.ShapeDtypeStruct((batch, seq_len, d), F32),
        compiler_params=_params("parallel", "parallel"),
        name="to_batch_major",
    )(x).reshape(m, d)


def _mm_res_kernel(a_ref, w_ref, x_ref, gate_ref, o_ref):
    acc = jnp.dot(a_ref[...], w_ref[...], preferred_element_type=F32)
    o_ref[...] = x_ref[...] + _per_seq(acc, gate_ref[0], jnp.multiply)


def _matmul_gate_residual(a, w, x, gate, seq_len):
    m, k = a.shape
    n = w.shape[1]
    group_rows = seq_len * SUBLANE
    tm = _tile(group_rows, 1024, BF16_ROWS)
    tn = _tile(n, 512, LANE)
    return pl.pallas_call(
        _mm_res_kernel,
        grid=(m // tm, n // tn),
        in_specs=[pl.BlockSpec((tm, k), lambda i, j: (i, 0)),
                  pl.BlockSpec((k, tn), lambda i, j: (0, j)),
                  pl.BlockSpec((tm, tn), lambda i, j: (i, j)),
                  pl.BlockSpec((1, SUBLANE, tn), lambda i, j: (i * tm // group_rows, 0, j))],
        out_specs=pl.BlockSpec((tm, tn), lambda i, j: (i, j)),
        out_shape=jax.ShapeDtypeStruct((m, n), F32),
        compiler_params=_params("parallel", "arbitrary"),
        name="matmul_gate_residual",
    )(a, w, x, gate)


def _up_kernel(x_ref, xp_ref, xn_ref, g_ref, sc_ref, sh_ref, wg_ref, wv_ref,
               cwg_ref, cwv_ref, cbg_ref, cbv_ref, o_ref, xm_ref, *, tm, group_rows, chunk):
    halo = BF16_ROWS
    i = pl.program_id(0)
    j = pl.program_id(1)

    @pl.when(j == 0)
    def _():
        g, sc, sh = g_ref[...], sc_ref[0], sh_ref[0]
        xm_ref[0:halo] = _norm_mod_tm(xp_ref[...], g, sc, sh).astype(BF16)
        xm_ref[halo:halo + tm] = _norm_mod_tm(x_ref[...], g, sc, sh).astype(BF16)
        xm_ref[halo + tm:] = _norm_mod_tm(xn_ref[...], g, sc, sh).astype(BF16)

        @pl.when((i * tm) % group_rows == 0)
        def _():
            xm_ref[0:halo] = jnp.zeros((halo, xm_ref.shape[1]), BF16)

        @pl.when(((i + 1) * tm) % group_rows == 0)
        def _():
            xm_ref[halo + tm:] = jnp.zeros((halo, xm_ref.shape[1]), BF16)

    xm = xm_ref[...]

    def conv(h, cw_ref, cb_ref, cols):
        cw = cw_ref[:, cols]
        lo = halo - SUBLANE
        return (h[lo:lo + tm] * cw[0:1] + h[halo:halo + tm] * cw[1:2]
                + h[halo + SUBLANE:halo + SUBLANE + tm] * cw[2:3] + cb_ref[:, cols])

    for c in range(o_ref.shape[1] // chunk):
        cols = slice(c * chunk, (c + 1) * chunk)
        gate = conv(jnp.dot(xm, wg_ref[:, cols], preferred_element_type=F32), cwg_ref, cbg_ref, cols)
        val = conv(jnp.dot(xm, wv_ref[:, cols], preferred_element_type=F32), cwv_ref, cbv_ref, cols)
        o_ref[:, cols] = (gate * jax.nn.sigmoid(gate) * val).astype(o_ref.dtype)


def _conv_ffn_up(x, g, sc, sh, w_up, conv_w, conv_b, seq_len):
    m, d = x.shape
    d_ff = w_up.shape[1] // 2
    halo = BF16_ROWS
    group_rows = seq_len * SUBLANE
    tm = _tile(group_rows, 1024, halo)
    tn = _tile(d_ff, 512, LANE)
    chunk = _tile(tn, MXU_DIM, LANE)
    nj = d_ff // tn
    hb = tm // halo
    last_hb = m // halo - 1

    mod_spec = pl.BlockSpec((1, SUBLANE, d), lambda i, j: (i * tm // group_rows, 0, 0))
    return pl.pallas_call(
        functools.partial(_up_kernel, tm=tm, group_rows=group_rows, chunk=chunk),
        grid=(m // tm, nj),
        in_specs=[pl.BlockSpec((tm, d), lambda i, j: (i, 0)),
                  pl.BlockSpec((halo, d), lambda i, j: (jnp.maximum(i * hb - 1, 0), 0)),
                  pl.BlockSpec((halo, d), lambda i, j: (jnp.minimum((i + 1) * hb, last_hb), 0)),
                  pl.BlockSpec((1, d), lambda i, j: (0, 0)),
                  mod_spec, mod_spec,
                  pl.BlockSpec((d, tn), lambda i, j: (0, j)),
                  pl.BlockSpec((d, tn), lambda i, j: (0, nj + j)),
                  pl.BlockSpec((3, tn), lambda i, j: (0, j)),
                  pl.BlockSpec((3, tn), lambda i, j: (0, nj + j)),
                  pl.BlockSpec((1, tn), lambda i, j: (0, j)),
                  pl.BlockSpec((1, tn), lambda i, j: (0, nj + j))],
        out_specs=pl.BlockSpec((tm, tn), lambda i, j: (i, j)),
        out_shape=jax.ShapeDtypeStruct((m, d_ff), BF16),
        scratch_shapes=[pltpu.VMEM((tm + 2 * halo, d), BF16)],
        compiler_params=_params("parallel", "arbitrary"),
        name="conv_ffn_up",
    )(x, x, x, g, sc, sh, w_up, w_up, conv_w, conv_w, conv_b, conv_b)


def _discretize(a_re, a_im, log_step):
    dt = jnp.exp(log_step)
    mag = jnp.exp(a_re * dt)
    lb_re = mag * jnp.cos(a_im * dt)
    lb_im = mag * jnp.sin(a_im * dt)
    num_re = lb_re - 1.0
    den = a_re * a_re + a_im * a_im
    f_re = (num_re * a_re + lb_im * a_im) / den
    f_im = (lb_im * a_re - num_re * a_im) / den
    return lb_re, lb_im, f_re, f_im


def _s5_disc_kernel(a_re_ref, a_im_ref, ls_ref, ax_re_ref, ax_im_ref, lsx_ref, b_re_ref, b_im_ref,
                    lb_re_ref, lb_im_ref, bb_re_ref, bb_im_ref):
    lb_re, lb_im, _, _ = _discretize(a_re_ref[...], a_im_ref[...], ls_ref[...])
    lb_re_ref[...] = lb_re
    lb_im_ref[...] = lb_im
    _, _, f_re, f_im = _discretize(ax_re_ref[...], ax_im_ref[...], lsx_ref[...])
    b_re, b_im = b_re_ref[...], b_im_ref[...]
    bb_re_ref[...] = f_re * b_re - f_im * b_im
    bb_im_ref[...] = f_re * b_im + f_im * b_re


def _s5_discretize(a_re, a_im, log_step, b_re, b_im):
    two, g, p, c = b_re.shape
    rows = two * g
    a2 = lambda a: a.reshape(rows, p)
    ax = lambda a: jnp.repeat(a.reshape(rows, p), c, axis=-1)
    ls = jnp.broadcast_to(log_step.reshape(rows, 1), (rows, p))
    lsx = jnp.broadcast_to(log_step.reshape(rows, 1), (rows, p * c))
    return pl.pallas_call(
        _s5_disc_kernel,
        out_shape=[jax.ShapeDtypeStruct((rows, p), F32)] * 2 + [jax.ShapeDtypeStruct((rows, p * c), F32)] * 2,
        compiler_params=pltpu.CompilerParams(vmem_limit_bytes=VMEM_LIMIT_BYTES),
        name="s5_discretize",
    )(a2(a_re), a2(a_im), ls, ax(a_re), ax(a_im), lsx, b_re.reshape(rows, p * c), b_im.reshape(rows, p * c))


def _s5_in_kernel(x_ref, g_ref, sc_ref, sh_ref, o_ref):
    o_ref[...] = _norm_mod_tm(x_ref[...], g_ref[...], sc_ref[0], sh_ref[0]).astype(o_ref.dtype)


def _s5_input(x, g, sc, sh, seq_len):
    m, d = x.shape
    group_rows = seq_len * SUBLANE
    tm = _tile(group_rows, 1024, BF16_ROWS)
    mod_spec = pl.BlockSpec((1, SUBLANE, d), lambda i: (i * tm // group_rows, 0, 0))
    return pl.pallas_call(
        _s5_in_kernel,
        grid=(m // tm,),
        in_specs=[pl.BlockSpec((tm, d), lambda i: (i, 0)), pl.BlockSpec((1, d), lambda i: (0, 0)),
                  mod_spec, mod_spec],
        out_specs=pl.BlockSpec((tm, d), lambda i: (i, 0)),
        out_shape=jax.ShapeDtypeStruct((m, d), BF16),
        compiler_params=_params("parallel"),
        name="s5_input",
    )(x, g, sc, sh)


def _s5_scan_kernel(u_ref, wb_ref, wc_ref, lam_ref, h0_ref, y_ref, fin_ref, st_ref, buf_ref,
                    *, nsub, nt, sub_t):
    direction = pl.program_id(1)
    t_idx = pl.program_id(3)
    n_tiles = st_ref.shape[0]
    half = n_tiles // 2
    sr = sub_t * SUBLANE
    steps = sub_t // n_tiles

    @pl.when(t_idx == 0)
    def _():
        st_ref[...] = h0_ref[0, 0, 0]

    lam = jnp.broadcast_to(lam_ref[0, 0], st_ref.shape)
    a_re, a_im = lam[:half], lam[half:]

    def first_row(p, d):
        row = (p + d * (nsub - 1 - 2 * p)) * sr
        return row if isinstance(row, int) else pl.multiple_of(row, sr)

    def run_pass(d, p, state, do_bu, do_scan, do_ch, phase=None):
        phase = p % 3 if phase is None else phase
        slot_bu, slot_scan, slot_ch = phase, (phase + 2) % 3, (phase + 1) % 3
        if do_ch:
            h_all = jnp.concatenate([buf_ref[slot_ch, i] for i in range(n_tiles)], axis=1)
            y_ref[0, pl.ds(first_row(p - 2, d), sr), :] = jnp.dot(h_all.astype(BF16), wc_ref[0, 0],
                                                                  preferred_element_type=F32)
        if do_bu:
            u_sub = u_ref[pl.ds(first_row(p, d), sr), :]
        h_re, h_im = state
        for i in range(n_tiles):
            if do_bu:
                buf_ref[slot_bu, i] = jnp.dot(u_sub, wb_ref[0, 0, i], preferred_element_type=F32)
            if do_scan:
                for k in range(steps):
                    t = i * steps + k
                    row = (t + d * (sub_t - 1 - 2 * t)) * SUBLANE
                    bu = buf_ref[slot_scan, :, pl.ds(row, SUBLANE), :]
                    n_re = a_re * h_re - a_im * h_im + bu[:half]
                    n_im = a_re * h_im + a_im * h_re + bu[half:]
                    buf_ref[slot_scan, 0:half, pl.ds(row, SUBLANE), :] = n_re
                    buf_ref[slot_scan, half:, pl.ds(row, SUBLANE), :] = n_im
                    h_re, h_im = n_re, n_im
        return h_re, h_im

    def scan_block(d):
        state = (st_ref[0:half], st_ref[half:])
        for p in range(min(2, nsub + 2)):
            state = run_pass(d, p, state, p < nsub, 1 <= p <= nsub, False)
        n_trips = max(nsub - 2, 0) // 3

        def three_passes(trip, st):
            for k in range(3):
                st = run_pass(d, 2 + 3 * trip + k, st, True, True, True, phase=(2 + k) % 3)
            return st

        if n_trips > 0:
            state = lax.fori_loop(0, n_trips, three_passes, state)
        for p in range(2 + 3 * n_trips, nsub):
            state = run_pass(d, p, state, True, True, True)
        for p in range(max(2, nsub), nsub + 2):
            state = run_pass(d, p, state, p < nsub, p <= nsub, True)
        st_ref[0:half] = state[0]
        st_ref[half:] = state[1]

    for static_d in (0, 1):
        pl.when(direction == static_d)(functools.partial(scan_block, static_d))

    @pl.when(t_idx == nt - 1)
    def _():
        fin_ref[0, 0, 0] = st_ref[...]


def _s5_scan(u, wb, wc, lam, h0, seq_len):
    rows, d = u.shape
    ns = rows // (seq_len * SUBLANE)
    ngb = d // MXU_DIM
    n_tiles = wb.shape[2]
    sub_t = _tile(seq_len, 64, n_tiles)
    tc = _tile(seq_len, 16 * sub_t, sub_t)
    nt = seq_len // tc
    assert sub_t % n_tiles == 0 and tc % sub_t == 0

    def t_blk(n, dd, t):
        return n * nt + t + dd * (nt - 1 - 2 * t)

    w_spec = pl.BlockSpec((1, 1, n_tiles, MXU_DIM, MXU_DIM), lambda n, dd, gb, t: (dd, gb, 0, 0, 0))
    st_spec = pl.BlockSpec((1, 1, 1, n_tiles, SUBLANE, MXU_DIM), lambda n, dd, gb, t: (n, dd, gb, 0, 0, 0))
    return pl.pallas_call(
        functools.partial(_s5_scan_kernel, nsub=tc // sub_t, nt=nt, sub_t=sub_t),
        grid=(ns, 2, ngb, nt),
        in_specs=[pl.BlockSpec((tc * SUBLANE, MXU_DIM), lambda n, dd, gb, t: (t_blk(n, dd, t), gb)),
                  w_spec,
                  pl.BlockSpec((1, 1, n_tiles * MXU_DIM, MXU_DIM), lambda n, dd, gb, t: (dd, gb, 0, 0)),
                  pl.BlockSpec((1, 1, n_tiles, 1, MXU_DIM), lambda n, dd, gb, t: (dd, gb, 0, 0, 0)),
                  st_spec],
        out_specs=[pl.BlockSpec((1, tc * SUBLANE, MXU_DIM), lambda n, dd, gb, t: (dd, t_blk(n, dd, t), gb)),
                   st_spec],
        out_shape=[jax.ShapeDtypeStruct((2, rows, d), F32),
                   jax.ShapeDtypeStruct((ns, 2, ngb, n_tiles, SUBLANE, MXU_DIM), F32)],
        scratch_shapes=[pltpu.VMEM((n_tiles, SUBLANE, MXU_DIM), F32),
                        pltpu.VMEM((3, n_tiles, sub_t * SUBLANE, MXU_DIM), F32)],
        compiler_params=_params("parallel", "parallel", "parallel", "arbitrary"),
        name="s5_scan",
    )(u, wb, wc, lam, h0)


def _s5_block_weights(lb_re, lb_im, bb_re, bb_im, c_re, c_im, n_groups, state_dim, group):
    gpb = MXU_DIM // group
    ngb = n_groups // gpb
    eye = jnp.eye(gpb, dtype=F32)

    def b_blocks(bb):
        bb = bb.reshape(2, ngb, gpb, state_dim, group)
        return jnp.einsum('dbgpc,gh->dbgchp', bb, eye).reshape(2, ngb, gpb * group, gpb * state_dim)

    def c_blocks(cc):
        cc = cc.reshape(2, ngb, gpb, group, state_dim)
        return jnp.einsum('dbgcp,gh->dbgphc', cc, eye).reshape(2, ngb, gpb * state_dim, gpb * group)

    wb = jnp.concatenate([b_blocks(bb_re), b_blocks(bb_im)], axis=-1).astype(BF16)
    wc = jnp.concatenate([c_blocks(c_re), -c_blocks(c_im)], axis=-2).astype(BF16)
    lam = jnp.concatenate([lb_re.reshape(2, ngb, 1, gpb * state_dim),
                           lb_im.reshape(2, ngb, 1, gpb * state_dim)], axis=-1)
    n_tiles = wb.shape[-1] // MXU_DIM
    wb = wb.reshape(2, ngb, MXU_DIM, n_tiles, MXU_DIM).transpose(0, 1, 3, 2, 4)
    lam = lam.reshape(2, ngb, n_tiles, 1, MXU_DIM)
    return wb, wc, lam


def _s5_state_blocks(s_re, s_im, ngb):
    b, two, g, p = s_re.shape
    sw = g * p // ngb

    def blk(s):
        return s.reshape(b // SUBLANE, SUBLANE, two, ngb, sw).transpose(0, 2, 3, 1, 4)

    st = jnp.concatenate([blk(s_re), blk(s_im)], axis=-1)
    st = st.reshape(st.shape[:4] + (2 * sw // MXU_DIM, MXU_DIM))
    return st.transpose(0, 1, 2, 4, 3, 5)


def _s5_state_unblock(fin, g, p):
    ns, two, ngb, n_tiles, sub, lanes = fin.shape
    fin = fin.transpose(0, 1, 2, 4, 3, 5).reshape(ns, two, ngb, sub, n_tiles * lanes)
    sw = n_tiles * lanes // 2

    def unblk(s):
        return s.transpose(0, 3, 1, 2, 4).reshape(ns * sub, two, g, p)

    return unblk(fin[..., :sw]), unblk(fin[..., sw:])


def _glu_kernel(x_ref, yf_ref, yb_ref, g_ref, sc_ref, sh_ref, dskip_ref, w_ref, b_ref, gate_ref, o_ref):
    x = x_ref[...]
    xm = _norm_mod_tm(x, g_ref[...], sc_ref[0], sh_ref[0])
    gz = jax.nn.gelu(yf_ref[0] + yb_ref[0] + dskip_ref[...] * xm)
    acc = jnp.dot(gz.astype(BF16), w_ref[...], preferred_element_type=F32) + b_ref[...]
    o_ref[...] = x + _per_seq(gz * jax.nn.sigmoid(acc), gate_ref[0], jnp.multiply)


def _s5_glu(x, y, g, sc, sh, d_skip, w_glu, b_glu, gate, seq_len):
    m, d = x.shape
    group_rows = seq_len * SUBLANE
    tm = _tile(group_rows, 256, BF16_ROWS)
    mod_spec = pl.BlockSpec((1, SUBLANE, d), lambda i: (i * tm // group_rows, 0, 0))
    vec_spec = pl.BlockSpec((1, d), lambda i: (0, 0))
    return pl.pallas_call(
        _glu_kernel,
        grid=(m // tm,),
        in_specs=[pl.BlockSpec((tm, d), lambda i: (i, 0)),
                  pl.BlockSpec((1, tm, d), lambda i: (0, i, 0)),
                  pl.BlockSpec((1, tm, d), lambda i: (1, i, 0)),
                  vec_spec, mod_spec, mod_spec, vec_spec,
                  pl.BlockSpec((d, d), lambda i: (0, 0)),
                  vec_spec, mod_spec],
        out_specs=pl.BlockSpec((tm, d), lambda i: (i, 0)),
        out_shape=jax.ShapeDtypeStruct((m, d), F32),
        compiler_params=_params("parallel"),
        name="s5_glu",
    )(x, y, y, g, sc, sh, d_skip, w_glu, b_glu, gate)


def _final_norm_kernel(x_ref, g_ref, o_ref):
    nb, tl, d = o_ref.shape
    x = x_ref[...]
    y = x * _rstd(x) * g_ref[...]
    o_ref[...] = pltpu.einshape("tbd->btd", y.reshape(tl, nb, d))


def _final_norm(x, g, batch, seq_len):
    m, d = x.shape
    tl = _tile(seq_len, 64, SUBLANE)
    nl = seq_len // tl
    return pl.pallas_call(
        _final_norm_kernel,
        grid=(batch // SUBLANE, nl),
        in_specs=[pl.BlockSpec((tl * SUBLANE, d), lambda s, t: (s * nl + t, 0)),
                  pl.BlockSpec((1, d), lambda s, t: (0, 0))],
        out_specs=pl.BlockSpec((SUBLANE, tl, d), lambda s, t: (s, t, 0)),
        out_shape=jax.ShapeDtypeStruct((batch, seq_len, d), F32),
        compiler_params=_params("parallel", "parallel"),
        name="final_norm",
    )(x, g)


def kernel(x_prompt, x_sample, cache_k, cache_v, state_re, state_im, c, c_ctx, w_mod, b_mod, norm_g, w_qkv, lam_vecs, subln_g, w_o, ssm_a_re, ssm_a_im, ssm_log_step, ssm_b_re, ssm_b_im, ssm_c_re, ssm_c_im, ssm_d, w_glu, b_glu, w_up, conv_w, conv_b, w_down, final_g):
    bp, lp, d = x_prompt.shape
    bs, ls, _ = x_sample.shape
    depth = w_mod.shape[0]
    n_heads, head_dim = cache_k.shape[3], cache_k.shape[5]
    n_groups, state_dim, group = ssm_b_re.shape[2], ssm_b_re.shape[3], ssm_b_re.shape[4]
    assert bp % SUBLANE == 0 and bs % SUBLANE == 0 and d % MXU_DIM == 0 and MXU_DIM % group == 0
    assert 2 * head_dim == MXU_DIM and ls % GRID_W == 0

    pad_rows = -(bs + 1) % SUBLANE
    cond = jnp.concatenate([c, c_ctx[None], jnp.zeros((pad_rows, d), F32)], axis=0)
    mod = _modulation(cond, w_mod, b_mod).reshape(depth, cond.shape[0], 6, d)

    streams = [dict(x=x_prompt.reshape(bp * lp, d), batch=bp, seq=lp, time_major=False),
               dict(x=x_sample.reshape(bs * ls, d), batch=bs, seq=ls, time_major=False)]
    new_k, new_v, new_sre, new_sim = [], [], [], []
    q_scale = head_dim ** -0.5 * math.log2(math.e)
    rope = _rope_tables(ls, head_dim, q_scale)
    ngb = d // MXU_DIM

    for i in range(depth):
        mix, slot = i % N_MIXERS, i // N_MIXERS
        mods = [[jnp.broadcast_to(mod[i, bs, j], (bp, d)) for j in range(6)],
                [mod[i, :bs, j] for j in range(6)]]
        g1 = norm_g[i, 0][None]
        g2 = norm_g[i, 1][None]

        def per_group(v):
            return v.reshape(v.shape[0] // SUBLANE, SUBLANE, d)

        if mix == 0:
            lam_init = 0.8 - 0.6 * math.exp(-0.3 * i)
            nqk = n_heads * 2 * head_dim
            wq = _to_bf16(w_qkv, slot, swap_cols=2 * nqk, head_dim=head_dim)
            wo = _to_bf16(w_o, slot)
            sub_g = subln_g[slot][None]
            ck = cache_k.reshape(cache_k.shape[:3] + (-1,))
            cv = cache_v.reshape(cache_v.shape[:3] + (-1,))
            for s, (st, md) in enumerate(zip(streams, mods)):
                b, l = st['batch'], st['seq']
                if st['time_major']:
                    st['x'] = _to_batch_major(st['x'], b, l)
                sh1, sc1 = (v[:1, None] if s == 0 else v[:, None] for v in (md[0], md[1]))
                if s == 0:
                    q_p, k_p, v_p = _qkv_proj(st['x'], g1, sc1, sh1, wq, l, q_scale)
                    new_k.append(_swap_middle_quarters(k_p, head_dim).reshape(b, l, n_heads, 2, head_dim))
                    new_v.append(v_p.reshape(b, l, n_heads, 2 * head_dim))
                    o = _diff_attention((q_p, k_p, v_p), lam_vecs[slot], sub_g, lam_init, b, l, n_heads, head_dim)
                else:
                    qkv = _qkv_proj(st['x'], g1, sc1, sh1, wq, l, q_scale, rope_tables=rope)
                    o = _diff_attention(qkv, lam_vecs[slot], sub_g, lam_init, b, l, n_heads, head_dim,
                                        cache_k=ck, cache_v=cv, slot=slot)
                st['x'] = _attn_out_proj(o, wo, st['x'], md[2][:, None], b, l)
                st['time_major'] = True
        else:
            lb_re, lb_im, bb_re, bb_im = _s5_discretize(ssm_a_re[slot], ssm_a_im[slot], ssm_log_step[slot],
                                                        ssm_b_re[slot], ssm_b_im[slot])
            wb, wc, lam = _s5_block_weights(lb_re, lb_im, bb_re, bb_im, ssm_c_re[slot], ssm_c_im[slot],
                                            n_groups, state_dim, group)
            wg = _to_bf16(w_glu, slot)
            for s, (st, md) in enumerate(zip(streams, mods)):
                b, l = st['batch'], st['seq']
                sh1, sc1, gt1 = per_group(md[0]), per_group(md[1]), per_group(md[2])
                u = _s5_input(st['x'], g1, sc1, sh1, l)
                if s == 0:
                    h0 = jnp.zeros((b // SUBLANE, 2, ngb, wb.shape[2], SUBLANE, MXU_DIM), F32)
                else:
                    h0 = _s5_state_blocks(state_re[:, slot], state_im[:, slot], ngb)
                y, fin = _s5_scan(u, wb, wc, lam, h0, l)
                if s == 0:
                    f_re, f_im = _s5_state_unblock(fin, n_groups, state_dim)
                    new_sre.append(f_re)
                    new_sim.append(f_im)
                st['x'] = _s5_glu(st['x'], y, g1, sc1, sh1, ssm_d[slot][None], wg, b_glu[slot][None], gt1, l)
        wu = _to_bf16(w_up, i)
        wd = _to_bf16(w_down, i)
        for st, md in zip(streams, mods):
            sh2, sc2, gt2 = per_group(md[3]), per_group(md[4]), per_group(md[5])
            act = _conv_ffn_up(st['x'], g2, sc2, sh2, wu, conv_w[i], conv_b[i][None], st['seq'])
            st['x'] = _matmul_gate_residual(act, wd, st['x'], gt2, st['seq'])

    y_prompt = _final_norm(streams[0]['x'], final_g[None], bp, lp)
    y_sample = _final_norm(streams[1]['x'], final_g[None], bs, ls)
    return (y_prompt, y_sample, jnp.stack(new_k, axis=1), jnp.stack(new_v, axis=1),
            jnp.stack(new_sre, axis=1), jnp.stack(new_sim, axis=1))
```

```python
import functools
import math

import jax
import jax.numpy as jnp
from jax import lax
from jax.experimental import pallas as pl
from jax.experimental.pallas import tpu as pltpu

F32 = jnp.float32
BF16 = jnp.bfloat16

GRID_W = 64
ROPE_THETA = 10000.0
NORM_EPS = 1e-6
SUBLN_EPS = 1e-5
N_MIXERS = 2

LANE = 128
SUBLANE = 8
BF16_ROWS = 16
MXU_DIM = 256
VMEM_LIMIT_BYTES = 52 * 1024 * 1024


def _tile(n, pref, align):
    t = min(pref, n)
    t -= t % align
    while t >= align:
        if n % t == 0:
            return t
        t -= align
    return n


def _row_tile(m, seq_len, n_mod, pref, align):
    return _tile(m, pref, align) if n_mod == 1 else _tile(seq_len, pref, align)


def _params(*sem):
    return pltpu.CompilerParams(dimension_semantics=sem, vmem_limit_bytes=VMEM_LIMIT_BYTES)


def _rstd(x):
    return lax.rsqrt(jnp.mean(x * x, axis=-1, keepdims=True) + NORM_EPS)


def _norm_mod(x, g, sc, sh):
    return x * _rstd(x) * (g * (1.0 + sc)) + sh


def _per_seq(x, pattern, op):
    rows, n = x.shape
    return op(x.reshape(rows // SUBLANE, SUBLANE, n), pattern[None]).reshape(rows, n)


def _norm_mod_tm(x, g, sc, sh):
    scaled = _per_seq(x * _rstd(x), g * (1.0 + sc), jnp.multiply)
    return _per_seq(scaled, sh, jnp.add)


def _mod_kernel(c_ref, w_ref, b_ref, o_ref):
    c = c_ref[...]
    s = (c * jax.nn.sigmoid(c)).astype(BF16)
    o_ref[0] = jnp.dot(s, w_ref[0].astype(BF16), preferred_element_type=F32) + b_ref[0]


def _modulation(cond, w_mod, b_mod):
    depth, d, n = w_mod.shape
    rows = cond.shape[0]
    tn = _tile(n, 1024, LANE)
    return pl.pallas_call(
        _mod_kernel,
        grid=(depth, n // tn),
        in_specs=[pl.BlockSpec((rows, d), lambda l, j: (0, 0)),
                  pl.BlockSpec((1, d, tn), lambda l, j: (l, 0, j)),
                  pl.BlockSpec((1, 1, tn), lambda l, j: (l, 0, j))],
        out_specs=pl.BlockSpec((1, rows, tn), lambda l, j: (l, 0, j)),
        out_shape=jax.ShapeDtypeStruct((depth, rows, n), F32),
        compiler_params=_params("parallel", "parallel"),
        name="modulation",
    )(cond, w_mod, b_mod.reshape(depth, 1, n))


def _swap_quarters_in_lanes(x, head_dim):
    q4 = head_dim // 4
    parts = []
    for c in range(x.shape[1] // head_dim):
        base = c * head_dim
        parts += [x[:, base:base + q4], x[:, base + 2 * q4:base + 3 * q4],
                  x[:, base + q4:base + 2 * q4], x[:, base + 3 * q4:base + 4 * q4]]
    return jnp.concatenate(parts, axis=1)


def _cast_kernel(w_ref, o_ref, *, swap_tiles, head_dim):
    w = w_ref[0]
    if swap_tiles == 0:
        o_ref[...] = w.astype(BF16)
        return
    j = pl.program_id(1)

    @pl.when(j < swap_tiles)
    def _():
        o_ref[...] = _swap_quarters_in_lanes(w, head_dim).astype(BF16)

    @pl.when(j >= swap_tiles)
    def _():
        o_ref[...] = w.astype(BF16)


def _to_bf16(w_stack, index, swap_cols=0, head_dim=LANE):
    _, k, n = w_stack.shape
    tk = _tile(k, 1024, BF16_ROWS)
    tn = _tile(math.gcd(swap_cols, n), 2048, LANE)
    return pl.pallas_call(
        functools.partial(_cast_kernel, swap_tiles=swap_cols // tn, head_dim=head_dim),
        grid=(k // tk, n // tn),
        in_specs=[pl.BlockSpec((1, tk, tn), lambda i, j: (index, i, j))],
        out_specs=pl.BlockSpec((tk, tn), lambda i, j: (i, j)),
        out_shape=jax.ShapeDtypeStruct((k, n), BF16),
        compiler_params=_params("parallel", "parallel"),
        name="weight_to_bf16",
    )(w_stack)


def _qkv_kernel(*refs, rope, n_rope_tiles, tn, q_scale):
    if rope:
        x_ref, g_ref, sc_ref, sh_ref, w_ref, cos_ref, sin_ref, o_ref, xm_ref = refs
    else:
        x_ref, g_ref, sc_ref, sh_ref, w_ref, oq_ref, ok_ref, ov_ref, xm_ref = refs
    j = pl.program_id(1)

    @pl.when(j == 0)
    def _():
        xm_ref[...] = _norm_mod(x_ref[...], g_ref[...], sc_ref[0], sh_ref[0]).astype(BF16)

    acc = jnp.dot(xm_ref[...], w_ref[...], preferred_element_type=F32)
    if not rope:
        n_q = n_rope_tiles // 2

        @pl.when(j < n_q)
        def _():
            oq_ref[...] = (acc * q_scale).astype(oq_ref.dtype)

        @pl.when((j >= n_q) & (j < 2 * n_q))
        def _():
            ok_ref[...] = acc

        @pl.when(j >= 2 * n_q)
        def _():
            ov_ref[...] = acc

        return

    @pl.when(j < n_rope_tiles)
    def _():
        cos = cos_ref[0]
        sin = sin_ref[0]
        for c in range(tn // LANE):
            x = acc[:, c * LANE:(c + 1) * LANE]
            o_ref[:, c * LANE:(c + 1) * LANE] = (x * cos + pltpu.roll(x, LANE // 2, 1) * sin).astype(o_ref.dtype)

    @pl.when(j >= n_rope_tiles)
    def _():
        o_ref[...] = acc.astype(o_ref.dtype)


def _qkv_proj(x, g, sc, sh, w, seq_len, q_scale, rope_tables=None):
    m, d = x.shape
    n = w.shape[1]
    n_rope_cols = 2 * n // 3
    n_mod = sc.shape[0]
    tm = _tile(seq_len, 1024, BF16_ROWS) if rope_tables is not None else _row_tile(m, seq_len, n_mod, 1024, BF16_ROWS)
    tn = _tile(n_rope_cols // 2, 1024 if rope_tables is not None else 512, LANE)
    nq = n_rope_cols // (2 * tn)

    def mod_i(i):
        return (i * tm // seq_len) % n_mod

    in_specs = [pl.BlockSpec((tm, d), lambda i, j: (i, 0)),
                pl.BlockSpec((1, d), lambda i, j: (0, 0)),
                pl.BlockSpec((1, 1, d), lambda i, j: (mod_i(i), 0, 0)),
                pl.BlockSpec((1, 1, d), lambda i, j: (mod_i(i), 0, 0)),
                pl.BlockSpec((d, tn), lambda i, j: (0, j))]
    args = [x, g, sc, sh, w]
    if rope_tables is not None:
        tiles_per_seq = seq_len // tm
        in_specs += [pl.BlockSpec((1, tm, LANE),
                                  lambda i, j: (jnp.where(j < nq, 0, 1), i % tiles_per_seq, 0))] * 2
        args += list(rope_tables)
        out_specs = pl.BlockSpec((tm, tn), lambda i, j: (i, j))
        out_shape = jax.ShapeDtypeStruct((m, n), BF16)
    else:
        out_specs = [pl.BlockSpec((tm, tn), lambda i, j, part=part: (i, jnp.clip(j - part * nq, 0, nq - 1)))
                     for part in range(3)]
        out_shape = [jax.ShapeDtypeStruct((m, n // 3), dt) for dt in (BF16, F32, F32)]
    return pl.pallas_call(
        functools.partial(_qkv_kernel, rope=rope_tables is not None,
                          n_rope_tiles=n_rope_cols // tn, tn=tn, q_scale=q_scale),
        grid=(m // tm, n // tn),
        in_specs=in_specs,
        out_specs=out_specs,
        out_shape=out_shape,
        scratch_shapes=[pltpu.VMEM((tm, d), BF16)],
        compiler_params=_params("parallel", "arbitrary"),
        name="qkv_proj",
    )(*args)


def _swap_middle_quarters(a, head_dim):
    lead = a.shape[:-1]
    a = a.reshape(lead + (a.shape[-1] // head_dim, 2, 2, head_dim // 4))
    return jnp.swapaxes(a, -3, -2).reshape(lead + (-1,))


def _rope_tables(length, head_dim, q_scale):
    pos = jnp.arange(length)
    r = (pos // GRID_W).astype(F32)
    col = (pos % GRID_W).astype(F32)
    half = head_dim // 2
    inv = ROPE_THETA ** (-jnp.arange(0, half, 2, dtype=F32) / half)
    ar = r[:, None] * inv
    ac = col[:, None] * inv
    cos = jnp.concatenate([jnp.cos(ar), jnp.cos(ac)] * 2, axis=-1)
    sin = jnp.concatenate([-jnp.sin(ar), -jnp.sin(ac), jnp.sin(ar), jnp.sin(ac)], axis=-1)
    scales = jnp.array([q_scale, 1.0], F32)[:, None, None]
    return cos[None] * scales, sin[None] * scales


def _attn_kernel(*refs, has_cache, lam_init, head_dim, kv_chunk, heads):
    if has_cache:
        lam_ref, q_ref, k_ref, v_ref, ck_ref, cv_ref, g_ref, o_ref = refs
    else:
        lam_ref, q_ref, k_ref, v_ref, g_ref, o_ref = refs
    hw = 2 * head_dim
    lv = lam_ref[...]
    lam = (jnp.exp(jnp.sum(lv[0:1] * lv[1:2], axis=-1, keepdims=True))
           - jnp.exp(jnp.sum(lv[2:3] * lv[3:4], axis=-1, keepdims=True)) + lam_init)
    for hh in range(heads):
        cols = slice(hh * hw, (hh + 1) * hw)
        q = q_ref[:, cols]
        ks = [k_ref[:, cols].astype(BF16)]
        vs = [v_ref[:, cols].astype(BF16)]
        if has_cache:
            ks.append(_swap_quarters_in_lanes(ck_ref[0, 0, :, cols], head_dim).astype(BF16))
            vs.append(cv_ref[0, 0, :, cols].astype(BF16))
        kv = [(k[r:r + kv_chunk], v[r:r + kv_chunk]) for k, v in zip(ks, vs) for r in range(0, k.shape[0], kv_chunk)]
        outs = []
        for c in range(2):
            qc = q[:, c * head_dim:(c + 1) * head_dim]
            m = denom = acc = None
            for k, v in kv:
                s = lax.dot_general(qc, k[:, c * head_dim:(c + 1) * head_dim],
                                    (((1,), (1,)), ((), ())), preferred_element_type=F32)
                s_max = jnp.max(s, axis=-1, keepdims=True)
                m_new = s_max if m is None else jnp.maximum(m, s_max)
                p = jnp.exp2(s - m_new)
                p_sum = jnp.sum(p, axis=-1, keepdims=True)
                pv = jnp.dot(p.astype(BF16), v, preferred_element_type=F32)
                if m is None:
                    denom, acc = p_sum, pv
                else:
                    alpha = jnp.exp2(m - m_new)
                    denom, acc = denom * alpha + p_sum, acc * alpha + pv
                m = m_new
            outs.append(acc / denom)
        o = outs[0] - lam * outs[1]
        r = lax.rsqrt(jnp.mean(o * o, axis=-1, keepdims=True) + SUBLN_EPS)
        o_ref[:, cols] = (o * r * g_ref[...] * (1.0 - lam_init)).astype(o_ref.dtype)


def _diff_attention(qkv, lam_vecs, subln_g, lam_init, batch, seq_len, n_heads, head_dim,
                    cache_k=None, cache_v=None, slot=0):
    hw = 2 * head_dim
    tq = _tile(seq_len, 1024, BF16_ROWS)
    nq = seq_len // tq
    heads = _tile(n_heads, max(4096 // seq_len, 1), 1)
    bw = heads * hw
    ng = n_heads // heads
    has_cache = cache_k is not None
    if isinstance(qkv, tuple):
        q_arr, k_arr, v_arr = qkv
        k_off = v_off = 0
    else:
        q_arr = k_arr = v_arr = qkv
        k_off, v_off = ng, 2 * ng
    in_specs = [pl.BlockSpec(lam_vecs.shape, lambda b, h, i: (0, 0)),
                pl.BlockSpec((tq, bw), lambda b, h, i: (b * nq + i, h)),
                pl.BlockSpec((seq_len, bw), lambda b, h, i: (b, k_off + h)),
                pl.BlockSpec((seq_len, bw), lambda b, h, i: (b, v_off + h))]
    args = [lam_vecs, q_arr, k_arr, v_arr]
    if has_cache:
        past = cache_k.shape[2]
        in_specs += [pl.BlockSpec((1, 1, past, bw), lambda b, h, i: (b, slot, 0, h))] * 2
        args += [cache_k, cache_v]
    in_specs.append(pl.BlockSpec((1, hw), lambda b, h, i: (0, 0)))
    args.append(subln_g)
    return pl.pallas_call(
        functools.partial(_attn_kernel, has_cache=has_cache, lam_init=lam_init,
                          head_dim=head_dim, kv_chunk=_tile(seq_len, 512, BF16_ROWS), heads=heads),
        grid=(batch, ng, nq),
        in_specs=in_specs,
        out_specs=pl.BlockSpec((tq, bw), lambda b, h, i: (b * nq + i, h)),
        out_shape=jax.ShapeDtypeStruct((batch * seq_len, n_heads * hw), BF16),
        compiler_params=_params("parallel", "parallel", "arbitrary"),
        name="diff_attention",
    )(*args)


def _attn_out_kernel(a_ref, w_ref, x_ref, gate_ref, o_ref):
    nb, tl, k = a_ref.shape
    acc = jnp.dot(a_ref[...].reshape(nb * tl, k), w_ref[...], preferred_element_type=F32)
    res = x_ref[...] + gate_ref[...] * acc.reshape(nb, tl, -1)
    o_ref[...] = pltpu.einshape("btn->tbn", res).reshape(tl * nb, -1)


def _attn_out_proj(o, w, x, gate, batch, seq_len):
    k = o.shape[1]
    n = w.shape[1]
    tl = _tile(seq_len, 128, BF16_ROWS)
    tn = _tile(n, 1024, LANE)
    nl = seq_len // tl
    return pl.pallas_call(
        _attn_out_kernel,
        grid=(batch // SUBLANE, nl, n // tn),
        in_specs=[pl.BlockSpec((SUBLANE, tl, k), lambda s, t, j: (s, t, 0)),
                  pl.BlockSpec((k, tn), lambda s, t, j: (0, j)),
                  pl.BlockSpec((SUBLANE, tl, tn), lambda s, t, j: (s, t, j)),
                  pl.BlockSpec((SUBLANE, 1, tn), lambda s, t, j: (s, 0, j))],
        out_specs=pl.BlockSpec((tl * SUBLANE, tn), lambda s, t, j: (s * nl + t, j)),
        out_shape=jax.ShapeDtypeStruct((batch * seq_len, n), F32),
        compiler_params=_params("parallel", "parallel", "arbitrary"),
        name="attn_out_proj",
    )(o.reshape(batch, seq_len, k), w, x.reshape(batch, seq_len, n), gate)


def _to_batch_major_kernel(x_ref, o_ref):
    nb, tl, d = o_ref.shape
    o_ref[...] = pltpu.einshape("tbd->btd", x_ref[...].reshape(tl, nb, d))


def _to_batch_major(x, batch, seq_len):
    m, d = x.shape
    tl = _tile(seq_len, 64, SUBLANE)
    nl = seq_len // tl
    return pl.pallas_call(
        _to_batch_major_kernel,
        grid=(batch // SUBLANE, nl),
        in_specs=[pl.BlockSpec((tl * SUBLANE, d), lambda s, t: (s * nl + t, 0))],
        out_specs=pl.BlockSpec((SUBLANE, tl, d), lambda s, t: (s, t, 0)),
        out_shape=jax.ShapeDtypeStruct((batch, seq_len, d), F32),
        compiler_params=_params("parallel", "parallel"),
        name="to_batch_major",
    )(x).reshape(m, d)


def _mm_res_kernel(a_ref, w_ref, x_ref, gate_ref, o_ref):
    acc = jnp.dot(a_ref[...], w_ref[...], preferred_element_type=F32)
    o_ref[...] = x_ref[...] + _per_seq(acc, gate_ref[0], jnp.multiply)


def _matmul_gate_residual(a, w, x, gate, seq_len):
    m, k = a.shape
    n = w.shape[1]
    group_rows = seq_len * SUBLANE
    tm = _tile(group_rows, 1024, BF16_ROWS)
    tn = _tile(n, 512, LANE)
    return pl.pallas_call(
        _mm_res_kernel,
        grid=(m // tm, n // tn),
        in_specs=[pl.BlockSpec((tm, k), lambda i, j: (i, 0)),
                  pl.BlockSpec((k, tn), lambda i, j: (0, j)),
                  pl.BlockSpec((tm, tn), lambda i, j: (i, j)),
                  pl.BlockSpec((1, SUBLANE, tn), lambda i, j: (i * tm // group_rows, 0, j))],
        out_specs=pl.BlockSpec((tm, tn), lambda i, j: (i, j)),
        out_shape=jax.ShapeDtypeStruct((m, n), F32),
        compiler_params=_params("parallel", "arbitrary"),
        name="matmul_gate_residual",
    )(a, w, x, gate)


def _up_kernel(x_ref, xp_ref, xn_ref, g_ref, sc_ref, sh_ref, wg_ref, wv_ref,
               cwg_ref, cwv_ref, cbg_ref, cbv_ref, o_ref, xm_ref, *, tm, group_rows, chunk):
    halo = BF16_ROWS
    i = pl.program_id(0)
    j = pl.program_id(1)

    @pl.when(j == 0)
    def _():
        g, sc, sh = g_ref[...], sc_ref[0], sh_ref[0]
        xm_ref[0:halo] = _norm_mod_tm(xp_ref[...], g, sc, sh).astype(BF16)
        xm_ref[halo:halo + tm] = _norm_mod_tm(x_ref[...], g, sc, sh).astype(BF16)
        xm_ref[halo + tm:] = _norm_mod_tm(xn_ref[...], g, sc, sh).astype(BF16)

        @pl.when((i * tm) % group_rows == 0)
        def _():
            xm_ref[0:halo] = jnp.zeros((halo, xm_ref.shape[1]), BF16)

        @pl.when(((i + 1) * tm) % group_rows == 0)
        def _():
            xm_ref[halo + tm:] = jnp.zeros((halo, xm_ref.shape[1]), BF16)

    xm = xm_ref[...]

    def conv(h, cw_ref, cb_ref, cols):
        cw = cw_ref[:, cols]
        lo = halo - SUBLANE
        return (h[lo:lo + tm] * cw[0:1] + h[halo:halo + tm] * cw[1:2]
                + h[halo + SUBLANE:halo + SUBLANE + tm] * cw[2:3] + cb_ref[:, cols])

    for c in range(o_ref.shape[1] // chunk):
        cols = slice(c * chunk, (c + 1) * chunk)
        gate = conv(jnp.dot(xm, wg_ref[:, cols], preferred_element_type=F32), cwg_ref, cbg_ref, cols)
        val = conv(jnp.dot(xm, wv_ref[:, cols], preferred_element_type=F32), cwv_ref, cbv_ref, cols)
        o_ref[:, cols] = (gate * jax.nn.sigmoid(gate) * val).astype(o_ref.dtype)


def _conv_ffn_up(x, g, sc, sh, w_up, conv_w, conv_b, seq_len):
    m, d = x.shape
    d_ff = w_up.shape[1] // 2
    halo = BF16_ROWS
    group_rows = seq_len * SUBLANE
    tm = _tile(group_rows, 1024, halo)
    tn = _tile(d_ff, 512, LANE)
    chunk = _tile(tn, MXU_DIM, LANE)
    nj = d_ff // tn
    hb = tm // halo
    last_hb = m // halo - 1

    mod_spec = pl.BlockSpec((1, SUBLANE, d), lambda i, j: (i * tm // group_rows, 0, 0))
    return pl.pallas_call(
        functools.partial(_up_kernel, tm=tm, group_rows=group_rows, chunk=chunk),
        grid=(m // tm, nj),
        in_specs=[pl.BlockSpec((tm, d), lambda i, j: (i, 0)),
                  pl.BlockSpec((halo, d), lambda i, j: (jnp.maximum(i * hb - 1, 0), 0)),
                  pl.BlockSpec((halo, d), lambda i, j: (jnp.minimum((i + 1) * hb, last_hb), 0)),
                  pl.BlockSpec((1, d), lambda i, j: (0, 0)),
                  mod_spec, mod_spec,
                  pl.BlockSpec((d, tn), lambda i, j: (0, j)),
                  pl.BlockSpec((d, tn), lambda i, j: (0, nj + j)),
                  pl.BlockSpec((3, tn), lambda i, j: (0, j)),
                  pl.BlockSpec((3, tn), lambda i, j: (0, nj + j)),
                  pl.BlockSpec((1, tn), lambda i, j: (0, j)),
                  pl.BlockSpec((1, tn), lambda i, j: (0, nj + j))],
        out_specs=pl.BlockSpec((tm, tn), lambda i, j: (i, j)),
        out_shape=jax.ShapeDtypeStruct((m, d_ff), BF16),
        scratch_shapes=[pltpu.VMEM((tm + 2 * halo, d), BF16)],
        compiler_params=_params("parallel", "arbitrary"),
        name="conv_ffn_up",
    )(x, x, x, g, sc, sh, w_up, w_up, conv_w, conv_w, conv_b, conv_b)


def _discretize(a_re, a_im, log_step):
    dt = jnp.exp(log_step)
    mag = jnp.exp(a_re * dt)
    lb_re = mag * jnp.cos(a_im * dt)
    lb_im = mag * jnp.sin(a_im * dt)
    num_re = lb_re - 1.0
    den = a_re * a_re + a_im * a_im
    f_re = (num_re * a_re + lb_im * a_im) / den
    f_im = (lb_im * a_re - num_re * a_im) / den
    return lb_re, lb_im, f_re, f_im


def _s5_disc_kernel(a_re_ref, a_im_ref, ls_ref, ax_re_ref, ax_im_ref, lsx_ref, b_re_ref, b_im_ref,
                    lb_re_ref, lb_im_ref, bb_re_ref, bb_im_ref):
    lb_re, lb_im, _, _ = _discretize(a_re_ref[...], a_im_ref[...], ls_ref[...])
    lb_re_ref[...] = lb_re
    lb_im_ref[...] = lb_im
    _, _, f_re, f_im = _discretize(ax_re_ref[...], ax_im_ref[...], lsx_ref[...])
    b_re, b_im = b_re_ref[...], b_im_ref[...]
    bb_re_ref[...] = f_re * b_re - f_im * b_im
    bb_im_ref[...] = f_re * b_im + f_im * b_re


def _s5_discretize(a_re, a_im, log_step, b_re, b_im):
    two, g, p, c = b_re.shape
    rows = two * g
    a2 = lambda a: a.reshape(rows, p)
    ax = lambda a: jnp.repeat(a.reshape(rows, p), c, axis=-1)
    ls = jnp.broadcast_to(log_step.reshape(rows, 1), (rows, p))
    lsx = jnp.broadcast_to(log_step.reshape(rows, 1), (rows, p * c))
    return pl.pallas_call(
        _s5_disc_kernel,
        out_shape=[jax.ShapeDtypeStruct((rows, p), F32)] * 2 + [jax.ShapeDtypeStruct((rows, p * c), F32)] * 2,
        compiler_params=pltpu.CompilerParams(vmem_limit_bytes=VMEM_LIMIT_BYTES),
        name="s5_discretize",
    )(a2(a_re), a2(a_im), ls, ax(a_re), ax(a_im), lsx, b_re.reshape(rows, p * c), b_im.reshape(rows, p * c))


def _s5_in_kernel(x_ref, g_ref, sc_ref, sh_ref, o_ref):
    o_ref[...] = _norm_mod_tm(x_ref[...], g_ref[...], sc_ref[0], sh_ref[0]).astype(o_ref.dtype)


def _s5_input(x, g, sc, sh, seq_len):
    m, d = x.shape
    group_rows = seq_len * SUBLANE
    tm = _tile(group_rows, 1024, BF16_ROWS)
    mod_spec = pl.BlockSpec((1, SUBLANE, d), lambda i: (i * tm // group_rows, 0, 0))
    return pl.pallas_call(
        _s5_in_kernel,
        grid=(m // tm,),
        in_specs=[pl.BlockSpec((tm, d), lambda i: (i, 0)), pl.BlockSpec((1, d), lambda i: (0, 0)),
                  mod_spec, mod_spec],
        out_specs=pl.BlockSpec((tm, d), lambda i: (i, 0)),
        out_shape=jax.ShapeDtypeStruct((m, d), BF16),
        compiler_params=_params("parallel"),
        name="s5_input",
    )(x, g, sc, sh)


def _s5_scan_kernel(u_ref, wb_ref, wc_ref, lam_ref, h0_ref, y_ref, fin_ref, st_ref, buf_ref,
                    *, nsub, nt, sub_t):
    direction = pl.program_id(1)
    t_idx = pl.program_id(3)
    n_tiles = st_ref.shape[0]
    half = n_tiles // 2
    sr = sub_t * SUBLANE
    steps = sub_t // n_tiles

    @pl.when(t_idx == 0)
    def _():
        st_ref[...] = h0_ref[0, 0, 0]

    lam = jnp.broadcast_to(lam_ref[0, 0], st_ref.shape)
    a_re, a_im = lam[:half], lam[half:]

    def first_row(p, d):
        row = (p + d * (nsub - 1 - 2 * p)) * sr
        return row if isinstance(row, int) else pl.multiple_of(row, sr)

    def run_pass(d, p, state, do_bu, do_scan, do_ch, phase=None):
        phase = p % 3 if phase is None else phase
        slot_bu, slot_scan, slot_ch = phase, (phase + 2) % 3, (phase + 1) % 3
        if do_ch:
            h_all = jnp.concatenate([buf_ref[slot_ch, i] for i in range(n_tiles)], axis=1)
            y_ref[0, pl.ds(first_row(p - 2, d), sr), :] = jnp.dot(h_all.astype(BF16), wc_ref[0, 0],
                                                                  preferred_element_type=F32)
        if do_bu:
            u_sub = u_ref[pl.ds(first_row(p, d), sr), :]
        h_re, h_im = state
        for i in range(n_tiles):
            if do_bu:
                buf_ref[slot_bu, i] = jnp.dot(u_sub, wb_ref[0, 0, i], preferred_element_type=F32)
            if do_scan:
                for k in range(steps):
                    t = i * steps + k
                    row = (t + d * (sub_t - 1 - 2 * t)) * SUBLANE
                    bu = buf_ref[slot_scan, :, pl.ds(row, SUBLANE), :]
                    n_re = a_re * h_re - a_im * h_im + bu[:half]
                    n_im = a_re * h_im + a_im * h_re + bu[half:]
                    buf_ref[slot_scan, 0:half, pl.ds(row, SUBLANE), :] = n_re
                    buf_ref[slot_scan, half:, pl.ds(row, SUBLANE), :] = n_im
                    h_re, h_im = n_re, n_im
        return h_re, h_im

    def scan_block(d):
        state = (st_ref[0:half], st_ref[half:])
        for p in range(min(2, nsub + 2)):
            state = run_pass(d, p, state, p < nsub, 1 <= p <= nsub, False)
        n_trips = max(nsub - 2, 0) // 3

        def three_passes(trip, st):
            for k in range(3):
                st = run_pass(d, 2 + 3 * trip + k, st, True, True, True, phase=(2 + k) % 3)
            return st

        if n_trips > 0:
            state = lax.fori_loop(0, n_trips, three_passes, state)
        for p in range(2 + 3 * n_trips, nsub):
            state = run_pass(d, p, state, True, True, True)
        for p in range(max(2, nsub), nsub + 2):
            state = run_pass(d, p, state, p < nsub, p <= nsub, True)
        st_ref[0:half] = state[0]
        st_ref[half:] = state[1]

    for static_d in (0, 1):
        pl.when(direction == static_d)(functools.partial(scan_block, static_d))

    @pl.when(t_idx == nt - 1)
    def _():
        fin_ref[0, 0, 0] = st_ref[...]


def _s5_scan(u, wb, wc, lam, h0, seq_len):
    rows, d = u.shape
    ns = rows // (seq_len * SUBLANE)
    ngb = d // MXU_DIM
    n_tiles = wb.shape[2]
    sub_t = _tile(seq_len, 64, n_tiles)
    tc = _tile(seq_len, 16 * sub_t, sub_t)
    nt = seq_len // tc
    assert sub_t % n_tiles == 0 and tc % sub_t == 0

    def t_blk(n, dd, t):
        return n * nt + t + dd * (nt - 1 - 2 * t)

    w_spec = pl.BlockSpec((1, 1, n_tiles, MXU_DIM, MXU_DIM), lambda n, dd, gb, t: (dd, gb, 0, 0, 0))
    st_spec = pl.BlockSpec((1, 1, 1, n_tiles, SUBLANE, MXU_DIM), lambda n, dd, gb, t: (n, dd, gb, 0, 0, 0))
    return pl.pallas_call(
        functools.partial(_s5_scan_kernel, nsub=tc // sub_t, nt=nt, sub_t=sub_t),
        grid=(ns, 2, ngb, nt),
        in_specs=[pl.BlockSpec((tc * SUBLANE, MXU_DIM), lambda n, dd, gb, t: (t_blk(n, dd, t), gb)),
                  w_spec,
                  pl.BlockSpec((1, 1, n_tiles * MXU_DIM, MXU_DIM), lambda n, dd, gb, t: (dd, gb, 0, 0)),
                  pl.BlockSpec((1, 1, n_tiles, 1, MXU_DIM), lambda n, dd, gb, t: (dd, gb, 0, 0, 0)),
                  st_spec],
        out_specs=[pl.BlockSpec((1, tc * SUBLANE, MXU_DIM), lambda n, dd, gb, t: (dd, t_blk(n, dd, t), gb)),
                   st_spec],
        out_shape=[jax.ShapeDtypeStruct((2, rows, d), F32),
                   jax.ShapeDtypeStruct((ns, 2, ngb, n_tiles, SUBLANE, MXU_DIM), F32)],
        scratch_shapes=[pltpu.VMEM((n_tiles, SUBLANE, MXU_DIM), F32),
                        pltpu.VMEM((3, n_tiles, sub_t * SUBLANE, MXU_DIM), F32)],
        compiler_params=_params("parallel", "parallel", "parallel", "arbitrary"),
        name="s5_scan",
    )(u, wb, wc, lam, h0)


def _s5_block_weights(lb_re, lb_im, bb_re, bb_im, c_re, c_im, n_groups, state_dim, group):
    gpb = MXU_DIM // group
    ngb = n_groups // gpb
    eye = jnp.eye(gpb, dtype=F32)

    def b_blocks(bb):
        bb = bb.reshape(2, ngb, gpb, state_dim, group)
        return jnp.einsum('dbgpc,gh->dbgchp', bb, eye).reshape(2, ngb, gpb * group, gpb * state_dim)

    def c_blocks(cc):
        cc = cc.reshape(2, ngb, gpb, group, state_dim)
        return jnp.einsum('dbgcp,gh->dbgphc', cc, eye).reshape(2, ngb, gpb * state_dim, gpb * group)

    wb = jnp.concatenate([b_blocks(bb_re), b_blocks(bb_im)], axis=-1).astype(BF16)
    wc = jnp.concatenate([c_blocks(c_re), -c_blocks(c_im)], axis=-2).astype(BF16)
    lam = jnp.concatenate([lb_re.reshape(2, ngb, 1, gpb * state_dim),
                           lb_im.reshape(2, ngb, 1, gpb * state_dim)], axis=-1)
    n_tiles = wb.shape[-1] // MXU_DIM
    wb = wb.reshape(2, ngb, MXU_DIM, n_tiles, MXU_DIM).transpose(0, 1, 3, 2, 4)
    lam = lam.reshape(2, ngb, n_tiles, 1, MXU_DIM)
    return wb, wc, lam


def _s5_state_blocks(s_re, s_im, ngb):
    b, two, g, p = s_re.shape
    sw = g * p // ngb

    def blk(s):
        return s.reshape(b // SUBLANE, SUBLANE, two, ngb, sw).transpose(0, 2, 3, 1, 4)

    st = jnp.concatenate([blk(s_re), blk(s_im)], axis=-1)
    st = st.reshape(st.shape[:4] + (2 * sw // MXU_DIM, MXU_DIM))
    return st.transpose(0, 1, 2, 4, 3, 5)


def _s5_state_unblock(fin, g, p):
    ns, two, ngb, n_tiles, sub, lanes = fin.shape
    fin = fin.transpose(0, 1, 2, 4, 3, 5).reshape(ns, two, ngb, sub, n_tiles * lanes)
    sw = n_tiles * lanes // 2

    def unblk(s):
        return s.transpose(0, 3, 1, 2, 4).reshape(ns * sub, two, g, p)

    return unblk(fin[..., :sw]), unblk(fin[..., sw:])


def _glu_kernel(x_ref, yf_ref, yb_ref, g_ref, sc_ref, sh_ref, dskip_ref, w_ref, b_ref, gate_ref, o_ref):
    x = x_ref[...]
    xm = _norm_mod_tm(x, g_ref[...], sc_ref[0], sh_ref[0])
    gz = jax.nn.gelu(yf_ref[0] + yb_ref[0] + dskip_ref[...] * xm)
    acc = jnp.dot(gz.astype(BF16), w_ref[...], preferred_element_type=F32) + b_ref[...]
    o_ref[...] = x + _per_seq(gz * jax.nn.sigmoid(acc), gate_ref[0], jnp.multiply)


def _s5_glu(x, y, g, sc, sh, d_skip, w_glu, b_glu, gate, seq_len):
    m, d = x.shape
    group_rows = seq_len * SUBLANE
    tm = _tile(group_rows, 256, BF16_ROWS)
    mod_spec = pl.BlockSpec((1, SUBLANE, d), lambda i: (i * tm // group_rows, 0, 0))
    vec_spec = pl.BlockSpec((1, d), lambda i: (0, 0))
    return pl.pallas_call(
        _glu_kernel,
        grid=(m // tm,),
        in_specs=[pl.BlockSpec((tm, d), lambda i: (i, 0)),
                  pl.BlockSpec((1, tm, d), lambda i: (0, i, 0)),
                  pl.BlockSpec((1, tm, d), lambda i: (1, i, 0)),
                  vec_spec, mod_spec, mod_spec, vec_spec,
                  pl.BlockSpec((d, d), lambda i: (0, 0)),
                  vec_spec, mod_spec],
        out_specs=pl.BlockSpec((tm, d), lambda i: (i, 0)),
        out_shape=jax.ShapeDtypeStruct((m, d), F32),
        compiler_params=_params("parallel"),
        name="s5_glu",
    )(x, y, y, g, sc, sh, d_skip, w_glu, b_glu, gate)


def _final_norm_kernel(x_ref, g_ref, o_ref):
    nb, tl, d = o_ref.shape
    x = x_ref[...]
    y = x * _rstd(x) * g_ref[...]
    o_ref[...] = pltpu.einshape("tbd->btd", y.reshape(tl, nb, d))


def _final_norm(x, g, batch, seq_len):
    m, d = x.shape
    tl = _tile(seq_len, 64, SUBLANE)
    nl = seq_len // tl
    return pl.pallas_call(
        _final_norm_kernel,
        grid=(batch // SUBLANE, nl),
        in_specs=[pl.BlockSpec((tl * SUBLANE, d), lambda s, t: (s * nl + t, 0)),
                  pl.BlockSpec((1, d), lambda s, t: (0, 0))],
        out_specs=pl.BlockSpec((SUBLANE, tl, d), lambda s, t: (s, t, 0)),
        out_shape=jax.ShapeDtypeStruct((batch, seq_len, d), F32),
        compiler_params=_params("parallel", "parallel"),
        name="final_norm",
    )(x, g)


def kernel(x_prompt, x_sample, cache_k, cache_v, state_re, state_im, c, c_ctx, w_mod, b_mod, norm_g, w_qkv, lam_vecs, subln_g, w_o, ssm_a_re, ssm_a_im, ssm_log_step, ssm_b_re, ssm_b_im, ssm_c_re, ssm_c_im, ssm_d, w_glu, b_glu, w_up, conv_w, conv_b, w_down, final_g):
    bp, lp, d = x_prompt.shape
    bs, ls, _ = x_sample.shape
    depth = w_mod.shape[0]
    n_heads, head_dim = cache_k.shape[3], cache_k.shape[5]
    n_groups, state_dim, group = ssm_b_re.shape[2], ssm_b_re.shape[3], ssm_b_re.shape[4]
    assert bp % SUBLANE == 0 and bs % SUBLANE == 0 and d % MXU_DIM == 0 and MXU_DIM % group == 0
    assert 2 * head_dim == MXU_DIM and ls % GRID_W == 0

    pad_rows = -(bs + 1) % SUBLANE
    cond = jnp.concatenate([c, c_ctx[None], jnp.zeros((pad_rows, d), F32)], axis=0)
    mod = _modulation(cond, w_mod, b_mod).reshape(depth, cond.shape[0], 6, d)

    streams = [dict(x=x_prompt.reshape(bp * lp, d), batch=bp, seq=lp, time_major=False),
               dict(x=x_sample.reshape(bs * ls, d), batch=bs, seq=ls, time_major=False)]
    new_k, new_v, new_sre, new_sim = [], [], [], []
    q_scale = head_dim ** -0.5 * math.log2(math.e)
    rope = _rope_tables(ls, head_dim, q_scale)
    ngb = d // MXU_DIM

    for i in range(depth):
        mix, slot = i % N_MIXERS, i // N_MIXERS
        mods = [[jnp.broadcast_to(mod[i, bs, j], (bp, d)) for j in range(6)],
                [mod[i, :bs, j] for j in range(6)]]
        g1 = norm_g[i, 0][None]
        g2 = norm_g[i, 1][None]

        def per_group(v):
            return v.reshape(v.shape[0] // SUBLANE, SUBLANE, d)

        if mix == 0:
            lam_init = 0.8 - 0.6 * math.exp(-0.3 * i)
            nqk = n_heads * 2 * head_dim
            wq = _to_bf16(w_qkv, slot, swap_cols=2 * nqk, head_dim=head_dim)
            wo = _to_bf16(w_o, slot)
            sub_g = subln_g[slot][None]
            ck = cache_k.reshape(cache_k.shape[:3] + (-1,))
            cv = cache_v.reshape(cache_v.shape[:3] + (-1,))
            for s, (st, md) in enumerate(zip(streams, mods)):
                b, l = st['batch'], st['seq']
                if st['time_major']:
                    st['x'] = _to_batch_major(st['x'], b, l)
                sh1, sc1 = (v[:1, None] if s == 0 else v[:, None] for v in (md[0], md[1]))
                if s == 0:
                    q_p, k_p, v_p = _qkv_proj(st['x'], g1, sc1, sh1, wq, l, q_scale)
                    new_k.append(_swap_middle_quarters(k_p, head_dim).reshape(b, l, n_heads, 2, head_dim))
                    new_v.append(v_p.reshape(b, l, n_heads, 2 * head_dim))
                    o = _diff_attention((q_p, k_p, v_p), lam_vecs[slot], sub_g, lam_init, b, l, n_heads, head_dim)
                else:
                    qkv = _qkv_proj(st['x'], g1, sc1, sh1, wq, l, q_scale, rope_tables=rope)
                    o = _diff_attention(qkv, lam_vecs[slot], sub_g, lam_init, b, l, n_heads, head_dim,
                                        cache_k=ck, cache_v=cv, slot=slot)
                st['x'] = _attn_out_proj(o, wo, st['x'], md[2][:, None], b, l)
                st['time_major'] = True
        else:
            lb_re, lb_im, bb_re, bb_im = _s5_discretize(ssm_a_re[slot], ssm_a_im[slot], ssm_log_step[slot],
                                                        ssm_b_re[slot], ssm_b_im[slot])
            wb, wc, lam = _s5_block_weights(lb_re, lb_im, bb_re, bb_im, ssm_c_re[slot], ssm_c_im[slot],
                                            n_groups, state_dim, group)
            wg = _to_bf16(w_glu, slot)
            for s, (st, md) in enumerate(zip(streams, mods)):
                b, l = st['batch'], st['seq']
                sh1, sc1, gt1 = per_group(md[0]), per_group(md[1]), per_group(md[2])
                u = _s5_input(st['x'], g1, sc1, sh1, l)
                if s == 0:
                    h0 = jnp.zeros((b // SUBLANE, 2, ngb, wb.shape[2], SUBLANE, MXU_DIM), F32)
                else:
                    h0 = _s5_state_blocks(state_re[:, slot], state_im[:, slot], ngb)
                y, fin = _s5_scan(u, wb, wc, lam, h0, l)
                if s == 0:
                    f_re, f_im = _s5_state_unblock(fin, n_groups, state_dim)
                    new_sre.append(f_re)
                    new_sim.append(f_im)
                st['x'] = _s5_glu(st['x'], y, g1, sc1, sh1, ssm_d[slot][None], wg, b_glu[slot][None], gt1, l)
        wu = _to_bf16(w_up, i)
        wd = _to_bf16(w_down, i)
        for st, md in zip(streams, mods):
            sh2, sc2, gt2 = per_group(md[3]), per_group(md[4]), per_group(md[5])
            act = _conv_ffn_up(st['x'], g2, sc2, sh2, wu, conv_w[i], conv_b[i][None], st['seq'])
            st['x'] = _matmul_gate_residual(act, wd, st['x'], gt2, st['seq'])

    y_prompt = _final_norm(streams[0]['x'], final_g[None], bp, lp)
    y_sample = _final_norm(streams[1]['x'], final_g[None], bs, ls)
    return (y_prompt, y_sample, jnp.stack(new_k, axis=1), jnp.stack(new_v, axis=1),
            jnp.stack(new_sre, axis=1), jnp.stack(new_sim, axis=1))
```

```python
import functools
import math

import jax
import jax.numpy as jnp
from jax import lax
from jax.experimental import pallas as pl
from jax.experimental.pallas import tpu as pltpu

F32 = jnp.float32
BF16 = jnp.bfloat16

GRID_W = 64
ROPE_THETA = 10000.0
NORM_EPS = 1e-6
SUBLN_EPS = 1e-5
N_MIXERS = 2

LANE = 128
SUBLANE = 8
BF16_ROWS = 16
MXU_DIM = 256
VMEM_LIMIT_BYTES = 52 * 1024 * 1024


def _tile(n, pref, align):
    t = min(pref, n)
    t -= t % align
    while t >= align:
        if n % t == 0:
            return t
        t -= align
    return n


def _row_tile(m, seq_len, n_mod, pref, align):
    return _tile(m, pref, align) if n_mod == 1 else _tile(seq_len, pref, align)


def _params(*sem):
    return pltpu.CompilerParams(dimension_semantics=sem, vmem_limit_bytes=VMEM_LIMIT_BYTES)


def _rstd(x):
    return lax.rsqrt(jnp.mean(x * x, axis=-1, keepdims=True) + NORM_EPS)


def _norm_mod(x, g, sc, sh):
    return x * _rstd(x) * (g * (1.0 + sc)) + sh


def _per_seq(x, pattern, op):
    rows, n = x.shape
    return op(x.reshape(rows // SUBLANE, SUBLANE, n), pattern[None]).reshape(rows, n)


def _norm_mod_tm(x, g, sc, sh):
    scaled = _per_seq(x * _rstd(x), g * (1.0 + sc), jnp.multiply)
    return _per_seq(scaled, sh, jnp.add)


def _mod_kernel(c_ref, w_ref, b_ref, o_ref):
    c = c_ref[...]
    s = (c * jax.nn.sigmoid(c)).astype(BF16)
    o_ref[0] = jnp.dot(s, w_ref[0].astype(BF16), preferred_element_type=F32) + b_ref[0]


def _modulation(cond, w_mod, b_mod):
    depth, d, n = w_mod.shape
    rows = cond.shape[0]
    tn = _tile(n, 1024, LANE)
    return pl.pallas_call(
        _mod_kernel,
        grid=(depth, n // tn),
        in_specs=[pl.BlockSpec((rows, d), lambda l, j: (0, 0)),
                  pl.BlockSpec((1, d, tn), lambda l, j: (l, 0, j)),
                  pl.BlockSpec((1, 1, tn), lambda l, j: (l, 0, j))],
        out_specs=pl.BlockSpec((1, rows, tn), lambda l, j: (l, 0, j)),
        out_shape=jax.ShapeDtypeStruct((depth, rows, n), F32),
        compiler_params=_params("parallel", "parallel"),
        name="modulation",
    )(cond, w_mod, b_mod.reshape(depth, 1, n))


def _swap_quarters_in_lanes(x, head_dim):
    q4 = head_dim // 4
    parts = []
    for c in range(x.shape[1] // head_dim):
        base = c * head_dim
        parts += [x[:, base:base + q4], x[:, base + 2 * q4:base + 3 * q4],
                  x[:, base + q4:base + 2 * q4], x[:, base + 3 * q4:base + 4 * q4]]
    return jnp.concatenate(parts, axis=1)


def _cast_kernel(w_ref, o_ref, *, swap_tiles, head_dim):
    w = w_ref[0]
    if swap_tiles == 0:
        o_ref[...] = w.astype(BF16)
        return
    j = pl.program_id(1)

    @pl.when(j < swap_tiles)
    def _():
        o_ref[...] = _swap_quarters_in_lanes(w, head_dim).astype(BF16)

    @pl.when(j >= swap_tiles)
    def _():
        o_ref[...] = w.astype(BF16)


def _to_bf16(w_stack, index, swap_cols=0, head_dim=LANE):
    _, k, n = w_stack.shape
    tk = _tile(k, 1024, BF16_ROWS)
    tn = _tile(math.gcd(swap_cols, n), 2048, LANE)
    return pl.pallas_call(
        functools.partial(_cast_kernel, swap_tiles=swap_cols // tn, head_dim=head_dim),
        grid=(k // tk, n // tn),
        in_specs=[pl.BlockSpec((1, tk, tn), lambda i, j: (index, i, j))],
        out_specs=pl.BlockSpec((tk, tn), lambda i, j: (i, j)),
        out_shape=jax.ShapeDtypeStruct((k, n), BF16),
        compiler_params=_params("parallel", "parallel"),
        name="weight_to_bf16",
    )(w_stack)


def _qkv_kernel(*refs, rope, n_rope_tiles, tn, q_scale):
    if rope:
        x_ref, g_ref, sc_ref, sh_ref, w_ref, cos_ref, sin_ref, o_ref, xm_ref = refs
    else:
        x_ref, g_ref, sc_ref, sh_ref, w_ref, oq_ref, ok_ref, ov_ref, xm_ref = refs
    j = pl.program_id(1)

    @pl.when(j == 0)
    def _():
        xm_ref[...] = _norm_mod(x_ref[...], g_ref[...], sc_ref[0], sh_ref[0]).astype(BF16)

    acc = jnp.dot(xm_ref[...], w_ref[...], preferred_element_type=F32)
    if not rope:
        n_q = n_rope_tiles // 2

        @pl.when(j < n_q)
        def _():
            oq_ref[...] = (acc * q_scale).astype(oq_ref.dtype)

        @pl.when((j >= n_q) & (j < 2 * n_q))
        def _():
            ok_ref[...] = acc

        @pl.when(j >= 2 * n_q)
        def _():
            ov_ref[...] = acc

        return

    @pl.when(j < n_rope_tiles)
    def _():
        cos = cos_ref[0]
        sin = sin_ref[0]
        for c in range(tn // LANE):
            x = acc[:, c * LANE:(c + 1) * LANE]
            o_ref[:, c * LANE:(c + 1) * LANE] = (x * cos + pltpu.roll(x, LANE // 2, 1) * sin).astype(o_ref.dtype)

    @pl.when(j >= n_rope_tiles)
    def _():
        o_ref[...] = acc.astype(o_ref.dtype)


def _qkv_proj(x, g, sc, sh, w, seq_len, q_scale, rope_tables=None):
    m, d = x.shape
    n = w.shape[1]
    n_rope_cols = 2 * n // 3
    n_mod = sc.shape[0]
    tm = _tile(seq_len, 1024, BF16_ROWS) if rope_tables is not None else _row_tile(m, seq_len, n_mod, 1024, BF16_ROWS)
    tn = _tile(n_rope_cols // 2, 1024 if rope_tables is not None else 512, LANE)
    nq = n_rope_cols // (2 * tn)

    def mod_i(i):
        return (i * tm // seq_len) % n_mod

    in_specs = [pl.BlockSpec((tm, d), lambda i, j: (i, 0)),
                pl.BlockSpec((1, d), lambda i, j: (0, 0)),
                pl.BlockSpec((1, 1, d), lambda i, j: (mod_i(i), 0, 0)),
                pl.BlockSpec((1, 1, d), lambda i, j: (mod_i(i), 0, 0)),
                pl.BlockSpec((d, tn), lambda i, j: (0, j))]
    args = [x, g, sc, sh, w]
    if rope_tables is not None:
        tiles_per_seq = seq_len // tm
        in_specs += [pl.BlockSpec((1, tm, LANE),
                                  lambda i, j: (jnp.where(j < nq, 0, 1), i % tiles_per_seq, 0))] * 2
        args += list(rope_tables)
        out_specs = pl.BlockSpec((tm, tn), lambda i, j: (i, j))
        out_shape = jax.ShapeDtypeStruct((m, n), BF16)
    else:
        out_specs = [pl.BlockSpec((tm, tn), lambda i, j, part=part: (i, jnp.clip(j - part * nq, 0, nq - 1)))
                     for part in range(3)]
        out_shape = [jax.ShapeDtypeStruct((m, n // 3), dt) for dt in (BF16, F32, F32)]
    return pl.pallas_call(
        functools.partial(_qkv_kernel, rope=rope_tables is not None,
                          n_rope_tiles=n_rope_cols // tn, tn=tn, q_scale=q_scale),
        grid=(m // tm, n // tn),
        in_specs=in_specs,
        out_specs=out_specs,
        out_shape=out_shape,
        scratch_shapes=[pltpu.VMEM((tm, d), BF16)],
        compiler_params=_params("parallel", "arbitrary"),
        name="qkv_proj",
    )(*args)


def _swap_middle_quarters(a, head_dim):
    lead = a.shape[:-1]
    a = a.reshape(lead + (a.shape[-1] // head_dim, 2, 2, head_dim // 4))
    return jnp.swapaxes(a, -3, -2).reshape(lead + (-1,))


def _rope_tables(length, head_dim, q_scale):
    pos = jnp.arange(length)
    r = (pos // GRID_W).astype(F32)
    col = (pos % GRID_W).astype(F32)
    half = head_dim // 2
    inv = ROPE_THETA ** (-jnp.arange(0, half, 2, dtype=F32) / half)
    ar = r[:, None] * inv
    ac = col[:, None] * inv
    cos = jnp.concatenate([jnp.cos(ar), jnp.cos(ac)] * 2, axis=-1)
    sin = jnp.concatenate([-jnp.sin(ar), -jnp.sin(ac), jnp.sin(ar), jnp.sin(ac)], axis=-1)
    scales = jnp.array([q_scale, 1.0], F32)[:, None, None]
    return cos[None] * scales, sin[None] * scales


def _attn_kernel(*refs, has_cache, lam_init, head_dim, kv_chunk, heads):
    if has_cache:
        lam_ref, q_ref, k_ref, v_ref, ck_ref, cv_ref, g_ref, o_ref = refs
    else:
        lam_ref, q_ref, k_ref, v_ref, g_ref, o_ref = refs
    hw = 2 * head_dim
    lv = lam_ref[...]
    lam = (jnp.exp(jnp.sum(lv[0:1] * lv[1:2], axis=-1, keepdims=True))
           - jnp.exp(jnp.sum(lv[2:3] * lv[3:4], axis=-1, keepdims=True)) + lam_init)
    for hh in range(heads):
        cols = slice(hh * hw, (hh + 1) * hw)
        q = q_ref[:, cols]
        ks = [k_ref[:, cols].astype(BF16)]
        vs = [v_ref[:, cols].astype(BF16)]
        if has_cache:
            ks.append(_swap_quarters_in_lanes(ck_ref[0, 0, :, cols], head_dim).astype(BF16))
            vs.append(cv_ref[0, 0, :, cols].astype(BF16))
        kv = [(k[r:r + kv_chunk], v[r:r + kv_chunk]) for k, v in zip(ks, vs) for r in range(0, k.shape[0], kv_chunk)]
        outs = []
        for c in range(2):
            qc = q[:, c * head_dim:(c + 1) * head_dim]
            m = denom = acc = None
            for k, v in kv:
                s = lax.dot_general(qc, k[:, c * head_dim:(c + 1) * head_dim],
                                    (((1,), (1,)), ((), ())), preferred_element_type=F32)
                s_max = jnp.max(s, axis=-1, keepdims=True)
                m_new = s_max if m is None else jnp.maximum(m, s_max)
                p = jnp.exp2(s - m_new)
                p_sum = jnp.sum(p, axis=-1, keepdims=True)
                pv = jnp.dot(p.astype(BF16), v, preferred_element_type=F32)
                if m is None:
                    denom, acc = p_sum, pv
                else:
                    alpha = jnp.exp2(m - m_new)
                    denom, acc = denom * alpha + p_sum, acc * alpha + pv
                m = m_new
            outs.append(acc / denom)
        o = outs[0] - lam * outs[1]
        r = lax.rsqrt(jnp.mean(o * o, axis=-1, keepdims=True) + SUBLN_EPS)
        o_ref[:, cols] = (o * r * g_ref[...] * (1.0 - lam_init)).astype(o_ref.dtype)


def _diff_attention(qkv, lam_vecs, subln_g, lam_init, batch, seq_len, n_heads, head_dim,
                    cache_k=None, cache_v=None, slot=0):
    hw = 2 * head_dim
    tq = _tile(seq_len, 1024, BF16_ROWS)
    nq = seq_len // tq
    heads = _tile(n_heads, max(4096 // seq_len, 1), 1)
    bw = heads * hw
    ng = n_heads // heads
    has_cache = cache_k is not None
    if isinstance(qkv, tuple):
        q_arr, k_arr, v_arr = qkv
        k_off = v_off = 0
    else:
        q_arr = k_arr = v_arr = qkv
        k_off, v_off = ng, 2 * ng
    in_specs = [pl.BlockSpec(lam_vecs.shape, lambda b, h, i: (0, 0)),
                pl.BlockSpec((tq, bw), lambda b, h, i: (b * nq + i, h)),
                pl.BlockSpec((seq_len, bw), lambda b, h, i: (b, k_off + h)),
                pl.BlockSpec((seq_len, bw), lambda b, h, i: (b, v_off + h))]
    args = [lam_vecs, q_arr, k_arr, v_arr]
    if has_cache:
        past = cache_k.shape[2]
        in_specs += [pl.BlockSpec((1, 1, past, bw), lambda b, h, i: (b, slot, 0, h))] * 2
        args += [cache_k, cache_v]
    in_specs.append(pl.BlockSpec((1, hw), lambda b, h, i: (0, 0)))
    args.append(subln_g)
    return pl.pallas_call(
        functools.partial(_attn_kernel, has_cache=has_cache, lam_init=lam_init,
                          head_dim=head_dim, kv_chunk=_tile(seq_len, 512, BF16_ROWS), heads=heads),
        grid=(batch, ng, nq),
        in_specs=in_specs,
        out_specs=pl.BlockSpec((tq, bw), lambda b, h, i: (b * nq + i, h)),
        out_shape=jax.ShapeDtypeStruct((batch * seq_len, n_heads * hw), BF16),
        compiler_params=_params("parallel", "parallel", "arbitrary"),
        name="diff_attention",
    )(*args)


def _attn_out_kernel(a_ref, w_ref, x_ref, gate_ref, o_ref):
    nb, tl, k = a_ref.shape
    acc = jnp.dot(a_ref[...].reshape(nb * tl, k), w_ref[...], preferred_element_type=F32)
    res = x_ref[...] + gate_ref[...] * acc.reshape(nb, tl, -1)
    o_ref[...] = pltpu.einshape("btn->tbn", res).reshape(tl * nb, -1)


def _attn_out_proj(o, w, x, gate, batch, seq_len):
    k = o.shape[1]
    n = w.shape[1]
    tl = _tile(seq_len, 128, BF16_ROWS)
    tn = _tile(n, 1024, LANE)
    nl = seq_len // tl
    return pl.pallas_call(
        _attn_out_kernel,
        grid=(batch // SUBLANE, nl, n // tn),
        in_specs=[pl.BlockSpec((SUBLANE, tl, k), lambda s, t, j: (s, t, 0)),
                  pl.BlockSpec((k, tn), lambda s, t, j: (0, j)),
                  pl.BlockSpec((SUBLANE, tl, tn), lambda s, t, j: (s, t, j)),
                  pl.BlockSpec((SUBLANE, 1, tn), lambda s, t, j: (s, 0, j))],
        out_specs=pl.BlockSpec((tl * SUBLANE, tn), lambda s, t, j: (s * nl + t, j)),
        out_shape=jax.ShapeDtypeStruct((batch * seq_len, n), F32),
        compiler_params=_params("parallel", "parallel", "arbitrary"),
        name="attn_out_proj",
    )(o.reshape(batch, seq_len, k), w, x.reshape(batch, seq_len, n), gate)


def _to_batch_major_kernel(x_ref, o_ref):
    nb, tl, d = o_ref.shape
    o_ref[...] = pltpu.einshape("tbd->btd", x_ref[...].reshape(tl, nb, d))


def _to_batch_major(x, batch, seq_len):
    m, d = x.shape
    tl = _tile(seq_len, 64, SUBLANE)
    nl = seq_len // tl
    return pl.pallas_call(
        _to_batch_major_kernel,
        grid=(batch // SUBLANE, nl),
        in_specs=[pl.BlockSpec((tl * SUBLANE, d), lambda s, t: (s * nl + t, 0))],
        out_specs=pl.BlockSpec((SUBLANE, tl, d), lambda s, t: (s, t, 0)),
        out_shape=jax.ShapeDtypeStruct((batch, seq_len, d), F32),
        compiler_params=_params("parallel", "parallel"),
        name="to_batch_major",
    )(x).reshape(m, d)


def _mm_res_kernel(a_ref, w_ref, x_ref, gate_ref, o_ref):
    acc = jnp.dot(a_ref[...], w_ref[...], preferred_element_type=F32)
    o_ref[...] = x_ref[...] + _per_seq(acc, gate_ref[0], jnp.multiply)


def _matmul_gate_residual(a, w, x, gate, seq_len):
    m, k = a.shape
    n = w.shape[1]
    group_rows = seq_len * SUBLANE
    tm = _tile(group_rows, 1024, BF16_ROWS)
    tn = _tile(n, 512, LANE)
    return pl.pallas_call(
        _mm_res_kernel,
        grid=(m // tm, n // tn),
        in_specs=[pl.BlockSpec((tm, k), lambda i, j: (i, 0)),
                  pl.BlockSpec((k, tn), lambda i, j: (0, j)),
                  pl.BlockSpec((tm, tn), lambda i, j: (i, j)),
                  pl.BlockSpec((1, SUBLANE, tn), lambda i, j: (i * tm // group_rows, 0, j))],
        out_specs=pl.BlockSpec((tm, tn), lambda i, j: (i, j)),
        out_shape=jax.ShapeDtypeStruct((m, n), F32),
        compiler_params=_params("parallel", "arbitrary"),
        name="matmul_gate_residual",
    )(a, w, x, gate)


def _up_kernel(x_ref, xp_ref, xn_ref, g_ref, sc_ref, sh_ref, wg_ref, wv_ref,
               cwg_ref, cwv_ref, cbg_ref, cbv_ref, o_ref, xm_ref, *, tm, group_rows, chunk):
    halo = BF16_ROWS
    i = pl.program_id(0)
    j = pl.program_id(1)

    @pl.when(j == 0)
    def _():
        g, sc, sh = g_ref[...], sc_ref[0], sh_ref[0]
        xm_ref[0:halo] = _norm_mod_tm(xp_ref[...], g, sc, sh).astype(BF16)
        xm_ref[halo:halo + tm] = _norm_mod_tm(x_ref[...], g, sc, sh).astype(BF16)
        xm_ref[halo + tm:] = _norm_mod_tm(xn_ref[...], g, sc, sh).astype(BF16)

        @pl.when((i * tm) % group_rows == 0)
        def _():
            xm_ref[0:halo] = jnp.zeros((halo, xm_ref.shape[1]), BF16)

        @pl.when(((i + 1) * tm) % group_rows == 0)
        def _():
            xm_ref[halo + tm:] = jnp.zeros((halo, xm_ref.shape[1]), BF16)

    xm = xm_ref[...]

    def conv(h, cw_ref, cb_ref, cols):
        cw = cw_ref[:, cols]
        lo = halo - SUBLANE
        return (h[lo:lo + tm] * cw[0:1] + h[halo:halo + tm] * cw[1:2]
                + h[halo + SUBLANE:halo + SUBLANE + tm] * cw[2:3] + cb_ref[:, cols])

    for c in range(o_ref.shape[1] // chunk):
        cols = slice(c * chunk, (c + 1) * chunk)
        gate = conv(jnp.dot(xm, wg_ref[:, cols], preferred_element_type=F32), cwg_ref, cbg_ref, cols)
        val = conv(jnp.dot(xm, wv_ref[:, cols], preferred_element_type=F32), cwv_ref, cbv_ref, cols)
        o_ref[:, cols] = (gate * jax.nn.sigmoid(gate) * val).astype(o_ref.dtype)


def _conv_ffn_up(x, g, sc, sh, w_up, conv_w, conv_b, seq_len):
    m, d = x.shape
    d_ff = w_up.shape[1] // 2
    halo = BF16_ROWS
    group_rows = seq_len * SUBLANE
    tm = _tile(group_rows, 1024, halo)
    tn = _tile(d_ff, 512, LANE)
    chunk = _tile(tn, MXU_DIM, LANE)
    nj = d_ff // tn
    hb = tm // halo
    last_hb = m // halo - 1

    mod_spec = pl.BlockSpec((1, SUBLANE, d), lambda i, j: (i * tm // group_rows, 0, 0))
    return pl.pallas_call(
        functools.partial(_up_kernel, tm=tm, group_rows=group_rows, chunk=chunk),
        grid=(m // tm, nj),
        in_specs=[pl.BlockSpec((tm, d), lambda i, j: (i, 0)),
                  pl.BlockSpec((halo, d), lambda i, j: (jnp.maximum(i * hb - 1, 0), 0)),
                  pl.BlockSpec((halo, d), lambda i, j: (jnp.minimum((i + 1) * hb, last_hb), 0)),
                  pl.BlockSpec((1, d), lambda i, j: (0, 0)),
                  mod_spec, mod_spec,
                  pl.BlockSpec((d, tn), lambda i, j: (0, j)),
                  pl.BlockSpec((d, tn), lambda i, j: (0, nj + j)),
                  pl.BlockSpec((3, tn), lambda i, j: (0, j)),
                  pl.BlockSpec((3, tn), lambda i, j: (0, nj + j)),
                  pl.BlockSpec((1, tn), lambda i, j: (0, j)),
                  pl.BlockSpec((1, tn), lambda i, j: (0, nj + j))],
        out_specs=pl.BlockSpec((tm, tn), lambda i, j: (i, j)),
        out_shape=jax.ShapeDtypeStruct((m, d_ff), BF16),
        scratch_shapes=[pltpu.VMEM((tm + 2 * halo, d), BF16)],
        compiler_params=_params("parallel", "arbitrary"),
        name="conv_ffn_up",
    )(x, x, x, g, sc, sh, w_up, w_up, conv_w, conv_w, conv_b, conv_b)


def _discretize(a_re, a_im, log_step):
    dt = jnp.exp(log_step)
    mag = jnp.exp(a_re * dt)
    lb_re = mag * jnp.cos(a_im * dt)
    lb_im = mag * jnp.sin(a_im * dt)
    num_re = lb_re - 1.0
    den = a_re * a_re + a_im * a_im
    f_re = (num_re * a_re + lb_im * a_im) / den
    f_im = (lb_im * a_re - num_re * a_im) / den
    return lb_re, lb_im, f_re, f_im


def _s5_disc_kernel(a_re_ref, a_im_ref, ls_ref, ax_re_ref, ax_im_ref, lsx_ref, b_re_ref, b_im_ref,
                    lb_re_ref, lb_im_ref, bb_re_ref, bb_im_ref):
    lb_re, lb_im, _, _ = _discretize(a_re_ref[...], a_im_ref[...], ls_ref[...])
    lb_re_ref[...] = lb_re
    lb_im_ref[...] = lb_im
    _, _, f_re, f_im = _discretize(ax_re_ref[...], ax_im_ref[...], lsx_ref[...])
    b_re, b_im = b_re_ref[...], b_im_ref[...]
    bb_re_ref[...] = f_re * b_re - f_im * b_im
    bb_im_ref[...] = f_re * b_im + f_im * b_re


def _s5_discretize(a_re, a_im, log_step, b_re, b_im):
    two, g, p, c = b_re.shape
    rows = two * g
    a2 = lambda a: a.reshape(rows, p)
    ax = lambda a: jnp.repeat(a.reshape(rows, p), c, axis=-1)
    ls = jnp.broadcast_to(log_step.reshape(rows, 1), (rows, p))
    lsx = jnp.broadcast_to(log_step.reshape(rows, 1), (rows, p * c))
    return pl.pallas_call(
        _s5_disc_kernel,
        out_shape=[jax.ShapeDtypeStruct((rows, p), F32)] * 2 + [jax.ShapeDtypeStruct((rows, p * c), F32)] * 2,
        compiler_params=pltpu.CompilerParams(vmem_limit_bytes=VMEM_LIMIT_BYTES),
        name="s5_discretize",
    )(a2(a_re), a2(a_im), ls, ax(a_re), ax(a_im), lsx, b_re.reshape(rows, p * c), b_im.reshape(rows, p * c))


def _s5_in_kernel(x_ref, g_ref, sc_ref, sh_ref, o_ref):
    o_ref[...] = _norm_mod_tm(x_ref[...], g_ref[...], sc_ref[0], sh_ref[0]).astype(o_ref.dtype)


def _s5_input(x, g, sc, sh, seq_len):
    m, d = x.shape
    group_rows = seq_len * SUBLANE
    tm = _tile(group_rows, 1024, BF16_ROWS)
    mod_spec = pl.BlockSpec((1, SUBLANE, d), lambda i: (i * tm // group_rows, 0, 0))
    return pl.pallas_call(
        _s5_in_kernel,
        grid=(m // tm,),
        in_specs=[pl.BlockSpec((tm, d), lambda i: (i, 0)), pl.BlockSpec((1, d), lambda i: (0, 0)),
                  mod_spec, mod_spec],
        out_specs=pl.BlockSpec((tm, d), lambda i: (i, 0)),
        out_shape=jax.ShapeDtypeStruct((m, d), BF16),
        compiler_params=_params("parallel"),
        name="s5_input",
    )(x, g, sc, sh)


def _s5_scan_kernel(u_ref, wb_ref, wc_ref, lam_ref, h0_ref, y_ref, fin_ref, st_ref, buf_ref,
                    *, nsub, nt, sub_t):
    direction = pl.program_id(1)
    t_idx = pl.program_id(3)
    n_tiles = st_ref.shape[0]
    half = n_tiles // 2
    sr = sub_t * SUBLANE
    steps = sub_t // n_tiles

    @pl.when(t_idx == 0)
    def _():
        st_ref[...] = h0_ref[0, 0, 0]

    lam = jnp.broadcast_to(lam_ref[0, 0], st_ref.shape)
    a_re, a_im = lam[:half], lam[half:]

    def first_row(p, d):
        row = (p + d * (nsub - 1 - 2 * p)) * sr
        return row if isinstance(row, int) else pl.multiple_of(row, sr)

    def run_pass(d, p, state, do_bu, do_scan, do_ch, phase=None):
        phase = p % 3 if phase is None else phase
        slot_bu, slot_scan, slot_ch = phase, (phase + 2) % 3, (phase + 1) % 3
        if do_ch:
            h_all = jnp.concatenate([buf_ref[slot_ch, i] for i in range(n_tiles)], axis=1)
            y_ref[0, pl.ds(first_row(p - 2, d), sr), :] = jnp.dot(h_all.astype(BF16), wc_ref[0, 0],
                                                                  preferred_element_type=F32)
        if do_bu:
            u_sub = u_ref[pl.ds(first_row(p, d), sr), :]
        h_re, h_im = state
        for i in range(n_tiles):
            if do_bu:
                buf_ref[slot_bu, i] = jnp.dot(u_sub, wb_ref[0, 0, i], preferred_element_type=F32)
            if do_scan:
                for k in range(steps):
                    t = i * steps + k
                    row = (t + d * (sub_t - 1 - 2 * t)) * SUBLANE
                    bu = buf_ref[slot_scan, :, pl.ds(row, SUBLANE), :]
                    n_re = a_re * h_re - a_im * h_im + bu[:half]
                    n_im = a_re * h_im + a_im * h_re + bu[half:]
                    buf_ref[slot_scan, 0:half, pl.ds(row, SUBLANE), :] = n_re
                    buf_ref[slot_scan, half:, pl.ds(row, SUBLANE), :] = n_im
                    h_re, h_im = n_re, n_im
        return h_re, h_im

    def scan_block(d):
        state = (st_ref[0:half], st_ref[half:])
        for p in range(min(2, nsub + 2)):
            state = run_pass(d, p, state, p < nsub, 1 <= p <= nsub, False)
        n_trips = max(nsub - 2, 0) // 3

        def three_passes(trip, st):
            for k in range(3):
                st = run_pass(d, 2 + 3 * trip + k, st, True, True, True, phase=(2 + k) % 3)
            return st

        if n_trips > 0:
            state = lax.fori_loop(0, n_trips, three_passes, state)
        for p in range(2 + 3 * n_trips, nsub):
            state = run_pass(d, p, state, True, True, True)
        for p in range(max(2, nsub), nsub + 2):
            state = run_pass(d, p, state, p < nsub, p <= nsub, True)
        st_ref[0:half] = state[0]
        st_ref[half:] = state[1]

    for static_d in (0, 1):
        pl.when(direction == static_d)(functools.partial(scan_block, static_d))

    @pl.when(t_idx == nt - 1)
    def _():
        fin_ref[0, 0, 0] = st_ref[...]


def _s5_scan(u, wb, wc, lam, h0, seq_len):
    rows, d = u.shape
    ns = rows // (seq_len * SUBLANE)
    ngb = d // MXU_DIM
    n_tiles = wb.shape[2]
    sub_t = _tile(seq_len, 64, n_tiles)
    tc = _tile(seq_len, 16 * sub_t, sub_t)
    nt = seq_len // tc
    assert sub_t % n_tiles == 0 and tc % sub_t == 0

    def t_blk(n, dd, t):
        return n * nt + t + dd * (nt - 1 - 2 * t)

    w_spec = pl.BlockSpec((1, 1, n_tiles, MXU_DIM, MXU_DIM), lambda n, dd, gb, t: (dd, gb, 0, 0, 0))
    st_spec = pl.BlockSpec((1, 1, 1, n_tiles, SUBLANE, MXU_DIM), lambda n, dd, gb, t: (n, dd, gb, 0, 0, 0))
    return pl.pallas_call(
        functools.partial(_s5_scan_kernel, nsub=tc // sub_t, nt=nt, sub_t=sub_t),
        grid=(ns, 2, ngb, nt),
        in_specs=[pl.BlockSpec((tc * SUBLANE, MXU_DIM), lambda n, dd, gb, t: (t_blk(n, dd, t), gb)),
                  w_spec,
                  pl.BlockSpec((1, 1, n_tiles * MXU_DIM, MXU_DIM), lambda n, dd, gb, t: (dd, gb, 0, 0)),
                  pl.BlockSpec((1, 1, n_tiles, 1, MXU_DIM), lambda n, dd, gb, t: (dd, gb, 0, 0, 0)),
                  st_spec],
        out_specs=[pl.BlockSpec((1, tc * SUBLANE, MXU_DIM), lambda n, dd, gb, t: (dd, t_blk(n, dd, t), gb)),
                   st_spec],
        out_shape=[jax.ShapeDtypeStruct((2, rows, d), F32),
                   jax.ShapeDtypeStruct((ns, 2, ngb, n_tiles, SUBLANE, MXU_DIM), F32)],
        scratch_shapes=[pltpu.VMEM((n_tiles, SUBLANE, MXU_DIM), F32),
                        pltpu.VMEM((3, n_tiles, sub_t * SUBLANE, MXU_DIM), F32)],
        compiler_params=_params("parallel", "parallel", "parallel", "arbitrary"),
        name="s5_scan",
    )(u, wb, wc, lam, h0)


def _s5_block_weights(lb_re, lb_im, bb_re, bb_im, c_re, c_im, n_groups, state_dim, group):
    gpb = MXU_DIM // group
    ngb = n_groups // gpb
    eye = jnp.eye(gpb, dtype=F32)

    def b_blocks(bb):
        bb = bb.reshape(2, ngb, gpb, state_dim, group)
        return jnp.einsum('dbgpc,gh->dbgchp', bb, eye).reshape(2, ngb, gpb * group, gpb * state_dim)

    def c_blocks(cc):
        cc = cc.reshape(2, ngb, gpb, group, state_dim)
        return jnp.einsum('dbgcp,gh->dbgphc', cc, eye).reshape(2, ngb, gpb * state_dim, gpb * group)

    wb = jnp.concatenate([b_blocks(bb_re), b_blocks(bb_im)], axis=-1).astype(BF16)
    wc = jnp.concatenate([c_blocks(c_re), -c_blocks(c_im)], axis=-2).astype(BF16)
    lam = jnp.concatenate([lb_re.reshape(2, ngb, 1, gpb * state_dim),
                           lb_im.reshape(2, ngb, 1, gpb * state_dim)], axis=-1)
    n_tiles = wb.shape[-1] // MXU_DIM
    wb = wb.reshape(2, ngb, MXU_DIM, n_tiles, MXU_DIM).transpose(0, 1, 3, 2, 4)
    lam = lam.reshape(2, ngb, n_tiles, 1, MXU_DIM)
    return wb, wc, lam


def _s5_state_blocks(s_re, s_im, ngb):
    b, two, g, p = s_re.shape
    sw = g * p // ngb

    def blk(s):
        return s.reshape(b // SUBLANE, SUBLANE, two, ngb, sw).transpose(0, 2, 3, 1, 4)

    st = jnp.concatenate([blk(s_re), blk(s_im)], axis=-1)
    st = st.reshape(st.shape[:4] + (2 * sw // MXU_DIM, MXU_DIM))
    return st.transpose(0, 1, 2, 4, 3, 5)


def _s5_state_unblock(fin, g, p):
    ns, two, ngb, n_tiles, sub, lanes = fin.shape
    fin = fin.transpose(0, 1, 2, 4, 3, 5).reshape(ns, two, ngb, sub, n_tiles * lanes)
    sw = n_tiles * lanes // 2

    def unblk(s):
        return s.transpose(0, 3, 1, 2, 4).reshape(ns * sub, two, g, p)

    return unblk(fin[..., :sw]), unblk(fin[..., sw:])


def _glu_kernel(x_ref, yf_ref, yb_ref, g_ref, sc_ref, sh_ref, dskip_ref, w_ref, b_ref, gate_ref, o_ref):
    x = x_ref[...]
    xm = _norm_mod_tm(x, g_ref[...], sc_ref[0], sh_ref[0])
    gz = jax.nn.gelu(yf_ref[0] + yb_ref[0] + dskip_ref[...] * xm)
    acc = jnp.dot(gz.astype(BF16), w_ref[...], preferred_element_type=F32) + b_ref[...]
    o_ref[...] = x + _per_seq(gz * jax.nn.sigmoid(acc), gate_ref[0], jnp.multiply)


def _s5_glu(x, y, g, sc, sh, d_skip, w_glu, b_glu, gate, seq_len):
    m, d = x.shape
    group_rows = seq_len * SUBLANE
    tm = _tile(group_rows, 512, BF16_ROWS)
    mod_spec = pl.BlockSpec((1, SUBLANE, d), lambda i: (i * tm // group_rows, 0, 0))
    vec_spec = pl.BlockSpec((1, d), lambda i: (0, 0))
    return pl.pallas_call(
        _glu_kernel,
        grid=(m // tm,),
        in_specs=[pl.BlockSpec((tm, d), lambda i: (i, 0)),
                  pl.BlockSpec((1, tm, d), lambda i: (0, i, 0)),
                  pl.BlockSpec((1, tm, d), lambda i: (1, i, 0)),
                  vec_spec, mod_spec, mod_spec, vec_spec,
                  pl.BlockSpec((d, d), lambda i: (0, 0), pipeline_mode=pl.Buffered(1)),
                  vec_spec, mod_spec],
        out_specs=pl.BlockSpec((tm, d), lambda i: (i, 0)),
        out_shape=jax.ShapeDtypeStruct((m, d), F32),
        compiler_params=_params("parallel"),
        name="s5_glu",
    )(x, y, y, g, sc, sh, d_skip, w_glu, b_glu, gate)


def _final_norm_kernel(x_ref, g_ref, o_ref):
    nb, tl, d = o_ref.shape
    x = x_ref[...]
    y = x * _rstd(x) * g_ref[...]
    o_ref[...] = pltpu.einshape("tbd->btd", y.reshape(tl, nb, d))


def _final_norm(x, g, batch, seq_len):
    m, d = x.shape
    tl = _tile(seq_len, 64, SUBLANE)
    nl = seq_len // tl
    return pl.pallas_call(
        _final_norm_kernel,
        grid=(batch // SUBLANE, nl),
        in_specs=[pl.BlockSpec((tl * SUBLANE, d), lambda s, t: (s * nl + t, 0)),
                  pl.BlockSpec((1, d), lambda s, t: (0, 0))],
        out_specs=pl.BlockSpec((SUBLANE, tl, d), lambda s, t: (s, t, 0)),
        out_shape=jax.ShapeDtypeStruct((batch, seq_len, d), F32),
        compiler_params=_params("parallel", "parallel"),
        name="final_norm",
    )(x, g)


def kernel(x_prompt, x_sample, cache_k, cache_v, state_re, state_im, c, c_ctx, w_mod, b_mod, norm_g, w_qkv, lam_vecs, subln_g, w_o, ssm_a_re, ssm_a_im, ssm_log_step, ssm_b_re, ssm_b_im, ssm_c_re, ssm_c_im, ssm_d, w_glu, b_glu, w_up, conv_w, conv_b, w_down, final_g):
    bp, lp, d = x_prompt.shape
    bs, ls, _ = x_sample.shape
    depth = w_mod.shape[0]
    n_heads, head_dim = cache_k.shape[3], cache_k.shape[5]
    n_groups, state_dim, group = ssm_b_re.shape[2], ssm_b_re.shape[3], ssm_b_re.shape[4]
    assert bp % SUBLANE == 0 and bs % SUBLANE == 0 and d % MXU_DIM == 0 and MXU_DIM % group == 0
    assert 2 * head_dim == MXU_DIM and ls % GRID_W == 0

    pad_rows = -(bs + 1) % SUBLANE
    cond = jnp.concatenate([c, c_ctx[None], jnp.zeros((pad_rows, d), F32)], axis=0)
    mod = _modulation(cond, w_mod, b_mod).reshape(depth, cond.shape[0], 6, d)

    streams = [dict(x=x_prompt.reshape(bp * lp, d), batch=bp, seq=lp, time_major=False),
               dict(x=x_sample.reshape(bs * ls, d), batch=bs, seq=ls, time_major=False)]
    new_k, new_v, new_sre, new_sim = [], [], [], []
    q_scale = head_dim ** -0.5 * math.log2(math.e)
    rope = _rope_tables(ls, head_dim, q_scale)
    ngb = d // MXU_DIM

    for i in range(depth):
        mix, slot = i % N_MIXERS, i // N_MIXERS
        mods = [[jnp.broadcast_to(mod[i, bs, j], (bp, d)) for j in range(6)],
                [mod[i, :bs, j] for j in range(6)]]
        g1 = norm_g[i, 0][None]
        g2 = norm_g[i, 1][None]

        def per_group(v):
            return v.reshape(v.shape[0] // SUBLANE, SUBLANE, d)

        if mix == 0:
            lam_init = 0.8 - 0.6 * math.exp(-0.3 * i)
            nqk = n_heads * 2 * head_dim
            wq = _to_bf16(w_qkv, slot, swap_cols=2 * nqk, head_dim=head_dim)
            wo = _to_bf16(w_o, slot)
            sub_g = subln_g[slot][None]
            ck = cache_k.reshape(cache_k.shape[:3] + (-1,))
            cv = cache_v.reshape(cache_v.shape[:3] + (-1,))
            for s, (st, md) in enumerate(zip(streams, mods)):
                b, l = st['batch'], st['seq']
                if st['time_major']:
                    st['x'] = _to_batch_major(st['x'], b, l)
                sh1, sc1 = (v[:1, None] if s == 0 else v[:, None] for v in (md[0], md[1]))
                if s == 0:
                    q_p, k_p, v_p = _qkv_proj(st['x'], g1, sc1, sh1, wq, l, q_scale)
                    new_k.append(_swap_middle_quarters(k_p, head_dim).reshape(b, l, n_heads, 2, head_dim))
                    new_v.append(v_p.reshape(b, l, n_heads, 2 * head_dim))
                    o = _diff_attention((q_p, k_p, v_p), lam_vecs[slot], sub_g, lam_init, b, l, n_heads, head_dim)
                else:
                    qkv = _qkv_proj(st['x'], g1, sc1, sh1, wq, l, q_scale, rope_tables=rope)
                    o = _diff_attention(qkv, lam_vecs[slot], sub_g, lam_init, b, l, n_heads, head_dim,
                                        cache_k=ck, cache_v=cv, slot=slot)
                st['x'] = _attn_out_proj(o, wo, st['x'], md[2][:, None], b, l)
                st['time_major'] = True
        else:
            lb_re, lb_im, bb_re, bb_im = _s5_discretize(ssm_a_re[slot], ssm_a_im[slot], ssm_log_step[slot],
                                                        ssm_b_re[slot], ssm_b_im[slot])
            wb, wc, lam = _s5_block_weights(lb_re, lb_im, bb_re, bb_im, ssm_c_re[slot], ssm_c_im[slot],
                                            n_groups, state_dim, group)
            wg = _to_bf16(w_glu, slot)
            for s, (st, md) in enumerate(zip(streams, mods)):
                b, l = st['batch'], st['seq']
                sh1, sc1, gt1 = per_group(md[0]), per_group(md[1]), per_group(md[2])
                u = _s5_input(st['x'], g1, sc1, sh1, l)
                if s == 0:
                    h0 = jnp.zeros((b // SUBLANE, 2, ngb, wb.shape[2], SUBLANE, MXU_DIM), F32)
                else:
                    h0 = _s5_state_blocks(state_re[:, slot], state_im[:, slot], ngb)
                y, fin = _s5_scan(u, wb, wc, lam, h0, l)
                if s == 0:
                    f_re, f_im = _s5_state_unblock(fin, n_groups, state_dim)
                    new_sre.append(f_re)
                    new_sim.append(f_im)
                st['x'] = _s5_glu(st['x'], y, g1, sc1, sh1, ssm_d[slot][None], wg, b_glu[slot][None], gt1, l)
        wu = _to_bf16(w_up, i)
        wd = _to_bf16(w_down, i)
        for st, md in zip(streams, mods):
            sh2, sc2, gt2 = per_group(md[3]), per_group(md[4]), per_group(md[5])
            act = _conv_ffn_up(st['x'], g2, sc2, sh2, wu, conv_w[i], conv_b[i][None], st['seq'])
            st['x'] = _matmul_gate_residual(act, wd, st['x'], gt2, st['seq'])

    y_prompt = _final_norm(streams[0]['x'], final_g[None], bp, lp)
    y_sample = _final_norm(streams[1]['x'], final_g[None], bs, ls)
    return (y_prompt, y_sample, jnp.stack(new_k, axis=1), jnp.stack(new_v, axis=1),
            jnp.stack(new_sre, axis=1), jnp.stack(new_sim, axis=1))
```
